```python
import math
import jax, jax.numpy as jnp
from jax import lax
import numpy as np

D_MODEL = 1024
BATCH = 4
SEQ = 4096
DEPTH = 4

CHUNK = 64
Q_BLOCK = 128
ROPE_THETA = 500000.0
EPS = 1e-6
NEG_INF = -1e30
TINY = 1e-30
D_FF = 2816
MLA_HEADS = 4
MLA_NOPE = 128
MLA_ROPE = 64
MLA_V = 128
MLA_Q_RANK = 384
MLA_KV_RANK = 256
HG_HEADS = 4
HG_DK = 128
HG_DV = 128
DF_HEADS = 8
DF_DH = 64
DF_ROT = DF_DH // 4
N_EVEN = (DEPTH + 1) // 2
N_ODD = DEPTH // 2
EVEN_SPLITS = (MLA_Q_RANK, MLA_KV_RANK, MLA_ROPE, HG_HEADS * HG_DK, HG_HEADS * HG_DK, HG_HEADS * HG_DV, HG_HEADS * HG_DV)
EVEN_IN = sum(EVEN_SPLITS)
EVEN_MIX = MLA_HEADS * MLA_V + HG_HEADS * HG_DV
ODD_IN = 3 * DF_HEADS * 2 * DF_DH
ODD_MIX = DF_HEADS * 2 * DF_DH

kernel_name = 'hybrid_mla_hgrn2_diffattn_macaron'


def rms_norm(x, g):
    xf = x.astype(jnp.float32)
    y = xf * lax.rsqrt(jnp.mean(xf * xf, axis=-1, keepdims=True) + EPS)
    return (y * g.astype(jnp.float32)).astype(x.dtype)


def rope_tables(positions, dim):
    inv_freq = ROPE_THETA ** (-jnp.arange(0, dim, 2, dtype=jnp.float32) / dim)
    ang = positions.astype(jnp.float32)[..., None] * inv_freq
    return jnp.cos(ang), jnp.sin(ang)


def apply_rope(x, cos, sin):
    half = cos.shape[-1]
    shape = cos.shape[:2] + (1,) * (x.ndim - 3) + (half,)
    c = cos.reshape(shape)
    s = sin.reshape(shape)
    xf = x.astype(jnp.float32)
    x1, x2 = xf[..., :half], xf[..., half:]
    return jnp.concatenate([x1 * c - x2 * s, x2 * c + x1 * s], axis=-1).astype(x.dtype)


def partial_rope(x, cos, sin):
    rot = 2 * cos.shape[-1]
    return jnp.concatenate([apply_rope(x[..., :rot], cos, sin), x[..., rot:]], axis=-1)


def swiglu(x, w_gate, w_up, w_down):
    return (jax.nn.silu(x @ w_gate) * (x @ w_up)) @ w_down


def block_causal_mask(blk, seq):
    q_idx = blk * Q_BLOCK + jnp.arange(Q_BLOCK)
    k_idx = jnp.arange(seq)
    return (k_idx[None, :] // CHUNK) <= (q_idx[:, None] // CHUNK)


def to_query_blocks(t):
    b, s = t.shape[:2]
    return jnp.moveaxis(t.reshape((b, s // Q_BLOCK, Q_BLOCK) + t.shape[2:]), 1, 0)


def from_query_blocks(o):
    nb, b, qb = o.shape[:3]
    return jnp.moveaxis(o, 0, 1).reshape((b, nb * qb) + o.shape[3:])


def chunk_causal_attention(q, k, v, scale):
    seq = k.shape[1]

    def one_block(args):
        qb, blk = args
        s = jnp.einsum('bqhd,bkhd->bhqk', qb, k).astype(jnp.float32) * scale
        s = jnp.where(block_causal_mask(blk, seq), s, NEG_INF)
        p = jax.nn.softmax(s, axis=-1).astype(v.dtype)
        return jnp.einsum('bhqk,bkhd->bqhd', p, v)

    o = lax.map(one_block, (to_query_blocks(q), jnp.arange(seq // Q_BLOCK)))
    return from_query_blocks(o)


def chunk_causal_diff_attention(q, k, v, lam, scale):
    seq = k.shape[1]

    def one_block(args):
        qb, blk = args
        s = jnp.einsum('bqhcd,bkhcd->bchqk', qb, k).astype(jnp.float32) * scale
        s = jnp.where(block_causal_mask(blk, seq), s, NEG_INF)
        p = jax.nn.softmax(s, axis=-1)
        p = (p[:, 0] - lam * p[:, 1]).astype(v.dtype)
        return jnp.einsum('bhqk,bkhd->bqhd', p, v)

    o = lax.map(one_block, (to_query_blocks(q), jnp.arange(seq // Q_BLOCK)))
    return from_query_blocks(o)


def hgrn2_chunkwise(q, k, logf, v):
    b_, s_, h_, dk = q.shape
    dv = v.shape[-1]
    n = s_ // CHUNK

    def to_chunks(t):
        return jnp.moveaxis(t.reshape(b_, n, CHUNK, h_, t.shape[-1]), (1, 3), (0, 2))

    tril = jnp.tril(jnp.ones((CHUNK, CHUNK), dtype=bool))[:, :, None]

    def step(state, inp):
        qc, kc, gc, vc = inp
        bcum = jnp.cumsum(gc, axis=2)
        o_inter = jnp.einsum('bhtk,bhkv->bhtv', qc * jnp.exp(bcum), state)
        rel = bcum[:, :, :, None, :] - bcum[:, :, None, :, :]
        decay = jnp.where(tril, jnp.exp(jnp.where(tril, rel, 0.0)), 0.0)
        attn = jnp.einsum('bhtk,bhsk,bhtsk->bhts', qc, kc, decay)
        o_intra = jnp.einsum('bhts,bhsv->bhtv', attn, vc)
        b_last = bcum[:, :, -1:, :]
        k_dec = kc * jnp.exp(b_last - bcum)
        new_state = jnp.exp(b_last[:, :, 0, :])[..., None] * state + jnp.einsum('bhsk,bhsv->bhkv', k_dec, vc)
        return new_state, o_inter + o_intra

    init = jnp.zeros((b_, h_, dk, dv), jnp.float32)
    _, o = lax.scan(step, init, (to_chunks(q), to_chunks(k), to_chunks(logf), to_chunks(v)))
    return jnp.moveaxis(o, (0, 2), (1, 3)).reshape(b_, s_, h_, dv)


def even_mixer(h, cos_m, sin_m, w_in, g_q, w_uq, g_kv, w_ukv, lb, g_out, w_out):
    b_, s_, _ = h.shape
    z = h @ w_in
    idx = np.cumsum(EVEN_SPLITS)[:-1].tolist()
    c_q, c_kv, k_pe, hq, hf, hi, hg = jnp.split(z, idx, axis=-1)
    q = (rms_norm(c_q, g_q) @ w_uq).reshape(b_, s_, MLA_HEADS, MLA_NOPE + MLA_ROPE)
    q = jnp.concatenate([q[..., :MLA_NOPE], apply_rope(q[..., MLA_NOPE:], cos_m, sin_m)], axis=-1)
    kv = (rms_norm(c_kv, g_kv) @ w_ukv).reshape(b_, s_, MLA_HEADS, MLA_NOPE + MLA_V)
    k_nope, v = kv[..., :MLA_NOPE], kv[..., MLA_NOPE:]
    k_pe = apply_rope(k_pe.reshape(b_, s_, 1, MLA_ROPE), cos_m, sin_m)
    k = jnp.concatenate([k_nope, jnp.broadcast_to(k_pe, (b_, s_, MLA_HEADS, MLA_ROPE))], axis=-1)
    o_a = chunk_causal_attention(q, k, v, (MLA_NOPE + MLA_ROPE) ** -0.5)
    zf = hf.astype(jnp.float32).reshape(b_, s_, HG_HEADS, HG_DK)
    sig = jax.nn.sigmoid(zf)
    logf = jnp.log(jnp.maximum(lb + (1.0 - lb) * sig, TINY))
    kk = (1.0 - lb) * (1.0 - sig)
    qq = hq.astype(jnp.float32).reshape(b_, s_, HG_HEADS, HG_DK)
    vv = hi.astype(jnp.float32).reshape(b_, s_, HG_HEADS, HG_DV)
    o_b = hgrn2_chunkwise(qq, kk, logf, vv)
    gate = jax.nn.silu(hg.astype(jnp.float32).reshape(b_, s_, HG_HEADS, HG_DV))
    o_b = (rms_norm(o_b, g_out) * gate).astype(h.dtype)
    o = jnp.concatenate([o_a.reshape(b_, s_, -1), o_b.reshape(b_, s_, -1)], axis=-1)
    return o @ w_out


def odd_mixer(h, cos_d, sin_d, w_in, lam_p, g_head, w_out, lambda_init):
    b_, s_, _ = h.shape
    z = h @ w_in
    q, k, v = jnp.split(z, 3, axis=-1)
    q = partial_rope(q.reshape(b_, s_, DF_HEADS, 2, DF_DH), cos_d, sin_d)
    k = partial_rope(k.reshape(b_, s_, DF_HEADS, 2, DF_DH), cos_d, sin_d)
    v = v.reshape(b_, s_, DF_HEADS, 2 * DF_DH)
    lp = lam_p.astype(jnp.float32)
    lam = jnp.exp(jnp.sum(lp[0] * lp[1])) - jnp.exp(jnp.sum(lp[2] * lp[3])) + lambda_init
    o = chunk_causal_diff_attention(q, k, v, lam, DF_DH ** -0.5)
    o = rms_norm(o, g_head) * (1.0 - lambda_init)
    return o.reshape(b_, s_, -1) @ w_out


def setup_inputs(seed: int = 0) -> dict:
    key = jax.random.key(seed)
    ks = jax.random.split(key, 20)

    def nrm(k, shape, fan_in):
        return jax.random.normal(k, shape, jnp.float32) * fan_in ** -0.5

    def gain(k, shape):
        return 1.0 + 0.02 * jax.random.normal(k, shape, jnp.float32)

    x = jax.random.normal(ks[0], (BATCH, SEQ, D_MODEL), jnp.float32)
    offsets = jax.random.randint(ks[1], (BATCH, 1), 0, 64, dtype=jnp.int32) * CHUNK
    positions = (offsets + jnp.arange(SEQ, dtype=jnp.int32)[None, :]).astype(jnp.int32)
    return {
        'x': x,
        'positions': positions,
        'norm_g': gain(ks[2], (DEPTH, 3, 2, D_MODEL)),
        'ffn_w_gate': nrm(ks[3], (DEPTH, 2, D_MODEL, D_FF), D_MODEL),
        'ffn_w_up': nrm(ks[4], (DEPTH, 2, D_MODEL, D_FF), D_MODEL),
        'ffn_w_down': nrm(ks[5], (DEPTH, 2, D_FF, D_MODEL), D_FF),
        'ev_w_in': nrm(ks[6], (N_EVEN, D_MODEL, EVEN_IN), D_MODEL),
        'ev_g_q': gain(ks[7], (N_EVEN, MLA_Q_RANK)),
        'ev_w_uq': nrm(ks[8], (N_EVEN, MLA_Q_RANK, MLA_HEADS * (MLA_NOPE + MLA_ROPE)), MLA_Q_RANK),
        'ev_g_kv': gain(ks[9], (N_EVEN, MLA_KV_RANK)),
        'ev_w_ukv': nrm(ks[10], (N_EVEN, MLA_KV_RANK, MLA_HEADS * (MLA_NOPE + MLA_V)), MLA_KV_RANK),
        'ev_lb_logits': jax.random.normal(ks[11], (N_EVEN, HG_HEADS * HG_DK), jnp.float32),
        'ev_g_out': gain(ks[12], (N_EVEN, HG_HEADS, HG_DV)),
        'ev_w_out': nrm(ks[13], (N_EVEN, EVEN_MIX, D_MODEL), EVEN_MIX),
        'od_w_in': nrm(ks[14], (N_ODD, D_MODEL, ODD_IN), D_MODEL),
        'od_lambda': 0.1 * jax.random.normal(ks[15], (N_ODD, 4, DF_DH), jnp.float32),
        'od_g_head': gain(ks[16], (N_ODD, DF_HEADS, 2 * DF_DH)),
        'od_w_out': nrm(ks[17], (N_ODD, ODD_MIX, D_MODEL), ODD_MIX),
    }


def reference(x, positions, norm_g, ffn_w_gate, ffn_w_up, ffn_w_down, ev_w_in, ev_g_q, ev_w_uq, ev_g_kv, ev_w_ukv, ev_lb_logits, ev_g_out, ev_w_out, od_w_in, od_lambda, od_g_head, od_w_out):
    cos_m, sin_m = rope_tables(positions, MLA_ROPE)
    cos_d, sin_d = rope_tables(positions, DF_ROT)
    lb_w = jax.nn.softmax(ev_lb_logits.astype(jnp.float32), axis=0)
    lb_all = jnp.cumsum(lb_w, axis=0) - lb_w[0:1]
    for l in range(DEPTH):
        g = norm_g[l]
        h = swiglu(rms_norm(x, g[0, 0]), ffn_w_gate[l, 0], ffn_w_up[l, 0], ffn_w_down[l, 0])
        x = x + 0.5 * rms_norm(h, g[0, 1])
        h = rms_norm(x, g[1, 0])
        if l % 2 == 0:
            j = l // 2
            lb = lb_all[j].reshape(HG_HEADS, HG_DK)
            m = even_mixer(h, cos_m, sin_m, ev_w_in[j], ev_g_q[j], ev_w_uq[j], ev_g_kv[j], ev_w_ukv[j], lb, ev_g_out[j], ev_w_out[j])
        else:
            j = l // 2
            lambda_init = 0.8 - 0.6 * math.exp(-0.3 * l)
            m = odd_mixer(h, cos_d, sin_d, od_w_in[j], od_lambda[j], od_g_head[j], od_w_out[j], lambda_init)
        x = x + rms_norm(m, g[1, 1])
        h = swiglu(rms_norm(x, g[2, 0]), ffn_w_gate[l, 1], ffn_w_up[l, 1], ffn_w_down[l, 1])
        x = x + 0.5 * rms_norm(h, g[2, 1])
    return x
```

```python
import functools
import math

import jax
import jax.numpy as jnp
from jax import lax
from jax.experimental import pallas as pl
from jax.experimental.pallas import tpu as pltpu

D_MODEL = 1024
DEPTH = 4
CHUNK = 64
ROPE_THETA = 500000.0
EPS = 1e-6
NEG_INF = -1e30
TINY = 1e-30
D_FF = 2816
MLA_HEADS = 4
MLA_NOPE = 128
MLA_ROPE = 64
MLA_V = 128
MLA_Q_RANK = 384
MLA_KV_RANK = 256
HG_HEADS = 4
HG_DK = 128
HG_DV = 128
DF_HEADS = 8
DF_DH = 64
DF_ROT = DF_DH // 4

LANES = 128
MXU_DIM = 256
VMEM_LIMIT = 56 * 1024 * 1024

TOKEN_TILE = 512
ATTN_TILE = 256
HG_SEQ_TILE = 1024
HG_SUB = 16
FF_CHUNKS = ((0, 1024), (1024, 2048), (2048, D_FF))

MLA_QK_PAD = 2 * LANES
EVEN_Z = MLA_Q_RANK + MLA_KV_RANK + LANES + 4 * HG_HEADS * HG_DK

F32 = jnp.float32
BF16 = jnp.bfloat16


def _rms(x, g):
    ms = jnp.mean(x * x, axis=-1, keepdims=True)
    return x * lax.rsqrt(ms + EPS) * g


def _dot(a, b):
    return jnp.dot(a, b, preferred_element_type=F32)


def _dot_nt(a, b):
    return lax.dot_general(a, b, (((1,), (1,)), ((), ())), preferred_element_type=F32)


def _dot_tn(a, b):
    return lax.dot_general(a, b, (((0,), (0,)), ((), ())), preferred_element_type=F32)


def _rope_slab(x, c, sa, sb, half):
    return x * c + pltpu.roll(x, LANES - half, 1) * sa + pltpu.roll(x, half, 1) * sb


def _const_spec(shape):
    nd = len(shape)
    return pl.BlockSpec(shape, lambda *_: (0,) * nd, pipeline_mode=pl.Buffered(1))


def _params(*sem):
    return pltpu.CompilerParams(dimension_semantics=sem, vmem_limit_bytes=VMEM_LIMIT)


def _ffn_apply(x, gpre, gpost, wg_ref, wu_ref, wd_ref):
    xn = _rms(x, gpre).astype(BF16)
    acc = None
    for lo, hi in FF_CHUNKS:
        g = _dot(xn, wg_ref[:, lo:hi])
        u = _dot(xn, wu_ref[:, lo:hi])
        a = (jax.nn.silu(g) * u).astype(BF16)
        h = _dot(a, wd_ref[lo:hi, :])
        acc = h if acc is None else acc + h
    return x + 0.5 * _rms(acc, gpost)


def _ffn_kernel(x_ref, g_ref, wg_ref, wu_ref, wd_ref, o_ref):
    g = g_ref[...]
    o_ref[...] = _ffn_apply(x_ref[...], g[0:1], g[1:2], wg_ref, wu_ref, wd_ref)


def _ffn_call(x, g2, wg, wu, wd):
    t = x.shape[0]
    tile = pl.BlockSpec((TOKEN_TILE, D_MODEL), lambda i: (i, 0))
    return pl.pallas_call(
        _ffn_kernel,
        grid=(t // TOKEN_TILE,),
        in_specs=[tile, _const_spec((2, D_MODEL)), _const_spec(wg.shape),
                  _const_spec(wu.shape), _const_spec(wd.shape)],
        out_specs=tile,
        out_shape=jax.ShapeDtypeStruct(x.shape, F32),
        compiler_params=_params("parallel"),
        name="ffn",
    )(x, g2, wg, wu, wd)


def _mix_ffn_kernel(n_parts, *refs):
    x_ref = refs[0]
    part_refs = refs[1:1 + n_parts]
    wo_ref, g_ref, wg_ref, wu_ref, wd_ref, o_ref = refs[1 + n_parts:]
    g = g_ref[...]
    m = None
    row = 0
    for p_ref in part_refs:
        w = p_ref.shape[-1]
        d = _dot(p_ref[...], wo_ref[row:row + w, :])
        m = d if m is None else m + d
        row += w
    x1 = x_ref[...] + _rms(m, g[0:1])
    o_ref[...] = _ffn_apply(x1, g[1:2], g[2:3], wg_ref, wu_ref, wd_ref)


def _mix_ffn_call(x, parts, wo, g3, wg, wu, wd):
    t = x.shape[0]
    tile = pl.BlockSpec((TOKEN_TILE, D_MODEL), lambda i: (i, 0))
    part_specs = [pl.BlockSpec((TOKEN_TILE, p.shape[-1]), lambda i: (i, 0)) for p in parts]
    return pl.pallas_call(
        functools.partial(_mix_ffn_kernel, len(parts)),
        grid=(t // TOKEN_TILE,),
        in_specs=[tile] + part_specs + [_const_spec(wo.shape), _const_spec((3, D_MODEL)),
                                        _const_spec(wg.shape), _const_spec(wu.shape),
                                        _const_spec(wd.shape)],
        out_specs=tile,
        out_shape=jax.ShapeDtypeStruct(x.shape, F32),
        compiler_params=_params("parallel"),
        name="mix_ffn",
    )(x, *parts, wo, g3, wg, wu, wd)


def _even_pre_kernel(x_ref, g_ref, win_ref, gq_ref, wuq_ref, gkv_ref, wukv_ref,
                     c_ref, sa_ref, sb_ref, q_ref, k_ref, v_ref, zh_ref):
    h = _rms(x_ref[...], g_ref[...]).astype(BF16)
    z = _dot(h, win_ref[...])
    o_kv = MLA_Q_RANK
    o_pe = o_kv + MLA_KV_RANK
    o_h = o_pe + LANES
    zh_ref[...] = z[:, o_h:]
    c, sa, sb = c_ref[...], sa_ref[...], sb_ref[...]
    half = MLA_ROPE // 2
    scale = (MLA_NOPE + MLA_ROPE) ** -0.5

    cq = _rms(z[:, :o_kv], gq_ref[...]).astype(BF16)
    q = _dot(cq, wuq_ref[...]) * scale
    for hd in range(MLA_HEADS):
        base = hd * MLA_QK_PAD
        q_ref[:, base:base + LANES] = q[:, base:base + LANES].astype(BF16)
        q_ref[:, base + LANES:base + 2 * LANES] = _rope_slab(
            q[:, base + LANES:base + 2 * LANES], c, sa, sb, half).astype(BF16)

    ckv = _rms(z[:, o_kv:o_pe], gkv_ref[...]).astype(BF16)
    kv = _dot(ckv, wukv_ref[...])
    kpe = _rope_slab(z[:, o_pe:o_h], c, sa, sb, half).astype(BF16)
    for hd in range(MLA_HEADS):
        base = hd * MLA_QK_PAD
        k_ref[:, base:base + LANES] = kv[:, hd * LANES:(hd + 1) * LANES].astype(BF16)
        k_ref[:, base + LANES:base + 2 * LANES] = kpe
    v_ref[...] = kv[:, MLA_HEADS * MLA_NOPE:].astype(BF16)


def _even_pre_call(x, g, win, gq, wuq, gkv, wukv, tabs):
    t = x.shape[0]
    row = lambda w: pl.BlockSpec((TOKEN_TILE, w), lambda i: (i, 0))
    qk_w = MLA_HEADS * MLA_QK_PAD
    v_w = MLA_HEADS * MLA_V
    zh_w = 4 * HG_HEADS * HG_DK
    return pl.pallas_call(
        _even_pre_kernel,
        grid=(t // TOKEN_TILE,),
        in_specs=[row(D_MODEL), _const_spec((1, D_MODEL)), _const_spec(win.shape),
                  _const_spec(gq.shape), _const_spec(wuq.shape), _const_spec(gkv.shape),
                  _const_spec(wukv.shape), row(LANES), row(LANES), row(LANES)],
        out_specs=[row(qk_w), row(qk_w), row(v_w), row(zh_w)],
        out_shape=[jax.ShapeDtypeStruct((t, qk_w), BF16), jax.ShapeDtypeStruct((t, qk_w), BF16),
                   jax.ShapeDtypeStruct((t, v_w), BF16), jax.ShapeDtypeStruct((t, zh_w), F32)],
        compiler_params=_params("parallel"),
        name="even_pre",
    )(x, g, win, gq, wuq, gkv, wukv, *tabs)


def _odd_pre_kernel(x_ref, g_ref, win_ref, c_ref, sa_ref, sb_ref, q_ref, k_ref, v_ref):
    h = _rms(x_ref[...], g_ref[...]).astype(BF16)
    c, sa, sb = c_ref[...], sa_ref[...], sb_ref[...]
    half = DF_ROT // 2
    width = DF_HEADS * 2 * DF_DH
    scale = DF_DH ** -0.5
    q = _dot(h, win_ref[:, :width]) * scale
    k = _dot(h, win_ref[:, width:2 * width])
    for j in range(width // LANES):
        sl = slice(j * LANES, (j + 1) * LANES)
        q_ref[:, sl] = _rope_slab(q[:, sl], c, sa, sb, half).astype(BF16)
        k_ref[:, sl] = _rope_slab(k[:, sl], c, sa, sb, half).astype(BF16)
    v_ref[...] = _dot(h, win_ref[:, 2 * width:]).astype(BF16)


def _odd_pre_call(x, g, win, tabs):
    t = x.shape[0]
    width = DF_HEADS * 2 * DF_DH
    row = lambda w: pl.BlockSpec((TOKEN_TILE, w), lambda i: (i, 0))
    return pl.pallas_call(
        _odd_pre_kernel,
        grid=(t // TOKEN_TILE,),
        in_specs=[row(D_MODEL), _const_spec((1, D_MODEL)), _const_spec(win.shape),
                  row(LANES), row(LANES), row(LANES)],
        out_specs=[row(width)] * 3,
        out_shape=[jax.ShapeDtypeStruct((t, width), BF16)] * 3,
        compiler_params=_params("parallel"),
        name="odd_pre",
    )(x, g, win, *tabs)


def _flash(q, q_chunk, k_ref, v_ref, n_tiles):
    m_rows = q.shape[0]
    dv = v_ref.shape[-1]

    def body(kt, carry):
        m_i, l_i, acc = carry
        start = pl.multiple_of(kt * ATTN_TILE, ATTN_TILE)
        k = k_ref[0, pl.ds(start, ATTN_TILE), :]
        v = v_ref[0, pl.ds(start, ATTN_TILE), :]
        s = _dot_nt(q, k)
        k_chunk = (start + lax.broadcasted_iota(jnp.int32, (1, ATTN_TILE), 1)) // CHUNK
        s = jnp.where(k_chunk <= q_chunk, s, NEG_INF)
        m_new = jnp.maximum(m_i, jnp.max(s, axis=-1, keepdims=True))
        alpha = jnp.exp(m_i - m_new)
        p = jnp.exp(s - m_new)
        l_new = alpha * l_i + jnp.sum(p, axis=-1, keepdims=True)
        acc = alpha * acc + _dot(p.astype(BF16), v)
        return m_new, l_new, acc

    init = (jnp.full((m_rows, 1), NEG_INF, F32), jnp.zeros((m_rows, 1), F32),
            jnp.zeros((m_rows, dv), F32))
    _, l_i, acc = lax.fori_loop(0, n_tiles, body, init)
    return acc / l_i


def _row_chunks(qi):
    rows = qi * ATTN_TILE + lax.broadcasted_iota(jnp.int32, (ATTN_TILE, 1), 0)
    return rows // CHUNK


def _mla_attn_kernel(q_ref, k_ref, v_ref, o_ref):
    qi = pl.program_id(2)
    o = _flash(q_ref[0], _row_chunks(qi), k_ref, v_ref, qi + 1)
    o_ref[0] = o.astype(BF16)


def _mla_attn_call(q, k, v):
    b, s, _ = q.shape
    return pl.pallas_call(
        _mla_attn_kernel,
        grid=(b, MLA_HEADS, s // ATTN_TILE),
        in_specs=[pl.BlockSpec((1, ATTN_TILE, MLA_QK_PAD), lambda bi, h, i: (bi, i, h)),
                  pl.BlockSpec((1, s, MLA_QK_PAD), lambda bi, h, i: (bi, 0, h)),
                  pl.BlockSpec((1, s, MLA_V), lambda bi, h, i: (bi, 0, h))],
        out_specs=pl.BlockSpec((1, ATTN_TILE, MLA_V), lambda bi, h, i: (bi, i, h)),
        out_shape=jax.ShapeDtypeStruct((b, s, MLA_HEADS * MLA_V), BF16),
        compiler_params=_params("parallel", "parallel", "arbitrary"),
        name="mla_attn",
    )(q, k, v)


def _diff_attn_kernel(lambda_init, q_ref, k_ref, v_ref, lam_ref, gh_ref, o_ref):
    qi = pl.program_id(2)
    q = q_ref[0]
    lane = lax.broadcasted_iota(jnp.int32, (1, 2 * DF_DH), 1)
    zero = jnp.zeros_like(q)
    q2 = jnp.concatenate([jnp.where(lane < DF_DH, q, zero), jnp.where(lane >= DF_DH, q, zero)], axis=0)
    chunks = _row_chunks(qi)
    o2 = _flash(q2, jnp.concatenate([chunks, chunks], axis=0), k_ref, v_ref, qi + 1)
    lp = lam_ref[...]
    lam = (jnp.exp(jnp.sum(lp[0:1] * lp[1:2], axis=-1, keepdims=True))
           - jnp.exp(jnp.sum(lp[2:3] * lp[3:4], axis=-1, keepdims=True)) + lambda_init)
    o = o2[:ATTN_TILE] - lam * o2[ATTN_TILE:]
    o_ref[0] = (_rms(o, gh_ref[...]) * (1.0 - lambda_init)).astype(BF16)


def _diff_attn_call(q, k, v, lam_p, g_head, lambda_init):
    b, s, _ = q.shape
    dv = 2 * DF_DH
    return pl.pallas_call(
        functools.partial(_diff_attn_kernel, lambda_init),
        grid=(b, DF_HEADS, s // ATTN_TILE),
        in_specs=[pl.BlockSpec((1, ATTN_TILE, dv), lambda bi, h, i: (bi, i, h)),
                  pl.BlockSpec((1, s, dv), lambda bi, h, i: (bi, 0, h)),
                  pl.BlockSpec((1, s, dv), lambda bi, h, i: (bi, 0, h)),
                  pl.BlockSpec(lam_p.shape, lambda bi, h, i: (0, 0)),
                  pl.BlockSpec((None, 1, dv), lambda bi, h, i: (h, 0, 0))],
        out_specs=pl.BlockSpec((1, ATTN_TILE, dv), lambda bi, h, i: (bi, i, h)),
        out_shape=jax.ShapeDtypeStruct((b, s, DF_HEADS * dv), BF16),
        compiler_params=_params("parallel", "parallel", "arbitrary"),
        name="diff_attn",
    )(q, k, v, lam_p, g_head.reshape(DF_HEADS, 1, dv))


def _chunk_cumsum(x, row):
    step = 1
    while step < CHUNK:
        x = x + jnp.where(row >= step, pltpu.roll(x, step, 0), 0.0)
        step *= 2
    return x


def _hgrn_kernel(hq_ref, hf_ref, hi_ref, hg_ref, lb_ref, go_ref, o_ref,
                 st_ref, bpad_ref, kpad_ref, vpad_ref):
    n_sub = CHUNK // HG_SUB
    off_w = HG_SUB * (n_sub * (n_sub - 1) // 2)

    @pl.when(pl.program_id(1) == 0)
    def _():
        st_ref[...] = jnp.zeros_like(st_ref)

    zpad = jnp.zeros((HG_HEADS, HG_SUB, HG_DK), F32)
    bpad_ref[:, :HG_SUB, :] = zpad
    kpad_ref[:, :HG_SUB, :] = zpad
    vpad_ref[:, :HG_SUB, :] = zpad

    row = lax.broadcasted_iota(jnp.int32, (CHUNK, 1), 0)
    row_sub = row % HG_SUB
    r2 = lax.broadcasted_iota(jnp.int32, (2 * HG_DK, 2 * HG_DK), 0) // HG_DK
    c2 = lax.broadcasted_iota(jnp.int32, (2 * HG_DK, 2 * HG_DK), 1) // HG_DK
    ones2 = jnp.where(r2 == c2, 1.0, 0.0).astype(BF16)
    col = lax.broadcasted_iota(jnp.int32, (1, off_w), 1)
    col_blk = jnp.zeros((1, off_w), jnp.int32)
    for i in range(1, n_sub):
        col_blk = col_blk + jnp.where(col >= HG_SUB * (i * (i - 1) // 2), 1, 0)
    off_mask = col_blk == (row // HG_SUB)

    def chunk_body(ci, carry):
        r0 = pl.multiple_of(ci * CHUNK, CHUNK)
        for hd in range(HG_HEADS):
            ls = slice(hd * HG_DK, (hd + 1) * HG_DK)
            qh = hq_ref[pl.ds(r0, CHUNK), ls]
            zf = hf_ref[pl.ds(r0, CHUNK), ls]
            vh = hi_ref[pl.ds(r0, CHUNK), ls]
            gh = hg_ref[pl.ds(r0, CHUNK), ls]
            lb = lb_ref[:, ls]
            sig = jax.nn.sigmoid(zf)
            logf = jnp.log(jnp.maximum(lb + (1.0 - lb) * sig, TINY))
            kk = (1.0 - lb) * (1.0 - sig)
            b = _chunk_cumsum(logf, row)

            st = st_ref[hd]
            o = _dot_nt((qh * jnp.exp(b)).astype(BF16), st.astype(BF16))
            b_last = b[CHUNK - 1:CHUNK]
            kdec = kk * jnp.exp(b_last - b)
            st_ref[hd] = st * jnp.exp(b_last) + _dot_tn(vh.astype(BF16), kdec.astype(BF16))

            refs = [b[i * HG_SUB - 1:i * HG_SUB] for i in range(1, n_sub)]
            bref = jnp.concatenate(
                [jnp.zeros((HG_SUB, HG_DK), F32)]
                + [jnp.broadcast_to(r, (HG_SUB, HG_DK)) for r in refs], axis=0)
            qs = qh * jnp.exp(b - bref)
            kst = jnp.concatenate(
                [kk[:i * HG_SUB] * jnp.exp(refs[i - 1] - b[:i * HG_SUB]) for i in range(1, n_sub)],
                axis=0)
            vst = jnp.concatenate([vh[:i * HG_SUB] for i in range(1, n_sub)], axis=0)
            a_off = jnp.where(off_mask, _dot_nt(qs.astype(BF16), kst.astype(BF16)), 0.0)
            o = o + _dot(a_off.astype(BF16), vst.astype(BF16))

            bpad_ref[hd, HG_SUB:, :] = b
            kpad_ref[hd, HG_SUB:, :] = kk
            vpad_ref[hd, HG_SUB:, :] = vh
            for dp in range(HG_SUB // 2):
                terms = []
                for d in (2 * dp, 2 * dp + 1):
                    lo = HG_SUB - d
                    w = qh * jnp.exp(b - bpad_ref[hd, lo:lo + CHUNK, :]) * kpad_ref[hd, lo:lo + CHUNK, :]
                    terms.append(jnp.where(row_sub >= d, w, 0.0))
                dsum = _dot(jnp.concatenate(terms, axis=1).astype(BF16), ones2)
                for j, d in enumerate((2 * dp, 2 * dp + 1)):
                    lo = HG_SUB - d
                    o = o + dsum[:, j * HG_DK:(j + 1) * HG_DK] * vpad_ref[hd, lo:lo + CHUNK, :]

            on = _rms(o, go_ref[:, ls]) * jax.nn.silu(gh)
            o_ref[pl.ds(r0, CHUNK), ls] = on.astype(BF16)
        return carry

    lax.fori_loop(0, HG_SEQ_TILE // CHUNK, chunk_body, 0)


def _hgrn_call(zh, lb, g_out, b, s):
    width = HG_HEADS * HG_DK
    n_seq = s // HG_SEQ_TILE
    part = lambda j: pl.BlockSpec((HG_SEQ_TILE, width), lambda bi, si: (bi * n_seq + si, j))
    pad = pltpu.VMEM((HG_HEADS, HG_SUB + CHUNK, HG_DK), F32)
    return pl.pallas_call(
        _hgrn_kernel,
        grid=(b, n_seq),
        in_specs=[part(0), part(1), part(2), part(3),
                  pl.BlockSpec((1, width), lambda bi, si: (0, 0)),
                  pl.BlockSpec((1, width), lambda bi, si: (0, 0))],
        out_specs=pl.BlockSpec((HG_SEQ_TILE, width), lambda bi, si: (bi * n_seq + si, 0)),
        out_shape=jax.ShapeDtypeStruct((b * s, width), BF16),
        scratch_shapes=[pltpu.VMEM((HG_HEADS, HG_DV, HG_DK), F32), pad, pad, pad],
        compiler_params=_params("parallel", "arbitrary"),
        name="hgrn2",
    )(zh, zh, zh, zh, lb, g_out)


def _rope_tables(positions, dim, group):
    half = dim // 2
    inv_freq = ROPE_THETA ** (-jnp.arange(0, dim, 2, dtype=F32) / dim)
    ang = positions.astype(F32).reshape(-1, 1) * inv_freq
    cos, sin = jnp.cos(ang), jnp.sin(ang)
    t = ang.shape[0]
    rest = group - dim
    c = jnp.concatenate([cos, cos, jnp.ones((t, rest), F32)], axis=1)
    sa = jnp.concatenate([-sin, jnp.zeros((t, half + rest), F32)], axis=1)
    sb = jnp.concatenate([jnp.zeros((t, half), F32), sin, jnp.zeros((t, rest), F32)], axis=1)
    rep = LANES // group
    return tuple(jnp.tile(a, (1, rep)) for a in (c, sa, sb))


def kernel(x, positions, norm_g, ffn_w_gate, ffn_w_up, ffn_w_down, ev_w_in, ev_g_q, ev_w_uq, ev_g_kv, ev_w_ukv, ev_lb_logits, ev_g_out, ev_w_out, od_w_in, od_lambda, od_g_head, od_w_out):
    b, s, d = x.shape
    t = b * s
    xt = x.reshape(t, d)
    tabs_m = _rope_tables(positions, MLA_ROPE, LANES)
    tabs_d = _rope_tables(positions, DF_ROT, DF_DH)
    lb_w = jax.nn.softmax(ev_lb_logits.astype(F32), axis=0)
    lb_all = jnp.cumsum(lb_w, axis=0) - lb_w[0:1]

    for l in range(DEPTH):
        g = norm_g[l]
        wg = ffn_w_gate[l].astype(BF16)
        wu = ffn_w_up[l].astype(BF16)
        wd = ffn_w_down[l].astype(BF16)
        xt = _ffn_call(xt, g[0], wg[0], wu[0], wd[0])
        j = l // 2
        if l % 2 == 0:
            w_in = ev_w_in[j]
            o_pe = MLA_Q_RANK + MLA_KV_RANK + MLA_ROPE
            w_in = jnp.concatenate(
                [w_in[:, :o_pe], jnp.zeros((d, LANES - MLA_ROPE), F32), w_in[:, o_pe:]], axis=1)
            w_uq = ev_w_uq[j].reshape(MLA_Q_RANK, MLA_HEADS, MLA_NOPE + MLA_ROPE)
            w_uq = jnp.pad(w_uq, ((0, 0), (0, 0), (0, MLA_QK_PAD - MLA_NOPE - MLA_ROPE)))
            w_uq = w_uq.reshape(MLA_Q_RANK, MLA_HEADS * MLA_QK_PAD)
            w_ukv = ev_w_ukv[j].reshape(MLA_KV_RANK, MLA_HEADS, 2, MLA_NOPE)
            w_ukv = w_ukv.transpose(0, 2, 1, 3).reshape(MLA_KV_RANK, 2 * MLA_HEADS * MLA_NOPE)
            q, k, v, zh = _even_pre_call(
                xt, g[1, 0:1], w_in.astype(BF16), ev_g_q[j].reshape(1, -1), w_uq.astype(BF16),
                ev_g_kv[j].reshape(1, -1), w_ukv.astype(BF16), tabs_m)
            o_a = _mla_attn_call(q.reshape(b, s, -1), k.reshape(b, s, -1), v.reshape(b, s, -1))
            o_b = _hgrn_call(zh, lb_all[j].reshape(1, -1), ev_g_out[j].reshape(1, -1), b, s)
            parts = [o_a.reshape(t, -1), o_b]
            w_out = ev_w_out[j]
        else:
            lambda_init = 0.8 - 0.6 * math.exp(-0.3 * l)
            q, k, v = _odd_pre_call(xt, g[1, 0:1], od_w_in[j].astype(BF16), tabs_d)
            o = _diff_attn_call(q.reshape(b, s, -1), k.reshape(b, s, -1), v.reshape(b, s, -1),
                                od_lambda[j], od_g_head[j], lambda_init)
            parts = [o.reshape(t, -1)]
            w_out = od_w_out[j]
        g3 = jnp.stack([g[1, 1], g[2, 0], g[2, 1]])
        xt = _mix_ffn_call(xt, parts, w_out.astype(BF16), g3, wg[1], wu[1], wd[1])
    return xt.reshape(b, s, d)
```

```python
import functools
import math

import jax
import jax.numpy as jnp
from jax import lax
from jax.experimental import pallas as pl
from jax.experimental.pallas import tpu as pltpu

D_MODEL = 1024
DEPTH = 4
CHUNK = 64
ROPE_THETA = 500000.0
EPS = 1e-6
NEG_INF = -1e30
TINY = 1e-30
D_FF = 2816
MLA_HEADS = 4
MLA_NOPE = 128
MLA_ROPE = 64
MLA_V = 128
MLA_Q_RANK = 384
MLA_KV_RANK = 256
HG_HEADS = 4
HG_DK = 128
HG_DV = 128
DF_HEADS = 8
DF_DH = 64
DF_ROT = DF_DH // 4

LANES = 128
MXU_DIM = 256
VMEM_LIMIT = 56 * 1024 * 1024

TOKEN_TILE = 512
MLA_Q_TILE = 512
DF_Q_TILE = 256
KEY_BLOCK = 1024
ATTN_HEADS_PER_STEP = 2
HG_SEQ_TILE = 1024
HG_SUB = 16
FF_CHUNKS = ((0, 1024), (1024, 2048), (2048, D_FF))

MLA_QK_PAD = 2 * LANES
V_PAD = 2 * LANES
EVEN_Z = MLA_Q_RANK + MLA_KV_RANK + LANES + 4 * HG_HEADS * HG_DK

F32 = jnp.float32
BF16 = jnp.bfloat16


def _rms(x, g):
    ms = jnp.mean(x * x, axis=-1, keepdims=True)
    return x * lax.rsqrt(ms + EPS) * g


def _dot(a, b):
    return jnp.dot(a, b, preferred_element_type=F32)


def _dot_nt(a, b):
    return lax.dot_general(a, b, (((1,), (1,)), ((), ())), preferred_element_type=F32)


def _dot_tn(a, b):
    return lax.dot_general(a, b, (((0,), (0,)), ((), ())), preferred_element_type=F32)


def _rope_slab(x, c, sa, sb, half):
    return x * c + pltpu.roll(x, LANES - half, 1) * sa + pltpu.roll(x, half, 1) * sb


def _store_values(v_ref, v, heads):
    ones = jnp.ones((v.shape[0], V_PAD - LANES), BF16)
    for hd in range(heads):
        v_ref[:, hd * V_PAD:hd * V_PAD + LANES] = v[:, hd * LANES:(hd + 1) * LANES].astype(BF16)
        v_ref[:, hd * V_PAD + LANES:(hd + 1) * V_PAD] = ones


def _const_spec(shape):
    nd = len(shape)
    return pl.BlockSpec(shape, lambda *_: (0,) * nd, pipeline_mode=pl.Buffered(1))


def _params(*sem):
    return pltpu.CompilerParams(dimension_semantics=sem, vmem_limit_bytes=VMEM_LIMIT)


def _ffn_apply(x, gpre, gpost, wg_ref, wu_ref, wd_ref):
    xn = _rms(x, gpre).astype(BF16)
    acc = None
    for lo, hi in FF_CHUNKS:
        g = _dot(xn, wg_ref[:, lo:hi])
        u = _dot(xn, wu_ref[:, lo:hi])
        a = (jax.nn.silu(g) * u).astype(BF16)
        h = _dot(a, wd_ref[lo:hi, :])
        acc = h if acc is None else acc + h
    return x + 0.5 * _rms(acc, gpost)


def _ffn_kernel(x_ref, g_ref, wg_ref, wu_ref, wd_ref, o_ref):
    g = g_ref[...]
    o_ref[...] = _ffn_apply(x_ref[...], g[0:1], g[1:2], wg_ref, wu_ref, wd_ref)


def _ffn_call(x, g2, wg, wu, wd):
    t = x.shape[0]
    tile = pl.BlockSpec((TOKEN_TILE, D_MODEL), lambda i: (i, 0))
    return pl.pallas_call(
        _ffn_kernel,
        grid=(t // TOKEN_TILE,),
        in_specs=[tile, _const_spec((2, D_MODEL)), _const_spec(wg.shape),
                  _const_spec(wu.shape), _const_spec(wd.shape)],
        out_specs=tile,
        out_shape=jax.ShapeDtypeStruct(x.shape, F32),
        compiler_params=_params("parallel"),
        name="ffn",
    )(x, g2, wg, wu, wd)


def _mix_ffn_kernel(n_parts, *refs):
    x_ref = refs[0]
    part_refs = refs[1:1 + n_parts]
    wo_ref, g_ref, wg_ref, wu_ref, wd_ref, o_ref = refs[1 + n_parts:]
    g = g_ref[...]
    m = None
    row = 0
    for p_ref in part_refs:
        w = p_ref.shape[-1]
        d = _dot(p_ref[...], wo_ref[row:row + w, :])
        m = d if m is None else m + d
        row += w
    x1 = x_ref[...] + _rms(m, g[0:1])
    o_ref[...] = _ffn_apply(x1, g[1:2], g[2:3], wg_ref, wu_ref, wd_ref)


def _mix_ffn_call(x, parts, wo, g3, wg, wu, wd):
    t = x.shape[0]
    tile = pl.BlockSpec((TOKEN_TILE, D_MODEL), lambda i: (i, 0))
    part_specs = [pl.BlockSpec((TOKEN_TILE, p.shape[-1]), lambda i: (i, 0)) for p in parts]
    return pl.pallas_call(
        functools.partial(_mix_ffn_kernel, len(parts)),
        grid=(t // TOKEN_TILE,),
        in_specs=[tile] + part_specs + [_const_spec(wo.shape), _const_spec((3, D_MODEL)),
                                        _const_spec(wg.shape), _const_spec(wu.shape),
                                        _const_spec(wd.shape)],
        out_specs=tile,
        out_shape=jax.ShapeDtypeStruct(x.shape, F32),
        compiler_params=_params("parallel"),
        name="mix_ffn",
    )(x, *parts, wo, g3, wg, wu, wd)


def _even_pre_kernel(x_ref, g_ref, win_ref, gq_ref, wuq_ref, gkv_ref, wukv_ref,
                     c_ref, sa_ref, sb_ref, q_ref, k_ref, v_ref, zh_ref):
    h = _rms(x_ref[...], g_ref[...]).astype(BF16)
    z = _dot(h, win_ref[...])
    o_kv = MLA_Q_RANK
    o_pe = o_kv + MLA_KV_RANK
    o_h = o_pe + LANES
    zh_ref[...] = z[:, o_h:]
    c, sa, sb = c_ref[...], sa_ref[...], sb_ref[...]
    half = MLA_ROPE // 2
    scale = (MLA_NOPE + MLA_ROPE) ** -0.5

    cq = _rms(z[:, :o_kv], gq_ref[...]).astype(BF16)
    q = _dot(cq, wuq_ref[...]) * scale
    for hd in range(MLA_HEADS):
        base = hd * MLA_QK_PAD
        q_ref[:, base:base + LANES] = q[:, base:base + LANES].astype(BF16)
        q_ref[:, base + LANES:base + 2 * LANES] = _rope_slab(
            q[:, base + LANES:base + 2 * LANES], c, sa, sb, half).astype(BF16)

    ckv = _rms(z[:, o_kv:o_pe], gkv_ref[...]).astype(BF16)
    kv = _dot(ckv, wukv_ref[...])
    kpe = _rope_slab(z[:, o_pe:o_h], c, sa, sb, half).astype(BF16)
    for hd in range(MLA_HEADS):
        base = hd * MLA_QK_PAD
        k_ref[:, base:base + LANES] = kv[:, hd * LANES:(hd + 1) * LANES].astype(BF16)
        k_ref[:, base + LANES:base + 2 * LANES] = kpe
    _store_values(v_ref, kv[:, MLA_HEADS * MLA_NOPE:], MLA_HEADS)


def _even_pre_call(x, g, win, gq, wuq, gkv, wukv, tabs):
    t = x.shape[0]
    row = lambda w: pl.BlockSpec((TOKEN_TILE, w), lambda i: (i, 0))
    qk_w = MLA_HEADS * MLA_QK_PAD
    v_w = MLA_HEADS * V_PAD
    zh_w = 4 * HG_HEADS * HG_DK
    return pl.pallas_call(
        _even_pre_kernel,
        grid=(t // TOKEN_TILE,),
        in_specs=[row(D_MODEL), _const_spec((1, D_MODEL)), _const_spec(win.shape),
                  _const_spec(gq.shape), _const_spec(wuq.shape), _const_spec(gkv.shape),
                  _const_spec(wukv.shape), row(LANES), row(LANES), row(LANES)],
        out_specs=[row(qk_w), row(qk_w), row(v_w), row(zh_w)],
        out_shape=[jax.ShapeDtypeStruct((t, qk_w), BF16), jax.ShapeDtypeStruct((t, qk_w), BF16),
                   jax.ShapeDtypeStruct((t, v_w), BF16), jax.ShapeDtypeStruct((t, zh_w), F32)],
        compiler_params=_params("parallel"),
        name="even_pre",
    )(x, g, win, gq, wuq, gkv, wukv, *tabs)


def _odd_pre_kernel(x_ref, g_ref, win_ref, c_ref, sa_ref, sb_ref, q_ref, k_ref, v_ref):
    h = _rms(x_ref[...], g_ref[...]).astype(BF16)
    c, sa, sb = c_ref[...], sa_ref[...], sb_ref[...]
    half = DF_ROT // 2
    width = DF_HEADS * 2 * DF_DH
    scale = DF_DH ** -0.5
    q = _dot(h, win_ref[:, :width]) * scale
    k = _dot(h, win_ref[:, width:2 * width])
    for j in range(width // LANES):
        sl = slice(j * LANES, (j + 1) * LANES)
        q_ref[:, sl] = _rope_slab(q[:, sl], c, sa, sb, half).astype(BF16)
        k_ref[:, sl] = _rope_slab(k[:, sl], c, sa, sb, half).astype(BF16)
    _store_values(v_ref, _dot(h, win_ref[:, 2 * width:]), DF_HEADS)


def _odd_pre_call(x, g, win, tabs):
    t = x.shape[0]
    width = DF_HEADS * 2 * DF_DH
    row = lambda w: pl.BlockSpec((TOKEN_TILE, w), lambda i: (i, 0))
    return pl.pallas_call(
        _odd_pre_kernel,
        grid=(t // TOKEN_TILE,),
        in_specs=[row(D_MODEL), _const_spec((1, D_MODEL)), _const_spec(win.shape),
                  row(LANES), row(LANES), row(LANES)],
        out_specs=[row(width), row(width), row(DF_HEADS * V_PAD)],
        out_shape=[jax.ShapeDtypeStruct((t, width), BF16), jax.ShapeDtypeStruct((t, width), BF16),
                   jax.ShapeDtypeStruct((t, DF_HEADS * V_PAD), BF16)],
        compiler_params=_params("parallel"),
        name="odd_pre",
    )(x, g, win, *tabs)


def _flash(streams, q_chunk, k_ref, v_ref, row0, tq):
    m_rows = streams[0][0].shape[0]

    def step(start, width, carries, masked):
        scores = [_dot_nt(q, k_ref[0, pl.ds(start, width), ksl]) for q, ksl, _ in streams]
        probs = []
        for s, (m_i, _, _) in zip(scores, carries):
            if masked:
                k_chunk = (start + (width - tq) + lax.broadcasted_iota(jnp.int32, (1, tq), 1)) // CHUNK
                tail = jnp.where(k_chunk <= q_chunk, s[:, width - tq:], NEG_INF)
                s = tail if width == tq else jnp.concatenate([s[:, :width - tq], tail], axis=1)
            m_new = jnp.maximum(m_i, jnp.max(s, axis=-1, keepdims=True))
            probs.append((m_new, jnp.exp(m_i - m_new), jnp.exp(s - m_new).astype(BF16)))
        outs = []
        for (m_new, alpha, p), (_, _, vsl), (_, l_i, acc) in zip(probs, streams, carries):
            pv = _dot(p, v_ref[0, pl.ds(start, width), vsl])
            outs.append((m_new, alpha * l_i + pv[:, LANES:], alpha * acc + pv[:, :LANES]))
        return tuple(outs)

    init = tuple((jnp.full((m_rows, 1), NEG_INF, F32), jnp.zeros((m_rows, LANES), F32),
                  jnp.zeros((m_rows, LANES), F32)) for _ in streams)
    n_full = row0 // KEY_BLOCK
    carries = lax.fori_loop(
        0, n_full,
        lambda j, c: step(pl.multiple_of(j * KEY_BLOCK, KEY_BLOCK), KEY_BLOCK, c, False), init)
    start = pl.multiple_of(n_full * KEY_BLOCK, KEY_BLOCK)
    diag = [functools.partial(step, start, tq * (p + 1), masked=True) for p in range(KEY_BLOCK // tq)]
    carries = lax.switch((row0 % KEY_BLOCK) // tq, diag, carries)
    return [acc / l_i for _, l_i, acc in carries]


def _row_chunks(row0, n_rows):
    rows = row0 + lax.broadcasted_iota(jnp.int32, (n_rows, 1), 0)
    return rows // CHUNK


def _mla_attn_kernel(q_ref, k_ref, v_ref, o_ref):
    row0 = pl.program_id(2) * MLA_Q_TILE
    streams = [(q_ref[0, :, h * MLA_QK_PAD:(h + 1) * MLA_QK_PAD],
                slice(h * MLA_QK_PAD, (h + 1) * MLA_QK_PAD), slice(h * V_PAD, (h + 1) * V_PAD))
               for h in range(ATTN_HEADS_PER_STEP)]
    outs = _flash(streams, _row_chunks(row0, MLA_Q_TILE), k_ref, v_ref, row0, MLA_Q_TILE)
    for h, o in enumerate(outs):
        o_ref[0, :, h * MLA_V:(h + 1) * MLA_V] = o.astype(BF16)


def _mla_attn_call(q, k, v):
    b, s, _ = q.shape
    hp = ATTN_HEADS_PER_STEP
    return pl.pallas_call(
        _mla_attn_kernel,
        grid=(b, MLA_HEADS // hp, s // MLA_Q_TILE),
        in_specs=[pl.BlockSpec((1, MLA_Q_TILE, hp * MLA_QK_PAD), lambda bi, h, i: (bi, i, h)),
                  pl.BlockSpec((1, s, hp * MLA_QK_PAD), lambda bi, h, i: (bi, 0, h)),
                  pl.BlockSpec((1, s, hp * V_PAD), lambda bi, h, i: (bi, 0, h))],
        out_specs=pl.BlockSpec((1, MLA_Q_TILE, hp * MLA_V), lambda bi, h, i: (bi, i, h)),
        out_shape=jax.ShapeDtypeStruct((b, s, MLA_HEADS * MLA_V), BF16),
        compiler_params=_params("parallel", "parallel", "arbitrary"),
        name="mla_attn",
    )(q, k, v)


def _diff_attn_kernel(lambda_init, q_ref, k_ref, v_ref, lam_ref, gh_ref, o_ref):
    row0 = pl.program_id(2) * DF_Q_TILE
    dv = 2 * DF_DH
    lane = lax.broadcasted_iota(jnp.int32, (1, dv), 1)
    streams = []
    for h in range(ATTN_HEADS_PER_STEP):
        hs = slice(h * dv, (h + 1) * dv)
        q = q_ref[0, :, hs]
        zero = jnp.zeros_like(q)
        q2 = jnp.concatenate([jnp.where(lane < DF_DH, q, zero), jnp.where(lane >= DF_DH, q, zero)], axis=0)
        streams.append((q2, hs, slice(h * V_PAD, (h + 1) * V_PAD)))
    chunks = _row_chunks(row0, DF_Q_TILE)
    outs = _flash(streams, jnp.concatenate([chunks, chunks], axis=0), k_ref, v_ref, row0, DF_Q_TILE)
    lp = lam_ref[...]
    lam = (jnp.exp(jnp.sum(lp[0:1] * lp[1:2], axis=-1, keepdims=True))
           - jnp.exp(jnp.sum(lp[2:3] * lp[3:4], axis=-1, keepdims=True)) + lambda_init)
    for h, o2 in enumerate(outs):
        hs = slice(h * dv, (h + 1) * dv)
        o = o2[:DF_Q_TILE] - lam * o2[DF_Q_TILE:]
        o_ref[0, :, hs] = (_rms(o, gh_ref[:, hs]) * (1.0 - lambda_init)).astype(BF16)


def _diff_attn_call(q, k, v, lam_p, g_head, lambda_init):
    b, s, _ = q.shape
    hp = ATTN_HEADS_PER_STEP
    w = hp * 2 * DF_DH
    return pl.pallas_call(
        functools.partial(_diff_attn_kernel, lambda_init),
        grid=(b, DF_HEADS // hp, s // DF_Q_TILE),
        in_specs=[pl.BlockSpec((1, DF_Q_TILE, w), lambda bi, h, i: (bi, i, h)),
                  pl.BlockSpec((1, s, w), lambda bi, h, i: (bi, 0, h)),
                  pl.BlockSpec((1, s, hp * V_PAD), lambda bi, h, i: (bi, 0, h)),
                  pl.BlockSpec(lam_p.shape, lambda bi, h, i: (0, 0)),
                  pl.BlockSpec((None, 1, w), lambda bi, h, i: (h, 0, 0))],
        out_specs=pl.BlockSpec((1, DF_Q_TILE, w), lambda bi, h, i: (bi, i, h)),
        out_shape=jax.ShapeDtypeStruct((b, s, DF_HEADS * 2 * DF_DH), BF16),
        compiler_params=_params("parallel", "parallel", "arbitrary"),
        name="diff_attn",
    )(q, k, v, lam_p, g_head.reshape(DF_HEADS // hp, 1, w))


def _chunk_cumsum(x, row):
    step = 1
    while step < CHUNK:
        x = x + jnp.where(row >= step, pltpu.roll(x, step, 0), 0.0)
        step *= 2
    return x


def _hgrn_kernel(hq_ref, hf_ref, hi_ref, hg_ref, lb_ref, go_ref, o_ref,
                 st_ref, bpad_ref, kpad_ref, vpad_ref):
    n_sub = CHUNK // HG_SUB
    off_w = HG_SUB * (n_sub * (n_sub - 1) // 2)

    @pl.when(pl.program_id(1) == 0)
    def _():
        st_ref[...] = jnp.zeros_like(st_ref)

    zpad = jnp.zeros((HG_HEADS, HG_SUB, HG_DK), F32)
    bpad_ref[:, :HG_SUB, :] = zpad
    kpad_ref[:, :HG_SUB, :] = zpad
    vpad_ref[:, :HG_SUB, :] = zpad

    row = lax.broadcasted_iota(jnp.int32, (CHUNK, 1), 0)
    row_sub = row % HG_SUB
    r2 = lax.broadcasted_iota(jnp.int32, (2 * HG_DK, 2 * HG_DK), 0) // HG_DK
    c2 = lax.broadcasted_iota(jnp.int32, (2 * HG_DK, 2 * HG_DK), 1) // HG_DK
    ones2 = jnp.where(r2 == c2, 1.0, 0.0).astype(BF16)
    col = lax.broadcasted_iota(jnp.int32, (1, off_w), 1)
    col_blk = jnp.zeros((1, off_w), jnp.int32)
    for i in range(1, n_sub):
        col_blk = col_blk + jnp.where(col >= HG_SUB * (i * (i - 1) // 2), 1, 0)
    off_mask = col_blk == (row // HG_SUB)

    def chunk_body(ci, carry):
        r0 = pl.multiple_of(ci * CHUNK, CHUNK)
        for hd in range(HG_HEADS):
            ls = slice(hd * HG_DK, (hd + 1) * HG_DK)
            qh = hq_ref[pl.ds(r0, CHUNK), ls]
            zf = hf_ref[pl.ds(r0, CHUNK), ls]
            vh = hi_ref[pl.ds(r0, CHUNK), ls]
            gh = hg_ref[pl.ds(r0, CHUNK), ls]
            lb = lb_ref[:, ls]
            sig = jax.nn.sigmoid(zf)
            logf = jnp.log(jnp.maximum(lb + (1.0 - lb) * sig, TINY))
            kk = (1.0 - lb) * (1.0 - sig)
            b = _chunk_cumsum(logf, row)

            st = st_ref[hd]
            o = _dot_nt((qh * jnp.exp(b)).astype(BF16), st.astype(BF16))
            b_last = b[CHUNK - 1:CHUNK]
            kdec = kk * jnp.exp(b_last - b)
            st_ref[hd] = st * jnp.exp(b_last) + _dot_tn(vh.astype(BF16), kdec.astype(BF16))

            refs = [b[i * HG_SUB - 1:i * HG_SUB] for i in range(1, n_sub)]
            bref = jnp.concatenate(
                [jnp.zeros((HG_SUB, HG_DK), F32)]
                + [jnp.broadcast_to(r, (HG_SUB, HG_DK)) for r in refs], axis=0)
            qs = qh * jnp.exp(b - bref)
            kst = jnp.concatenate(
                [kk[:i * HG_SUB] * jnp.exp(refs[i - 1] - b[:i * HG_SUB]) for i in range(1, n_sub)],
                axis=0)
            vst = jnp.concatenate([vh[:i * HG_SUB] for i in range(1, n_sub)], axis=0)
            a_off = jnp.where(off_mask, _dot_nt(qs.astype(BF16), kst.astype(BF16)), 0.0)
            o = o + _dot(a_off.astype(BF16), vst.astype(BF16))

            bpad_ref[hd, HG_SUB:, :] = b
            kpad_ref[hd, HG_SUB:, :] = kk
            vpad_ref[hd, HG_SUB:, :] = vh
            for dp in range(HG_SUB // 2):
                terms = []
                for d in (2 * dp, 2 * dp + 1):
                    lo = HG_SUB - d
                    w = qh * jnp.exp(b - bpad_ref[hd, lo:lo + CHUNK, :]) * kpad_ref[hd, lo:lo + CHUNK, :]
                    terms.append(jnp.where(row_sub >= d, w, 0.0))
                dsum = _dot(jnp.concatenate(terms, axis=1).astype(BF16), ones2)
                for j, d in enumerate((2 * dp, 2 * dp + 1)):
                    lo = HG_SUB - d
                    o = o + dsum[:, j * HG_DK:(j + 1) * HG_DK] * vpad_ref[hd, lo:lo + CHUNK, :]

            on = _rms(o, go_ref[:, ls]) * jax.nn.silu(gh)
            o_ref[pl.ds(r0, CHUNK), ls] = on.astype(BF16)
        return carry

    lax.fori_loop(0, HG_SEQ_TILE // CHUNK, chunk_body, 0)


def _hgrn_call(zh, lb, g_out, b, s):
    width = HG_HEADS * HG_DK
    n_seq = s // HG_SEQ_TILE
    part = lambda j: pl.BlockSpec((HG_SEQ_TILE, width), lambda bi, si: (bi * n_seq + si, j))
    pad = pltpu.VMEM((HG_HEADS, HG_SUB + CHUNK, HG_DK), F32)
    return pl.pallas_call(
        _hgrn_kernel,
        grid=(b, n_seq),
        in_specs=[part(0), part(1), part(2), part(3),
                  pl.BlockSpec((1, width), lambda bi, si: (0, 0)),
                  pl.BlockSpec((1, width), lambda bi, si: (0, 0))],
        out_specs=pl.BlockSpec((HG_SEQ_TILE, width), lambda bi, si: (bi * n_seq + si, 0)),
        out_shape=jax.ShapeDtypeStruct((b * s, width), BF16),
        scratch_shapes=[pltpu.VMEM((HG_HEADS, HG_DV, HG_DK), F32), pad, pad, pad],
        compiler_params=_params("parallel", "arbitrary"),
        name="hgrn2",
    )(zh, zh, zh, zh, lb, g_out)


def _rope_tables(positions, dim, group):
    half = dim // 2
    inv_freq = ROPE_THETA ** (-jnp.arange(0, dim, 2, dtype=F32) / dim)
    ang = positions.astype(F32).reshape(-1, 1) * inv_freq
    cos, sin = jnp.cos(ang), jnp.sin(ang)
    t = ang.shape[0]
    rest = group - dim
    c = jnp.concatenate([cos, cos, jnp.ones((t, rest), F32)], axis=1)
    sa = jnp.concatenate([-sin, jnp.zeros((t, half + rest), F32)], axis=1)
    sb = jnp.concatenate([jnp.zeros((t, half), F32), sin, jnp.zeros((t, rest), F32)], axis=1)
    rep = LANES // group
    return tuple(jnp.tile(a, (1, rep)) for a in (c, sa, sb))


def kernel(x, positions, norm_g, ffn_w_gate, ffn_w_up, ffn_w_down, ev_w_in, ev_g_q, ev_w_uq, ev_g_kv, ev_w_ukv, ev_lb_logits, ev_g_out, ev_w_out, od_w_in, od_lambda, od_g_head, od_w_out):
    b, s, d = x.shape
    t = b * s
    xt = x.reshape(t, d)
    tabs_m = _rope_tables(positions, MLA_ROPE, LANES)
    tabs_d = _rope_tables(positions, DF_ROT, DF_DH)
    lb_w = jax.nn.softmax(ev_lb_logits.astype(F32), axis=0)
    lb_all = jnp.cumsum(lb_w, axis=0) - lb_w[0:1]

    for l in range(DEPTH):
        g = norm_g[l]
        wg = ffn_w_gate[l].astype(BF16)
        wu = ffn_w_up[l].astype(BF16)
        wd = ffn_w_down[l].astype(BF16)
        xt = _ffn_call(xt, g[0], wg[0], wu[0], wd[0])
        j = l // 2
        if l % 2 == 0:
            w_in = ev_w_in[j]
            o_pe = MLA_Q_RANK + MLA_KV_RANK + MLA_ROPE
            w_in = jnp.concatenate(
                [w_in[:, :o_pe], jnp.zeros((d, LANES - MLA_ROPE), F32), w_in[:, o_pe:]], axis=1)
            w_uq = ev_w_uq[j].reshape(MLA_Q_RANK, MLA_HEADS, MLA_NOPE + MLA_ROPE)
            w_uq = jnp.pad(w_uq, ((0, 0), (0, 0), (0, MLA_QK_PAD - MLA_NOPE - MLA_ROPE)))
            w_uq = w_uq.reshape(MLA_Q_RANK, MLA_HEADS * MLA_QK_PAD)
            w_ukv = ev_w_ukv[j].reshape(MLA_KV_RANK, MLA_HEADS, 2, MLA_NOPE)
            w_ukv = w_ukv.transpose(0, 2, 1, 3).reshape(MLA_KV_RANK, 2 * MLA_HEADS * MLA_NOPE)
            q, k, v, zh = _even_pre_call(
                xt, g[1, 0:1], w_in.astype(BF16), ev_g_q[j].reshape(1, -1), w_uq.astype(BF16),
                ev_g_kv[j].reshape(1, -1), w_ukv.astype(BF16), tabs_m)
            o_a = _mla_attn_call(q.reshape(b, s, -1), k.reshape(b, s, -1), v.reshape(b, s, -1))
            o_b = _hgrn_call(zh, lb_all[j].reshape(1, -1), ev_g_out[j].reshape(1, -1), b, s)
            parts = [o_a.reshape(t, -1), o_b]
            w_out = ev_w_out[j]
        else:
            lambda_init = 0.8 - 0.6 * math.exp(-0.3 * l)
            q, k, v = _odd_pre_call(xt, g[1, 0:1], od_w_in[j].astype(BF16), tabs_d)
            o = _diff_attn_call(q.reshape(b, s, -1), k.reshape(b, s, -1), v.reshape(b, s, -1),
                                od_lambda[j], od_g_head[j], lambda_init)
            parts = [o.reshape(t, -1)]
            w_out = od_w_out[j]
        g3 = jnp.stack([g[1, 1], g[2, 0], g[2, 1]])
        xt = _mix_ffn_call(xt, parts, w_out.astype(BF16), g3, wg[1], wu[1], wd[1])
    return xt.reshape(b, s, d)
```

```python
import functools
import math

import jax
import jax.numpy as jnp
from jax import lax
from jax.experimental import pallas as pl
from jax.experimental.pallas import tpu as pltpu

D_MODEL = 1024
DEPTH = 4
CHUNK = 64
ROPE_THETA = 500000.0
EPS = 1e-6
NEG_INF = -1e30
TINY = 1e-30
D_FF = 2816
MLA_HEADS = 4
MLA_NOPE = 128
MLA_ROPE = 64
MLA_V = 128
MLA_Q_RANK = 384
MLA_KV_RANK = 256
HG_HEADS = 4
HG_DK = 128
HG_DV = 128
DF_HEADS = 8
DF_DH = 64
DF_ROT = DF_DH // 4

LANES = 128
MXU_DIM = 256
VMEM_LIMIT = 56 * 1024 * 1024

TOKEN_TILE = 512
MLA_Q_TILE = 512
DF_Q_TILE = 256
KEY_BLOCK = 1024
ATTN_HEADS_PER_STEP = 2
HG_SEQ_TILE = 1024
HG_SUB = 16
HG_UNROLL = 4
HG_SAFE_LOG = 75.0
FF_CHUNKS = ((0, 1024), (1024, 2048), (2048, D_FF))

MLA_QK_PAD = 2 * LANES
V_PAD = 2 * LANES
EVEN_Z = MLA_Q_RANK + MLA_KV_RANK + LANES + 4 * HG_HEADS * HG_DK

F32 = jnp.float32
BF16 = jnp.bfloat16


def _rms(x, g):
    ms = jnp.mean(x * x, axis=-1, keepdims=True)
    return x * lax.rsqrt(ms + EPS) * g


def _dot(a, b):
    return jnp.dot(a, b, preferred_element_type=F32)


def _dot_nt(a, b):
    return lax.dot_general(a, b, (((1,), (1,)), ((), ())), preferred_element_type=F32)


def _dot_tn(a, b):
    return lax.dot_general(a, b, (((0,), (0,)), ((), ())), preferred_element_type=F32)


def _rope_slab(x, c, sa, sb, half):
    return x * c + pltpu.roll(x, LANES - half, 1) * sa + pltpu.roll(x, half, 1) * sb


def _store_values(v_ref, v, heads):
    ones = jnp.ones((v.shape[0], V_PAD - LANES), BF16)
    for hd in range(heads):
        v_ref[:, hd * V_PAD:hd * V_PAD + LANES] = v[:, hd * LANES:(hd + 1) * LANES].astype(BF16)
        v_ref[:, hd * V_PAD + LANES:(hd + 1) * V_PAD] = ones


def _const_spec(shape):
    nd = len(shape)
    return pl.BlockSpec(shape, lambda *_: (0,) * nd, pipeline_mode=pl.Buffered(1))


def _params(*sem):
    return pltpu.CompilerParams(dimension_semantics=sem, vmem_limit_bytes=VMEM_LIMIT)


def _ffn_apply(x, gpre, gpost, wg_ref, wu_ref, wd_ref):
    xn = _rms(x, gpre).astype(BF16)
    acc = None
    for lo, hi in FF_CHUNKS:
        g = _dot(xn, wg_ref[:, lo:hi])
        u = _dot(xn, wu_ref[:, lo:hi])
        a = (jax.nn.silu(g) * u).astype(BF16)
        h = _dot(a, wd_ref[lo:hi, :])
        acc = h if acc is None else acc + h
    return x + 0.5 * _rms(acc, gpost)


def _ffn_kernel(x_ref, g_ref, wg_ref, wu_ref, wd_ref, o_ref):
    g = g_ref[...]
    o_ref[...] = _ffn_apply(x_ref[...], g[0:1], g[1:2], wg_ref, wu_ref, wd_ref)


def _ffn_call(x, g2, wg, wu, wd):
    t = x.shape[0]
    tile = pl.BlockSpec((TOKEN_TILE, D_MODEL), lambda i: (i, 0))
    return pl.pallas_call(
        _ffn_kernel,
        grid=(t // TOKEN_TILE,),
        in_specs=[tile, _const_spec((2, D_MODEL)), _const_spec(wg.shape),
                  _const_spec(wu.shape), _const_spec(wd.shape)],
        out_specs=tile,
        out_shape=jax.ShapeDtypeStruct(x.shape, F32),
        compiler_params=_params("parallel"),
        name="ffn",
    )(x, g2, wg, wu, wd)


def _mix_ffn_kernel(n_parts, *refs):
    x_ref = refs[0]
    part_refs = refs[1:1 + n_parts]
    wo_ref, g_ref, wg_ref, wu_ref, wd_ref, o_ref = refs[1 + n_parts:]
    g = g_ref[...]
    m = None
    row = 0
    for p_ref in part_refs:
        w = p_ref.shape[-1]
        d = _dot(p_ref[...], wo_ref[row:row + w, :])
        m = d if m is None else m + d
        row += w
    x1 = x_ref[...] + _rms(m, g[0:1])
    o_ref[...] = _ffn_apply(x1, g[1:2], g[2:3], wg_ref, wu_ref, wd_ref)


def _mix_ffn_call(x, parts, wo, g3, wg, wu, wd):
    t = x.shape[0]
    tile = pl.BlockSpec((TOKEN_TILE, D_MODEL), lambda i: (i, 0))
    part_specs = [pl.BlockSpec((TOKEN_TILE, p.shape[-1]), lambda i: (i, 0)) for p in parts]
    return pl.pallas_call(
        functools.partial(_mix_ffn_kernel, len(parts)),
        grid=(t // TOKEN_TILE,),
        in_specs=[tile] + part_specs + [_const_spec(wo.shape), _const_spec((3, D_MODEL)),
                                        _const_spec(wg.shape), _const_spec(wu.shape),
                                        _const_spec(wd.shape)],
        out_specs=tile,
        out_shape=jax.ShapeDtypeStruct(x.shape, F32),
        compiler_params=_params("parallel"),
        name="mix_ffn",
    )(x, *parts, wo, g3, wg, wu, wd)


def _even_pre_kernel(x_ref, g_ref, win_ref, gq_ref, wuq_ref, gkv_ref, wukv_ref,
                     c_ref, sa_ref, sb_ref, q_ref, k_ref, v_ref, zh_ref):
    h = _rms(x_ref[...], g_ref[...]).astype(BF16)
    z = _dot(h, win_ref[...])
    o_kv = MLA_Q_RANK
    o_pe = o_kv + MLA_KV_RANK
    o_h = o_pe + LANES
    zh_ref[...] = z[:, o_h:]
    c, sa, sb = c_ref[...], sa_ref[...], sb_ref[...]
    half = MLA_ROPE // 2
    scale = (MLA_NOPE + MLA_ROPE) ** -0.5

    cq = _rms(z[:, :o_kv], gq_ref[...]).astype(BF16)
    q = _dot(cq, wuq_ref[...]) * scale
    for hd in range(MLA_HEADS):
        base = hd * MLA_QK_PAD
        q_ref[:, base:base + LANES] = q[:, base:base + LANES].astype(BF16)
        q_ref[:, base + LANES:base + 2 * LANES] = _rope_slab(
            q[:, base + LANES:base + 2 * LANES], c, sa, sb, half).astype(BF16)

    ckv = _rms(z[:, o_kv:o_pe], gkv_ref[...]).astype(BF16)
    kv = _dot(ckv, wukv_ref[...])
    kpe = _rope_slab(z[:, o_pe:o_h], c, sa, sb, half).astype(BF16)
    for hd in range(MLA_HEADS):
        base = hd * MLA_QK_PAD
        k_ref[:, base:base + LANES] = kv[:, hd * LANES:(hd + 1) * LANES].astype(BF16)
        k_ref[:, base + LANES:base + 2 * LANES] = kpe
    _store_values(v_ref, kv[:, MLA_HEADS * MLA_NOPE:], MLA_HEADS)


def _even_pre_call(x, g, win, gq, wuq, gkv, wukv, tabs):
    t = x.shape[0]
    row = lambda w: pl.BlockSpec((TOKEN_TILE, w), lambda i: (i, 0))
    qk_w = MLA_HEADS * MLA_QK_PAD
    v_w = MLA_HEADS * V_PAD
    zh_w = 4 * HG_HEADS * HG_DK
    return pl.pallas_call(
        _even_pre_kernel,
        grid=(t // TOKEN_TILE,),
        in_specs=[row(D_MODEL), _const_spec((1, D_MODEL)), _const_spec(win.shape),
                  _const_spec(gq.shape), _const_spec(wuq.shape), _const_spec(gkv.shape),
                  _const_spec(wukv.shape), row(LANES), row(LANES), row(LANES)],
        out_specs=[row(qk_w), row(qk_w), row(v_w), row(zh_w)],
        out_shape=[jax.ShapeDtypeStruct((t, qk_w), BF16), jax.ShapeDtypeStruct((t, qk_w), BF16),
                   jax.ShapeDtypeStruct((t, v_w), BF16), jax.ShapeDtypeStruct((t, zh_w), F32)],
        compiler_params=_params("parallel"),
        name="even_pre",
    )(x, g, win, gq, wuq, gkv, wukv, *tabs)


def _odd_pre_kernel(x_ref, g_ref, win_ref, c_ref, sa_ref, sb_ref, q_ref, k_ref, v_ref):
    h = _rms(x_ref[...], g_ref[...]).astype(BF16)
    c, sa, sb = c_ref[...], sa_ref[...], sb_ref[...]
    half = DF_ROT // 2
    width = DF_HEADS * 2 * DF_DH
    scale = DF_DH ** -0.5
    q = _dot(h, win_ref[:, :width]) * scale
    k = _dot(h, win_ref[:, width:2 * width])
    for j in range(width // LANES):
        sl = slice(j * LANES, (j + 1) * LANES)
        q_ref[:, sl] = _rope_slab(q[:, sl], c, sa, sb, half).astype(BF16)
        k_ref[:, sl] = _rope_slab(k[:, sl], c, sa, sb, half).astype(BF16)
    _store_values(v_ref, _dot(h, win_ref[:, 2 * width:]), DF_HEADS)


def _odd_pre_call(x, g, win, tabs):
    t = x.shape[0]
    width = DF_HEADS * 2 * DF_DH
    row = lambda w: pl.BlockSpec((TOKEN_TILE, w), lambda i: (i, 0))
    return pl.pallas_call(
        _odd_pre_kernel,
        grid=(t // TOKEN_TILE,),
        in_specs=[row(D_MODEL), _const_spec((1, D_MODEL)), _const_spec(win.shape),
                  row(LANES), row(LANES), row(LANES)],
        out_specs=[row(width), row(width), row(DF_HEADS * V_PAD)],
        out_shape=[jax.ShapeDtypeStruct((t, width), BF16), jax.ShapeDtypeStruct((t, width), BF16),
                   jax.ShapeDtypeStruct((t, DF_HEADS * V_PAD), BF16)],
        compiler_params=_params("parallel"),
        name="odd_pre",
    )(x, g, win, *tabs)


def _flash(streams, q_chunk, k_ref, v_ref, row0, tq):
    m_rows = streams[0][0].shape[0]

    def step(start, width, carries, masked):
        scores = [_dot_nt(q, k_ref[0, pl.ds(start, width), ksl]) for q, ksl, _ in streams]
        probs = []
        for s, (m_i, _, _) in zip(scores, carries):
            if masked:
                k_chunk = (start + (width - tq) + lax.broadcasted_iota(jnp.int32, (1, tq), 1)) // CHUNK
                tail = jnp.where(k_chunk <= q_chunk, s[:, width - tq:], NEG_INF)
                s = tail if width == tq else jnp.concatenate([s[:, :width - tq], tail], axis=1)
            m_new = jnp.maximum(m_i, jnp.max(s, axis=-1, keepdims=True))
            probs.append((m_new, jnp.exp(m_i - m_new), jnp.exp(s - m_new).astype(BF16)))
        outs = []
        for (m_new, alpha, p), (_, _, vsl), (_, l_i, acc) in zip(probs, streams, carries):
            pv = _dot(p, v_ref[0, pl.ds(start, width), vsl])
            outs.append((m_new, alpha * l_i + pv[:, LANES:], alpha * acc + pv[:, :LANES]))
        return tuple(outs)

    init = tuple((jnp.full((m_rows, 1), NEG_INF, F32), jnp.zeros((m_rows, LANES), F32),
                  jnp.zeros((m_rows, LANES), F32)) for _ in streams)
    n_full = row0 // KEY_BLOCK
    carries = lax.fori_loop(
        0, n_full,
        lambda j, c: step(pl.multiple_of(j * KEY_BLOCK, KEY_BLOCK), KEY_BLOCK, c, False), init)
    start = pl.multiple_of(n_full * KEY_BLOCK, KEY_BLOCK)
    diag = [functools.partial(step, start, tq * (p + 1), masked=True) for p in range(KEY_BLOCK // tq)]
    carries = lax.switch((row0 % KEY_BLOCK) // tq, diag, carries)
    return [acc / l_i for _, l_i, acc in carries]


def _row_chunks(row0, n_rows):
    rows = row0 + lax.broadcasted_iota(jnp.int32, (n_rows, 1), 0)
    return rows // CHUNK


def _mla_attn_kernel(q_ref, k_ref, v_ref, o_ref):
    row0 = pl.program_id(2) * MLA_Q_TILE
    streams = [(q_ref[0, :, h * MLA_QK_PAD:(h + 1) * MLA_QK_PAD],
                slice(h * MLA_QK_PAD, (h + 1) * MLA_QK_PAD), slice(h * V_PAD, (h + 1) * V_PAD))
               for h in range(ATTN_HEADS_PER_STEP)]
    outs = _flash(streams, _row_chunks(row0, MLA_Q_TILE), k_ref, v_ref, row0, MLA_Q_TILE)
    for h, o in enumerate(outs):
        o_ref[0, :, h * MLA_V:(h + 1) * MLA_V] = o.astype(BF16)


def _mla_attn_call(q, k, v):
    b, s, _ = q.shape
    hp = ATTN_HEADS_PER_STEP
    return pl.pallas_call(
        _mla_attn_kernel,
        grid=(b, MLA_HEADS // hp, s // MLA_Q_TILE),
        in_specs=[pl.BlockSpec((1, MLA_Q_TILE, hp * MLA_QK_PAD), lambda bi, h, i: (bi, i, h)),
                  pl.BlockSpec((1, s, hp * MLA_QK_PAD), lambda bi, h, i: (bi, 0, h)),
                  pl.BlockSpec((1, s, hp * V_PAD), lambda bi, h, i: (bi, 0, h))],
        out_specs=pl.BlockSpec((1, MLA_Q_TILE, hp * MLA_V), lambda bi, h, i: (bi, i, h)),
        out_shape=jax.ShapeDtypeStruct((b, s, MLA_HEADS * MLA_V), BF16),
        compiler_params=_params("parallel", "parallel", "arbitrary"),
        name="mla_attn",
    )(q, k, v)


def _diff_attn_kernel(lambda_init, q_ref, k_ref, v_ref, lam_ref, gh_ref, o_ref):
    row0 = pl.program_id(2) * DF_Q_TILE
    dv = 2 * DF_DH
    lane = lax.broadcasted_iota(jnp.int32, (1, dv), 1)
    streams = []
    for h in range(ATTN_HEADS_PER_STEP):
        hs = slice(h * dv, (h + 1) * dv)
        q = q_ref[0, :, hs]
        zero = jnp.zeros_like(q)
        q2 = jnp.concatenate([jnp.where(lane < DF_DH, q, zero), jnp.where(lane >= DF_DH, q, zero)], axis=0)
        streams.append((q2, hs, slice(h * V_PAD, (h + 1) * V_PAD)))
    chunks = _row_chunks(row0, DF_Q_TILE)
    outs = _flash(streams, jnp.concatenate([chunks, chunks], axis=0), k_ref, v_ref, row0, DF_Q_TILE)
    lp = lam_ref[...]
    lam = (jnp.exp(jnp.sum(lp[0:1] * lp[1:2], axis=-1, keepdims=True))
           - jnp.exp(jnp.sum(lp[2:3] * lp[3:4], axis=-1, keepdims=True)) + lambda_init)
    for h, o2 in enumerate(outs):
        hs = slice(h * dv, (h + 1) * dv)
        o = o2[:DF_Q_TILE] - lam * o2[DF_Q_TILE:]
        o_ref[0, :, hs] = (_rms(o, gh_ref[:, hs]) * (1.0 - lambda_init)).astype(BF16)


def _diff_attn_call(q, k, v, lam_p, g_head, lambda_init):
    b, s, _ = q.shape
    hp = ATTN_HEADS_PER_STEP
    w = hp * 2 * DF_DH
    return pl.pallas_call(
        functools.partial(_diff_attn_kernel, lambda_init),
        grid=(b, DF_HEADS // hp, s // DF_Q_TILE),
        in_specs=[pl.BlockSpec((1, DF_Q_TILE, w), lambda bi, h, i: (bi, i, h)),
                  pl.BlockSpec((1, s, w), lambda bi, h, i: (bi, 0, h)),
                  pl.BlockSpec((1, s, hp * V_PAD), lambda bi, h, i: (bi, 0, h)),
                  pl.BlockSpec(lam_p.shape, lambda bi, h, i: (0, 0)),
                  pl.BlockSpec((None, 1, w), lambda bi, h, i: (h, 0, 0))],
        out_specs=pl.BlockSpec((1, DF_Q_TILE, w), lambda bi, h, i: (bi, i, h)),
        out_shape=jax.ShapeDtypeStruct((b, s, DF_HEADS * 2 * DF_DH), BF16),
        compiler_params=_params("parallel", "parallel", "arbitrary"),
        name="diff_attn",
    )(q, k, v, lam_p, g_head.reshape(DF_HEADS // hp, 1, w))


def _hgrn_kernel(hq_ref, hf_ref, hi_ref, hg_ref, lb_ref, go_ref, o_ref,
                 st_ref, b_ref, kk_ref, bpad_ref, kpad_ref, vpad_ref):
    n_chunks = HG_SEQ_TILE // CHUNK
    mid = CHUNK // 2 - 1

    @pl.when(pl.program_id(1) == 0)
    def _():
        st_ref[...] = jnp.zeros_like(st_ref)

    row_in_chunk = lax.broadcasted_iota(jnp.int32, (HG_SEQ_TILE, 1), 0) % CHUNK
    for hd in range(HG_HEADS):
        ls = slice(hd * HG_DK, (hd + 1) * HG_DK)
        lb = lb_ref[:, ls]
        sig = jax.nn.sigmoid(hf_ref[:, ls])
        kk_ref[hd] = (1.0 - lb) * (1.0 - sig)
        b = jnp.log(jnp.maximum(lb + (1.0 - lb) * sig, TINY))
        step = 1
        while step < CHUNK:
            b = b + jnp.where(row_in_chunk >= step, pltpu.roll(b, step, 0), 0.0)
            step *= 2
        b_ref[hd] = b

    b_mid = b_ref[:, pl.ds(mid, n_chunks, stride=CHUNK), :]
    b_end = b_ref[:, pl.ds(CHUNK - 1, n_chunks, stride=CHUNK), :]
    worst = jnp.max(jnp.maximum(-b_mid, b_mid - b_end))

    row = lax.broadcasted_iota(jnp.int32, (CHUNK, 1), 0)

    def load(ci, hd):
        rows = pl.ds(pl.multiple_of(ci * CHUNK, CHUNK), CHUNK)
        ls = slice(hd * HG_DK, (hd + 1) * HG_DK)
        return rows, ls, hq_ref[rows, ls], hi_ref[rows, ls], b_ref[hd, rows, :], kk_ref[hd, rows, :]

    def carry_state(hd, qh, vh, b, kk):
        st = st_ref[hd]
        o = _dot_nt((qh * jnp.exp(b)).astype(BF16), st.astype(BF16))
        b_last = b[CHUNK - 1:CHUNK]
        kdec = kk * jnp.exp(b_last - b)
        st_ref[hd] = st * jnp.exp(b_last) + _dot_tn(vh.astype(BF16), kdec.astype(BF16))
        return o

    def finish(rows, ls, o):
        on = _rms(o, go_ref[:, ls]) * jax.nn.silu(hg_ref[rows, ls])
        o_ref[rows, ls] = on.astype(BF16)

    @pl.when(worst < HG_SAFE_LOG)
    def _():
        causal = row >= lax.broadcasted_iota(jnp.int32, (1, CHUNK), 1)

        def chunk_body(cj, carry):
            for u in range(HG_UNROLL):
                ci = cj * HG_UNROLL + u
                for hd in range(HG_HEADS):
                    rows, ls, qh, vh, b, kk = load(ci, hd)
                    o = carry_state(hd, qh, vh, b, kk)
                    b_m = b[mid:mid + 1]
                    qf = (qh * jnp.exp(b - b_m)).astype(BF16)
                    kf = (kk * jnp.exp(b_m - b)).astype(BF16)
                    a = jnp.where(causal, _dot_nt(qf, kf), 0.0)
                    finish(rows, ls, o + _dot(a.astype(BF16), vh.astype(BF16)))
            return carry

        lax.fori_loop(0, n_chunks // HG_UNROLL, chunk_body, 0)

    @pl.when(jnp.logical_not(worst < HG_SAFE_LOG))
    def _():
        n_sub = CHUNK // HG_SUB
        off_w = HG_SUB * (n_sub * (n_sub - 1) // 2)
        zpad = jnp.zeros((HG_HEADS, HG_SUB, HG_DK), F32)
        bpad_ref[:, :HG_SUB, :] = zpad
        kpad_ref[:, :HG_SUB, :] = zpad
        vpad_ref[:, :HG_SUB, :] = zpad
        row_sub = row % HG_SUB
        r2 = lax.broadcasted_iota(jnp.int32, (2 * HG_DK, 2 * HG_DK), 0) // HG_DK
        c2 = lax.broadcasted_iota(jnp.int32, (2 * HG_DK, 2 * HG_DK), 1) // HG_DK
        ones2 = jnp.where(r2 == c2, 1.0, 0.0).astype(BF16)
        col = lax.broadcasted_iota(jnp.int32, (1, off_w), 1)
        col_blk = jnp.zeros((1, off_w), jnp.int32)
        for i in range(1, n_sub):
            col_blk = col_blk + jnp.where(col >= HG_SUB * (i * (i - 1) // 2), 1, 0)
        off_mask = col_blk == (row // HG_SUB)

        def chunk_body(ci, carry):
            for hd in range(HG_HEADS):
                rows, ls, qh, vh, b, kk = load(ci, hd)
                o = carry_state(hd, qh, vh, b, kk)

                refs = [b[i * HG_SUB - 1:i * HG_SUB] for i in range(1, n_sub)]
                bref = jnp.concatenate(
                    [jnp.zeros((HG_SUB, HG_DK), F32)]
                    + [jnp.broadcast_to(r, (HG_SUB, HG_DK)) for r in refs], axis=0)
                qs = qh * jnp.exp(b - bref)
                kst = jnp.concatenate(
                    [kk[:i * HG_SUB] * jnp.exp(refs[i - 1] - b[:i * HG_SUB]) for i in range(1, n_sub)],
                    axis=0)
                vst = jnp.concatenate([vh[:i * HG_SUB] for i in range(1, n_sub)], axis=0)
                a_off = jnp.where(off_mask, _dot_nt(qs.astype(BF16), kst.astype(BF16)), 0.0)
                o = o + _dot(a_off.astype(BF16), vst.astype(BF16))

                bpad_ref[hd, HG_SUB:, :] = b
                kpad_ref[hd, HG_SUB:, :] = kk
                vpad_ref[hd, HG_SUB:, :] = vh
                for dp in range(HG_SUB // 2):
                    terms = []
                    for d in (2 * dp, 2 * dp + 1):
                        lo = HG_SUB - d
                        w = qh * jnp.exp(b - bpad_ref[hd, lo:lo + CHUNK, :]) * kpad_ref[hd, lo:lo + CHUNK, :]
                        terms.append(jnp.where(row_sub >= d, w, 0.0))
                    dsum = _dot(jnp.concatenate(terms, axis=1).astype(BF16), ones2)
                    for j, d in enumerate((2 * dp, 2 * dp + 1)):
                        lo = HG_SUB - d
                        o = o + dsum[:, j * HG_DK:(j + 1) * HG_DK] * vpad_ref[hd, lo:lo + CHUNK, :]
                finish(rows, ls, o)
            return carry

        lax.fori_loop(0, n_chunks, chunk_body, 0)


def _hgrn_call(zh, lb, g_out, b, s):
    width = HG_HEADS * HG_DK
    n_seq = s // HG_SEQ_TILE
    part = lambda j: pl.BlockSpec((HG_SEQ_TILE, width), lambda bi, si: (bi * n_seq + si, j))
    pad = pltpu.VMEM((HG_HEADS, HG_SUB + CHUNK, HG_DK), F32)
    return pl.pallas_call(
        _hgrn_kernel,
        grid=(b, n_seq),
        in_specs=[part(0), part(1), part(2), part(3),
                  pl.BlockSpec((1, width), lambda bi, si: (0, 0)),
                  pl.BlockSpec((1, width), lambda bi, si: (0, 0))],
        out_specs=pl.BlockSpec((HG_SEQ_TILE, width), lambda bi, si: (bi * n_seq + si, 0)),
        out_shape=jax.ShapeDtypeStruct((b * s, width), BF16),
        scratch_shapes=[pltpu.VMEM((HG_HEADS, HG_DV, HG_DK), F32),
                        pltpu.VMEM((HG_HEADS, HG_SEQ_TILE, HG_DK), F32),
                        pltpu.VMEM((HG_HEADS, HG_SEQ_TILE, HG_DK), F32),
                        pad, pad, pad],
        compiler_params=_params("parallel", "arbitrary"),
        name="hgrn2",
    )(zh, zh, zh, zh, lb, g_out)


def _rope_tables(positions, dim, group):
    half = dim // 2
    inv_freq = ROPE_THETA ** (-jnp.arange(0, dim, 2, dtype=F32) / dim)
    ang = positions.astype(F32).reshape(-1, 1) * inv_freq
    cos, sin = jnp.cos(ang), jnp.sin(ang)
    t = ang.shape[0]
    rest = group - dim
    c = jnp.concatenate([cos, cos, jnp.ones((t, rest), F32)], axis=1)
    sa = jnp.concatenate([-sin, jnp.zeros((t, half + rest), F32)], axis=1)
    sb = jnp.concatenate([jnp.zeros((t, half), F32), sin, jnp.zeros((t, rest), F32)], axis=1)
    rep = LANES // group
    return tuple(jnp.tile(a, (1, rep)) for a in (c, sa, sb))


def kernel(x, positions, norm_g, ffn_w_gate, ffn_w_up, ffn_w_down, ev_w_in, ev_g_q, ev_w_uq, ev_g_kv, ev_w_ukv, ev_lb_logits, ev_g_out, ev_w_out, od_w_in, od_lambda, od_g_head, od_w_out):
    b, s, d = x.shape
    t = b * s
    xt = x.reshape(t, d)
    tabs_m = _rope_tables(positions, MLA_ROPE, LANES)
    tabs_d = _rope_tables(positions, DF_ROT, DF_DH)
    lb_w = jax.nn.softmax(ev_lb_logits.astype(F32), axis=0)
    lb_all = jnp.cumsum(lb_w, axis=0) - lb_w[0:1]

    for l in range(DEPTH):
        g = norm_g[l]
        wg = ffn_w_gate[l].astype(BF16)
        wu = ffn_w_up[l].astype(BF16)
        wd = ffn_w_down[l].astype(BF16)
        xt = _ffn_call(xt, g[0], wg[0], wu[0], wd[0])
        j = l // 2
        if l % 2 == 0:
            w_in = ev_w_in[j]
            o_pe = MLA_Q_RANK + MLA_KV_RANK + MLA_ROPE
            w_in = jnp.concatenate(
                [w_in[:, :o_pe], jnp.zeros((d, LANES - MLA_ROPE), F32), w_in[:, o_pe:]], axis=1)
            w_uq = ev_w_uq[j].reshape(MLA_Q_RANK, MLA_HEADS, MLA_NOPE + MLA_ROPE)
            w_uq = jnp.pad(w_uq, ((0, 0), (0, 0), (0, MLA_QK_PAD - MLA_NOPE - MLA_ROPE)))
            w_uq = w_uq.reshape(MLA_Q_RANK, MLA_HEADS * MLA_QK_PAD)
            w_ukv = ev_w_ukv[j].reshape(MLA_KV_RANK, MLA_HEADS, 2, MLA_NOPE)
            w_ukv = w_ukv.transpose(0, 2, 1, 3).reshape(MLA_KV_RANK, 2 * MLA_HEADS * MLA_NOPE)
            q, k, v, zh = _even_pre_call(
                xt, g[1, 0:1], w_in.astype(BF16), ev_g_q[j].reshape(1, -1), w_uq.astype(BF16),
                ev_g_kv[j].reshape(1, -1), w_ukv.astype(BF16), tabs_m)
            o_a = _mla_attn_call(q.reshape(b, s, -1), k.reshape(b, s, -1), v.reshape(b, s, -1))
            o_b = _hgrn_call(zh, lb_all[j].reshape(1, -1), ev_g_out[j].reshape(1, -1), b, s)
            parts = [o_a.reshape(t, -1), o_b]
            w_out = ev_w_out[j]
        else:
            lambda_init = 0.8 - 0.6 * math.exp(-0.3 * l)
            q, k, v = _odd_pre_call(xt, g[1, 0:1], od_w_in[j].astype(BF16), tabs_d)
            o = _diff_attn_call(q.reshape(b, s, -1), k.reshape(b, s, -1), v.reshape(b, s, -1),
                                od_lambda[j], od_g_head[j], lambda_init)
            parts = [o.reshape(t, -1)]
            w_out = od_w_out[j]
        g3 = jnp.stack([g[1, 1], g[2, 0], g[2, 1]])
        xt = _mix_ffn_call(xt, parts, w_out.astype(BF16), g3, wg[1], wu[1], wd[1])
    return xt.reshape(b, s, d)
```

```python
import functools
import math

import jax
import jax.numpy as jnp
from jax import lax
from jax.experimental import pallas as pl
from jax.experimental.pallas import tpu as pltpu

D_MODEL = 1024
DEPTH = 4
CHUNK = 64
ROPE_THETA = 500000.0
EPS = 1e-6
NEG_INF = -1e30
TINY = 1e-30
D_FF = 2816
MLA_HEADS = 4
MLA_NOPE = 128
MLA_ROPE = 64
MLA_V = 128
MLA_Q_RANK = 384
MLA_KV_RANK = 256
HG_HEADS = 4
HG_DK = 128
HG_DV = 128
DF_HEADS = 8
DF_DH = 64
DF_ROT = DF_DH // 4

LANES = 128
MXU_DIM = 256
VMEM_LIMIT = 56 * 1024 * 1024

TOKEN_TILE = 512
MLA_Q_TILE = 512
DF_Q_TILE = 256
KEY_BLOCK = 1024
ATTN_HEADS_PER_STEP = 2
HG_SEQ_TILE = 1024
HG_SUB = 16
HG_UNROLL = 4
HG_SAFE_LOG = 75.0
FF_CHUNKS = ((0, 1024), (1024, 2048), (2048, D_FF))

MLA_QK_PAD = 2 * LANES
V_PAD = 2 * LANES
EVEN_Z = MLA_Q_RANK + MLA_KV_RANK + LANES + 4 * HG_HEADS * HG_DK

F32 = jnp.float32
BF16 = jnp.bfloat16


def _rms(x, g):
    ms = jnp.mean(x * x, axis=-1, keepdims=True)
    return x * lax.rsqrt(ms + EPS) * g


def _dot(a, b):
    return jnp.dot(a, b, preferred_element_type=F32)


def _dot_nt(a, b):
    return lax.dot_general(a, b, (((1,), (1,)), ((), ())), preferred_element_type=F32)


def _dot_tn(a, b):
    return lax.dot_general(a, b, (((0,), (0,)), ((), ())), preferred_element_type=F32)


def _rope_slab(x, c, sa, sb, half):
    return x * c + pltpu.roll(x, LANES - half, 1) * sa + pltpu.roll(x, half, 1) * sb


def _store_values(v_ref, v, heads):
    ones = jnp.ones((v.shape[0], V_PAD - LANES), BF16)
    for hd in range(heads):
        v_ref[:, hd * V_PAD:hd * V_PAD + LANES] = v[:, hd * LANES:(hd + 1) * LANES].astype(BF16)
        v_ref[:, hd * V_PAD + LANES:(hd + 1) * V_PAD] = ones


class _Pick:
    def __init__(self, arr, *idx):
        self.arr = arr
        nd = arr.ndim - len(idx)
        self.spec = pl.BlockSpec((None,) * len(idx) + arr.shape[len(idx):],
                                 lambda *_: idx + (0,) * nd, pipeline_mode=pl.Buffered(1))


def _params(*sem):
    return pltpu.CompilerParams(dimension_semantics=sem, vmem_limit_bytes=VMEM_LIMIT)


def _ffn_apply(x, gpre, gpost, wg_ref, wu_ref, wd_ref):
    xn = _rms(x, gpre).astype(BF16)
    acc = None
    for lo, hi in FF_CHUNKS:
        g = _dot(xn, wg_ref[:, lo:hi])
        u = _dot(xn, wu_ref[:, lo:hi])
        a = (jax.nn.silu(g) * u).astype(BF16)
        h = _dot(a, wd_ref[lo:hi, :])
        acc = h if acc is None else acc + h
    return x + 0.5 * _rms(acc, gpost)


def _ffn_kernel(x_ref, g_ref, wg_ref, wu_ref, wd_ref, o_ref):
    g = g_ref[...]
    o_ref[...] = _ffn_apply(x_ref[...], g[0:1], g[1:2], wg_ref, wu_ref, wd_ref)


def _ffn_call(x, g2, wg, wu, wd):
    t = x.shape[0]
    tile = pl.BlockSpec((TOKEN_TILE, D_MODEL), lambda i: (i, 0))
    picks = [g2, wg, wu, wd]
    return pl.pallas_call(
        _ffn_kernel,
        grid=(t // TOKEN_TILE,),
        in_specs=[tile] + [p.spec for p in picks],
        out_specs=tile,
        out_shape=jax.ShapeDtypeStruct(x.shape, F32),
        compiler_params=_params("parallel"),
        name="ffn",
    )(x, *[p.arr for p in picks])


def _mix_ffn_kernel(n_parts, *refs):
    x_ref = refs[0]
    part_refs = refs[1:1 + n_parts]
    wo_ref, gm_ref, gf_ref, wg_ref, wu_ref, wd_ref, o_ref = refs[1 + n_parts:]
    gf = gf_ref[...]
    m = None
    row = 0
    for p_ref in part_refs:
        w = p_ref.shape[-1]
        d = _dot(p_ref[...], wo_ref[row:row + w, :])
        m = d if m is None else m + d
        row += w
    x1 = x_ref[...] + _rms(m, gm_ref[1:2])
    o_ref[...] = _ffn_apply(x1, gf[0:1], gf[1:2], wg_ref, wu_ref, wd_ref)


def _mix_ffn_call(x, parts, wo, g_mix, g_ffn, wg, wu, wd):
    t = x.shape[0]
    tile = pl.BlockSpec((TOKEN_TILE, D_MODEL), lambda i: (i, 0))
    part_specs = [pl.BlockSpec((TOKEN_TILE, p.shape[-1]), lambda i: (i, 0)) for p in parts]
    picks = [wo, g_mix, g_ffn, wg, wu, wd]
    return pl.pallas_call(
        functools.partial(_mix_ffn_kernel, len(parts)),
        grid=(t // TOKEN_TILE,),
        in_specs=[tile] + part_specs + [p.spec for p in picks],
        out_specs=tile,
        out_shape=jax.ShapeDtypeStruct(x.shape, F32),
        compiler_params=_params("parallel"),
        name="mix_ffn",
    )(x, *parts, *[p.arr for p in picks])


def _even_pre_kernel(x_ref, g_ref, win_ref, gq_ref, wuq_ref, gkv_ref, wukv_ref,
                     c_ref, sa_ref, sb_ref, q_ref, k_ref, v_ref, zh_ref):
    h = _rms(x_ref[...], g_ref[0:1]).astype(BF16)
    z = _dot(h, win_ref[...])
    o_kv = MLA_Q_RANK
    o_pe = o_kv + MLA_KV_RANK
    o_h = o_pe + LANES
    zh_ref[...] = z[:, o_h:]
    c, sa, sb = c_ref[...], sa_ref[...], sb_ref[...]
    half = MLA_ROPE // 2
    scale = (MLA_NOPE + MLA_ROPE) ** -0.5

    cq = _rms(z[:, :o_kv], gq_ref[...]).astype(BF16)
    q = _dot(cq, wuq_ref[...]) * scale
    for hd in range(MLA_HEADS):
        base = hd * MLA_QK_PAD
        q_ref[:, base:base + LANES] = q[:, base:base + LANES].astype(BF16)
        q_ref[:, base + LANES:base + 2 * LANES] = _rope_slab(
            q[:, base + LANES:base + 2 * LANES], c, sa, sb, half).astype(BF16)

    ckv = _rms(z[:, o_kv:o_pe], gkv_ref[...]).astype(BF16)
    kv = _dot(ckv, wukv_ref[...])
    kpe = _rope_slab(z[:, o_pe:o_h], c, sa, sb, half).astype(BF16)
    for hd in range(MLA_HEADS):
        base = hd * MLA_QK_PAD
        k_ref[:, base:base + LANES] = kv[:, hd * LANES:(hd + 1) * LANES].astype(BF16)
        k_ref[:, base + LANES:base + 2 * LANES] = kpe
    _store_values(v_ref, kv[:, MLA_HEADS * MLA_NOPE:], MLA_HEADS)


def _even_pre_call(x, g, win, gq, wuq, gkv, wukv, tabs):
    t = x.shape[0]
    picks = [g, win, gq, wuq, gkv, wukv]
    row = lambda w: pl.BlockSpec((TOKEN_TILE, w), lambda i: (i, 0))
    qk_w = MLA_HEADS * MLA_QK_PAD
    v_w = MLA_HEADS * V_PAD
    zh_w = 4 * HG_HEADS * HG_DK
    return pl.pallas_call(
        _even_pre_kernel,
        grid=(t // TOKEN_TILE,),
        in_specs=[row(D_MODEL)] + [p.spec for p in picks] + [row(LANES), row(LANES), row(LANES)],
        out_specs=[row(qk_w), row(qk_w), row(v_w), row(zh_w)],
        out_shape=[jax.ShapeDtypeStruct((t, qk_w), BF16), jax.ShapeDtypeStruct((t, qk_w), BF16),
                   jax.ShapeDtypeStruct((t, v_w), BF16), jax.ShapeDtypeStruct((t, zh_w), F32)],
        compiler_params=_params("parallel"),
        name="even_pre",
    )(x, *[p.arr for p in picks], *tabs)


def _odd_pre_kernel(x_ref, g_ref, win_ref, c_ref, sa_ref, sb_ref, q_ref, k_ref, v_ref):
    h = _rms(x_ref[...], g_ref[0:1]).astype(BF16)
    c, sa, sb = c_ref[...], sa_ref[...], sb_ref[...]
    half = DF_ROT // 2
    width = DF_HEADS * 2 * DF_DH
    scale = DF_DH ** -0.5
    q = _dot(h, win_ref[:, :width]) * scale
    k = _dot(h, win_ref[:, width:2 * width])
    for j in range(width // LANES):
        sl = slice(j * LANES, (j + 1) * LANES)
        q_ref[:, sl] = _rope_slab(q[:, sl], c, sa, sb, half).astype(BF16)
        k_ref[:, sl] = _rope_slab(k[:, sl], c, sa, sb, half).astype(BF16)
    _store_values(v_ref, _dot(h, win_ref[:, 2 * width:]), DF_HEADS)


def _odd_pre_call(x, g, win, tabs):
    t = x.shape[0]
    width = DF_HEADS * 2 * DF_DH
    row = lambda w: pl.BlockSpec((TOKEN_TILE, w), lambda i: (i, 0))
    return pl.pallas_call(
        _odd_pre_kernel,
        grid=(t // TOKEN_TILE,),
        in_specs=[row(D_MODEL), g.spec, win.spec, row(LANES), row(LANES), row(LANES)],
        out_specs=[row(width), row(width), row(DF_HEADS * V_PAD)],
        out_shape=[jax.ShapeDtypeStruct((t, width), BF16), jax.ShapeDtypeStruct((t, width), BF16),
                   jax.ShapeDtypeStruct((t, DF_HEADS * V_PAD), BF16)],
        compiler_params=_params("parallel"),
        name="odd_pre",
    )(x, g.arr, win.arr, *tabs)


def _flash(streams, q_chunk, k_ref, v_ref, row0, tq):
    m_rows = streams[0][0].shape[0]

    def step(start, width, carries, masked):
        scores = [_dot_nt(q, k_ref[0, pl.ds(start, width), ksl]) for q, ksl, _ in streams]
        probs = []
        for s, (m_i, _, _) in zip(scores, carries):
            if masked:
                k_chunk = (start + (width - tq) + lax.broadcasted_iota(jnp.int32, (1, tq), 1)) // CHUNK
                tail = jnp.where(k_chunk <= q_chunk, s[:, width - tq:], NEG_INF)
                s = tail if width == tq else jnp.concatenate([s[:, :width - tq], tail], axis=1)
            m_new = jnp.maximum(m_i, jnp.max(s, axis=-1, keepdims=True))
            probs.append((m_new, jnp.exp(m_i - m_new), jnp.exp(s - m_new).astype(BF16)))
        outs = []
        for (m_new, alpha, p), (_, _, vsl), (_, l_i, acc) in zip(probs, streams, carries):
            pv = _dot(p, v_ref[0, pl.ds(start, width), vsl])
            outs.append((m_new, alpha * l_i + pv[:, LANES:], alpha * acc + pv[:, :LANES]))
        return tuple(outs)

    init = tuple((jnp.full((m_rows, 1), NEG_INF, F32), jnp.zeros((m_rows, LANES), F32),
                  jnp.zeros((m_rows, LANES), F32)) for _ in streams)
    n_full = row0 // KEY_BLOCK

    def unmasked(n, c):
        for j in range(n):
            c = step(j * KEY_BLOCK, KEY_BLOCK, c, False)
        return c

    carries = lax.switch(
        n_full, [functools.partial(unmasked, n) for n in range(k_ref.shape[1] // KEY_BLOCK)], init)
    start = pl.multiple_of(n_full * KEY_BLOCK, KEY_BLOCK)
    diag = [functools.partial(step, start, tq * (p + 1), masked=True) for p in range(KEY_BLOCK // tq)]
    carries = lax.switch((row0 % KEY_BLOCK) // tq, diag, carries)
    return [acc / l_i for _, l_i, acc in carries]


def _row_chunks(row0, n_rows):
    rows = row0 + lax.broadcasted_iota(jnp.int32, (n_rows, 1), 0)
    return rows // CHUNK


def _mla_attn_kernel(q_ref, k_ref, v_ref, o_ref):
    row0 = pl.program_id(2) * MLA_Q_TILE
    streams = [(q_ref[0, :, h * MLA_QK_PAD:(h + 1) * MLA_QK_PAD],
                slice(h * MLA_QK_PAD, (h + 1) * MLA_QK_PAD), slice(h * V_PAD, (h + 1) * V_PAD))
               for h in range(ATTN_HEADS_PER_STEP)]
    outs = _flash(streams, _row_chunks(row0, MLA_Q_TILE), k_ref, v_ref, row0, MLA_Q_TILE)
    for h, o in enumerate(outs):
        o_ref[0, :, h * MLA_V:(h + 1) * MLA_V] = o.astype(BF16)


def _mla_attn_call(q, k, v):
    b, s, _ = q.shape
    hp = ATTN_HEADS_PER_STEP
    return pl.pallas_call(
        _mla_attn_kernel,
        grid=(b, MLA_HEADS // hp, s // MLA_Q_TILE),
        in_specs=[pl.BlockSpec((1, MLA_Q_TILE, hp * MLA_QK_PAD), lambda bi, h, i: (bi, i, h)),
                  pl.BlockSpec((1, s, hp * MLA_QK_PAD), lambda bi, h, i: (bi, 0, h)),
                  pl.BlockSpec((1, s, hp * V_PAD), lambda bi, h, i: (bi, 0, h))],
        out_specs=pl.BlockSpec((1, MLA_Q_TILE, hp * MLA_V), lambda bi, h, i: (bi, i, h)),
        out_shape=jax.ShapeDtypeStruct((b, s, MLA_HEADS * MLA_V), BF16),
        compiler_params=_params("parallel", "parallel", "arbitrary"),
        name="mla_attn",
    )(q, k, v)


def _diff_attn_kernel(lambda_init, q_ref, k_ref, v_ref, lam_ref, gh_ref, o_ref):
    row0 = pl.program_id(2) * DF_Q_TILE
    dv = 2 * DF_DH
    lane = lax.broadcasted_iota(jnp.int32, (1, dv), 1)
    streams = []
    for h in range(ATTN_HEADS_PER_STEP):
        hs = slice(h * dv, (h + 1) * dv)
        q = q_ref[0, :, hs]
        zero = jnp.zeros_like(q)
        q2 = jnp.concatenate([jnp.where(lane < DF_DH, q, zero), jnp.where(lane >= DF_DH, q, zero)], axis=0)
        streams.append((q2, hs, slice(h * V_PAD, (h + 1) * V_PAD)))
    chunks = _row_chunks(row0, DF_Q_TILE)
    outs = _flash(streams, jnp.concatenate([chunks, chunks], axis=0), k_ref, v_ref, row0, DF_Q_TILE)
    lp = lam_ref[...]
    lam = (jnp.exp(jnp.sum(lp[0:1] * lp[1:2], axis=-1, keepdims=True))
           - jnp.exp(jnp.sum(lp[2:3] * lp[3:4], axis=-1, keepdims=True)) + lambda_init)
    for h, o2 in enumerate(outs):
        hs = slice(h * dv, (h + 1) * dv)
        o = o2[:DF_Q_TILE] - lam * o2[DF_Q_TILE:]
        o_ref[0, :, hs] = (_rms(o, gh_ref[:, hs]) * (1.0 - lambda_init)).astype(BF16)


def _diff_attn_call(q, k, v, lam_p, g_head, j, lambda_init):
    b, s, _ = q.shape
    hp = ATTN_HEADS_PER_STEP
    w = hp * 2 * DF_DH
    g_head = g_head.reshape(g_head.shape[0], DF_HEADS // hp, 1, w)
    return pl.pallas_call(
        functools.partial(_diff_attn_kernel, lambda_init),
        grid=(b, DF_HEADS // hp, s // DF_Q_TILE),
        in_specs=[pl.BlockSpec((1, DF_Q_TILE, w), lambda bi, h, i: (bi, i, h)),
                  pl.BlockSpec((1, s, w), lambda bi, h, i: (bi, 0, h)),
                  pl.BlockSpec((1, s, hp * V_PAD), lambda bi, h, i: (bi, 0, h)),
                  _Pick(lam_p, j).spec,
                  pl.BlockSpec((None, None, 1, w), lambda bi, h, i: (j, h, 0, 0))],
        out_specs=pl.BlockSpec((1, DF_Q_TILE, w), lambda bi, h, i: (bi, i, h)),
        out_shape=jax.ShapeDtypeStruct((b, s, DF_HEADS * 2 * DF_DH), BF16),
        compiler_params=_params("parallel", "parallel", "arbitrary"),
        name="diff_attn",
    )(q, k, v, lam_p, g_head)


def _hgrn_kernel(hq_ref, hf_ref, hi_ref, hg_ref, lb_ref, go_ref, o_ref,
                 st_ref, b_ref, kk_ref, bpad_ref, kpad_ref, vpad_ref):
    n_chunks = HG_SEQ_TILE // CHUNK
    mid = CHUNK // 2 - 1

    @pl.when(pl.program_id(1) == 0)
    def _():
        st_ref[...] = jnp.zeros_like(st_ref)

    row_in_chunk = lax.broadcasted_iota(jnp.int32, (HG_SEQ_TILE, 1), 0) % CHUNK
    for hd in range(HG_HEADS):
        ls = slice(hd * HG_DK, (hd + 1) * HG_DK)
        lb = lb_ref[:, ls]
        sig = jax.nn.sigmoid(hf_ref[:, ls])
        kk_ref[hd] = (1.0 - lb) * (1.0 - sig)
        b = jnp.log(jnp.maximum(lb + (1.0 - lb) * sig, TINY))
        step = 1
        while step < CHUNK:
            b = b + jnp.where(row_in_chunk >= step, pltpu.roll(b, step, 0), 0.0)
            step *= 2
        b_ref[hd] = b

    b_mid = b_ref[:, pl.ds(mid, n_chunks, stride=CHUNK), :]
    b_end = b_ref[:, pl.ds(CHUNK - 1, n_chunks, stride=CHUNK), :]
    worst = jnp.max(jnp.maximum(-b_mid, b_mid - b_end))

    row = lax.broadcasted_iota(jnp.int32, (CHUNK, 1), 0)

    def load(ci, hd):
        rows = pl.ds(pl.multiple_of(ci * CHUNK, CHUNK), CHUNK)
        ls = slice(hd * HG_DK, (hd + 1) * HG_DK)
        return rows, ls, hq_ref[rows, ls], hi_ref[rows, ls], b_ref[hd, rows, :], kk_ref[hd, rows, :]

    def carry_state(hd, qh, vh, b, kk):
        st = st_ref[hd]
        o = _dot_nt((qh * jnp.exp(b)).astype(BF16), st.astype(BF16))
        b_last = b[CHUNK - 1:CHUNK]
        kdec = kk * jnp.exp(b_last - b)
        st_ref[hd] = st * jnp.exp(b_last) + _dot_tn(vh.astype(BF16), kdec.astype(BF16))
        return o

    def finish(rows, ls, o):
        on = _rms(o, go_ref[:, ls]) * jax.nn.silu(hg_ref[rows, ls])
        o_ref[rows, ls] = on.astype(BF16)

    @pl.when(worst < HG_SAFE_LOG)
    def _():
        causal = row >= lax.broadcasted_iota(jnp.int32, (1, CHUNK), 1)

        def chunk_body(cj, carry):
            for u in range(HG_UNROLL):
                ci = cj * HG_UNROLL + u
                for hd in range(HG_HEADS):
                    rows, ls, qh, vh, b, kk = load(ci, hd)
                    o = carry_state(hd, qh, vh, b, kk)
                    b_m = b[mid:mid + 1]
                    qf = (qh * jnp.exp(b - b_m)).astype(BF16)
                    kf = (kk * jnp.exp(b_m - b)).astype(BF16)
                    a = jnp.where(causal, _dot_nt(qf, kf), 0.0)
                    finish(rows, ls, o + _dot(a.astype(BF16), vh.astype(BF16)))
            return carry

        lax.fori_loop(0, n_chunks // HG_UNROLL, chunk_body, 0)

    @pl.when(jnp.logical_not(worst < HG_SAFE_LOG))
    def _():
        n_sub = CHUNK // HG_SUB
        off_w = HG_SUB * (n_sub * (n_sub - 1) // 2)
        zpad = jnp.zeros((HG_HEADS, HG_SUB, HG_DK), F32)
        bpad_ref[:, :HG_SUB, :] = zpad
        kpad_ref[:, :HG_SUB, :] = zpad
        vpad_ref[:, :HG_SUB, :] = zpad
        row_sub = row % HG_SUB
        r2 = lax.broadcasted_iota(jnp.int32, (2 * HG_DK, 2 * HG_DK), 0) // HG_DK
        c2 = lax.broadcasted_iota(jnp.int32, (2 * HG_DK, 2 * HG_DK), 1) // HG_DK
        ones2 = jnp.where(r2 == c2, 1.0, 0.0).astype(BF16)
        col = lax.broadcasted_iota(jnp.int32, (1, off_w), 1)
        col_blk = jnp.zeros((1, off_w), jnp.int32)
        for i in range(1, n_sub):
            col_blk = col_blk + jnp.where(col >= HG_SUB * (i * (i - 1) // 2), 1, 0)
        off_mask = col_blk == (row // HG_SUB)

        def chunk_body(ci, carry):
            for hd in range(HG_HEADS):
                rows, ls, qh, vh, b, kk = load(ci, hd)
                o = carry_state(hd, qh, vh, b, kk)

                refs = [b[i * HG_SUB - 1:i * HG_SUB] for i in range(1, n_sub)]
                bref = jnp.concatenate(
                    [jnp.zeros((HG_SUB, HG_DK), F32)]
                    + [jnp.broadcast_to(r, (HG_SUB, HG_DK)) for r in refs], axis=0)
                qs = qh * jnp.exp(b - bref)
                kst = jnp.concatenate(
                    [kk[:i * HG_SUB] * jnp.exp(refs[i - 1] - b[:i * HG_SUB]) for i in range(1, n_sub)],
                    axis=0)
                vst = jnp.concatenate([vh[:i * HG_SUB] for i in range(1, n_sub)], axis=0)
                a_off = jnp.where(off_mask, _dot_nt(qs.astype(BF16), kst.astype(BF16)), 0.0)
                o = o + _dot(a_off.astype(BF16), vst.astype(BF16))

                bpad_ref[hd, HG_SUB:, :] = b
                kpad_ref[hd, HG_SUB:, :] = kk
                vpad_ref[hd, HG_SUB:, :] = vh
                for dp in range(HG_SUB // 2):
                    terms = []
                    for d in (2 * dp, 2 * dp + 1):
                        lo = HG_SUB - d
                        w = qh * jnp.exp(b - bpad_ref[hd, lo:lo + CHUNK, :]) * kpad_ref[hd, lo:lo + CHUNK, :]
                        terms.append(jnp.where(row_sub >= d, w, 0.0))
                    dsum = _dot(jnp.concatenate(terms, axis=1).astype(BF16), ones2)
                    for j, d in enumerate((2 * dp, 2 * dp + 1)):
                        lo = HG_SUB - d
                        o = o + dsum[:, j * HG_DK:(j + 1) * HG_DK] * vpad_ref[hd, lo:lo + CHUNK, :]
                finish(rows, ls, o)
            return carry

        lax.fori_loop(0, n_chunks, chunk_body, 0)


def _hgrn_call(zh, lb, g_out, b, s):
    width = HG_HEADS * HG_DK
    n_seq = s // HG_SEQ_TILE
    part = lambda j: pl.BlockSpec((HG_SEQ_TILE, width), lambda bi, si: (bi * n_seq + si, j))
    pad = pltpu.VMEM((HG_HEADS, HG_SUB + CHUNK, HG_DK), F32)
    return pl.pallas_call(
        _hgrn_kernel,
        grid=(b, n_seq),
        in_specs=[part(0), part(1), part(2), part(3), lb.spec, g_out.spec],
        out_specs=pl.BlockSpec((HG_SEQ_TILE, width), lambda bi, si: (bi * n_seq + si, 0)),
        out_shape=jax.ShapeDtypeStruct((b * s, width), BF16),
        scratch_shapes=[pltpu.VMEM((HG_HEADS, HG_DV, HG_DK), F32),
                        pltpu.VMEM((HG_HEADS, HG_SEQ_TILE, HG_DK), F32),
                        pltpu.VMEM((HG_HEADS, HG_SEQ_TILE, HG_DK), F32),
                        pad, pad, pad],
        compiler_params=_params("parallel", "arbitrary"),
        name="hgrn2",
    )(zh, zh, zh, zh, lb.arr, g_out.arr)


def _rope_tables(positions, dim, group):
    half = dim // 2
    inv_freq = ROPE_THETA ** (-jnp.arange(0, dim, 2, dtype=F32) / dim)
    ang = positions.astype(F32).reshape(-1, 1) * inv_freq
    cos, sin = jnp.cos(ang), jnp.sin(ang)
    t = ang.shape[0]
    rest = group - dim
    c = jnp.concatenate([cos, cos, jnp.ones((t, rest), F32)], axis=1)
    sa = jnp.concatenate([-sin, jnp.zeros((t, half + rest), F32)], axis=1)
    sb = jnp.concatenate([jnp.zeros((t, half), F32), sin, jnp.zeros((t, rest), F32)], axis=1)
    rep = LANES // group
    return tuple(jnp.tile(a, (1, rep)) for a in (c, sa, sb))


def kernel(x, positions, norm_g, ffn_w_gate, ffn_w_up, ffn_w_down, ev_w_in, ev_g_q, ev_w_uq, ev_g_kv, ev_w_ukv, ev_lb_logits, ev_g_out, ev_w_out, od_w_in, od_lambda, od_g_head, od_w_out):
    b, s, d = x.shape
    t = b * s
    xt = x.reshape(t, d)
    tabs_m = _rope_tables(positions, MLA_ROPE, LANES)
    tabs_d = _rope_tables(positions, DF_ROT, DF_DH)
    lb_w = jax.nn.softmax(ev_lb_logits.astype(F32), axis=0)
    lb_all = jnp.cumsum(lb_w, axis=0) - lb_w[0:1]

    wg = ffn_w_gate.astype(BF16)
    wu = ffn_w_up.astype(BF16)
    wd = ffn_w_down.astype(BF16)
    n_even = ev_w_in.shape[0]
    o_pe = MLA_Q_RANK + MLA_KV_RANK + MLA_ROPE
    w_in_e = jnp.concatenate(
        [ev_w_in[..., :o_pe].astype(BF16), jnp.zeros((n_even, d, LANES - MLA_ROPE), BF16),
         ev_w_in[..., o_pe:].astype(BF16)], axis=-1)
    w_uq = ev_w_uq.astype(BF16).reshape(n_even, MLA_Q_RANK, MLA_HEADS, MLA_NOPE + MLA_ROPE)
    w_uq = jnp.pad(w_uq, ((0, 0), (0, 0), (0, 0), (0, MLA_QK_PAD - MLA_NOPE - MLA_ROPE)))
    w_uq = w_uq.reshape(n_even, MLA_Q_RANK, MLA_HEADS * MLA_QK_PAD)
    w_ukv = ev_w_ukv.astype(BF16).reshape(n_even, MLA_KV_RANK, MLA_HEADS, 2, MLA_NOPE)
    w_ukv = w_ukv.transpose(0, 1, 3, 2, 4).reshape(n_even, MLA_KV_RANK, 2 * MLA_HEADS * MLA_NOPE)
    w_out_e = ev_w_out.astype(BF16)
    w_in_o = od_w_in.astype(BF16)
    w_out_o = od_w_out.astype(BF16)
    g_q = ev_g_q.reshape(n_even, 1, -1)
    g_kv = ev_g_kv.reshape(n_even, 1, -1)
    g_out = ev_g_out.reshape(n_even, 1, -1)
    lb_all = lb_all.reshape(n_even, 1, -1)

    for l in range(DEPTH):
        xt = _ffn_call(xt, _Pick(norm_g, l, 0), _Pick(wg, l, 0), _Pick(wu, l, 0), _Pick(wd, l, 0))
        j = l // 2
        g_mix = _Pick(norm_g, l, 1)
        if l % 2 == 0:
            q, k, v, zh = _even_pre_call(
                xt, g_mix, _Pick(w_in_e, j), _Pick(g_q, j), _Pick(w_uq, j), _Pick(g_kv, j),
                _Pick(w_ukv, j), tabs_m)
            o_a = _mla_attn_call(q.reshape(b, s, -1), k.reshape(b, s, -1), v.reshape(b, s, -1))
            o_b = _hgrn_call(zh, _Pick(lb_all, j), _Pick(g_out, j), b, s)
            parts = [o_a.reshape(t, -1), o_b]
            w_out = _Pick(w_out_e, j)
        else:
            lambda_init = 0.8 - 0.6 * math.exp(-0.3 * l)
            q, k, v = _odd_pre_call(xt, g_mix, _Pick(w_in_o, j), tabs_d)
            o = _diff_attn_call(q.reshape(b, s, -1), k.reshape(b, s, -1), v.reshape(b, s, -1),
                                od_lambda, od_g_head, j, lambda_init)
            parts = [o.reshape(t, -1)]
            w_out = _Pick(w_out_o, j)
        xt = _mix_ffn_call(xt, parts, w_out, g_mix, _Pick(norm_g, l, 2),
                           _Pick(wg, l, 1), _Pick(wu, l, 1), _Pick(wd, l, 1))
    return xt.reshape(b, s, d)
```

```python
import functools
import math

import jax
import jax.numpy as jnp
from jax import lax
from jax.experimental import pallas as pl
from jax.experimental.pallas import tpu as pltpu

D_MODEL = 1024
DEPTH = 4
CHUNK = 64
ROPE_THETA = 500000.0
EPS = 1e-6
NEG_INF = -1e30
TINY = 1e-30
D_FF = 2816
MLA_HEADS = 4
MLA_NOPE = 128
MLA_ROPE = 64
MLA_V = 128
MLA_Q_RANK = 384
MLA_KV_RANK = 256
HG_HEADS = 4
HG_DK = 128
HG_DV = 128
DF_HEADS = 8
DF_DH = 64
DF_ROT = DF_DH // 4

LANES = 128
MXU_DIM = 256
VMEM_LIMIT = 56 * 1024 * 1024

TOKEN_TILE = 512
MLA_Q_TILE = 512
DF_Q_TILE = 256
KEY_BLOCK = 1024
ATTN_HEADS_PER_STEP = 2
HG_SEQ_TILE = 1024
HG_SUB = 16
HG_UNROLL = 4
HG_SAFE_LOG = 75.0
FF_CHUNKS = ((0, 1024), (1024, 2048), (2048, D_FF))

MLA_QK_PAD = 2 * LANES
V_ROWS = LANES + 16
EVEN_Z = MLA_Q_RANK + MLA_KV_RANK + LANES + 4 * HG_HEADS * HG_DK

F32 = jnp.float32
BF16 = jnp.bfloat16


def _rms(x, g):
    ms = jnp.mean(x * x, axis=-1, keepdims=True)
    return x * lax.rsqrt(ms + EPS) * g


def _dot(a, b):
    return jnp.dot(a, b, preferred_element_type=F32)


def _dot_nt(a, b):
    return lax.dot_general(a, b, (((1,), (1,)), ((), ())), preferred_element_type=F32)


def _dot_tn(a, b):
    return lax.dot_general(a, b, (((0,), (0,)), ((), ())), preferred_element_type=F32)


def _rope_slab(x, c, sa, sb, half):
    return x * c + pltpu.roll(x, LANES - half, 1) * sa + pltpu.roll(x, half, 1) * sb


def _store_values(vt_ref, v, heads):
    ones = jnp.ones((V_ROWS - LANES, v.shape[0]), BF16)
    for hd in range(heads):
        vt_ref[hd, :LANES, :] = v[:, hd * LANES:(hd + 1) * LANES].T.astype(BF16)
        vt_ref[hd, LANES:, :] = ones


def _vt_out(heads, t):
    per_blk = KEY_BLOCK // TOKEN_TILE
    spec = pl.BlockSpec((heads, None, V_ROWS, TOKEN_TILE), lambda i: (0, i // per_blk, 0, i % per_blk))
    return spec, jax.ShapeDtypeStruct((heads, t // KEY_BLOCK, V_ROWS, KEY_BLOCK), BF16)


class _Pick:
    def __init__(self, arr, *idx):
        self.arr = arr
        nd = arr.ndim - len(idx)
        self.spec = pl.BlockSpec((None,) * len(idx) + arr.shape[len(idx):],
                                 lambda *_: idx + (0,) * nd, pipeline_mode=pl.Buffered(1))


def _params(*sem):
    return pltpu.CompilerParams(dimension_semantics=sem, vmem_limit_bytes=VMEM_LIMIT)


def _ffn_apply(x, gpre, gpost, wg_ref, wu_ref, wd_ref):
    xn = _rms(x, gpre).astype(BF16)
    acc = None
    for lo, hi in FF_CHUNKS:
        g = _dot(xn, wg_ref[:, lo:hi])
        u = _dot(xn, wu_ref[:, lo:hi])
        a = (jax.nn.silu(g) * u).astype(BF16)
        h = _dot(a, wd_ref[lo:hi, :])
        acc = h if acc is None else acc + h
    return x + 0.5 * _rms(acc, gpost)


def _ffn_kernel(x_ref, g_ref, wg_ref, wu_ref, wd_ref, o_ref):
    g = g_ref[...]
    o_ref[...] = _ffn_apply(x_ref[...], g[0:1], g[1:2], wg_ref, wu_ref, wd_ref)


def _ffn_call(x, g2, wg, wu, wd):
    t = x.shape[0]
    tile = pl.BlockSpec((TOKEN_TILE, D_MODEL), lambda i: (i, 0))
    picks = [g2, wg, wu, wd]
    return pl.pallas_call(
        _ffn_kernel,
        grid=(t // TOKEN_TILE,),
        in_specs=[tile] + [p.spec for p in picks],
        out_specs=tile,
        out_shape=jax.ShapeDtypeStruct(x.shape, F32),
        compiler_params=_params("parallel"),
        name="ffn",
    )(x, *[p.arr for p in picks])


def _mix_ffn_kernel(n_parts, *refs):
    x_ref = refs[0]
    part_refs = refs[1:1 + n_parts]
    wo_ref, gm_ref, gf_ref, wg_ref, wu_ref, wd_ref, o_ref = refs[1 + n_parts:]
    gf = gf_ref[...]
    m = None
    row = 0
    for p_ref in part_refs:
        w = p_ref.shape[-1]
        d = _dot(p_ref[...], wo_ref[row:row + w, :])
        m = d if m is None else m + d
        row += w
    x1 = x_ref[...] + _rms(m, gm_ref[1:2])
    o_ref[...] = _ffn_apply(x1, gf[0:1], gf[1:2], wg_ref, wu_ref, wd_ref)


def _mix_ffn_call(x, parts, wo, g_mix, g_ffn, wg, wu, wd):
    t = x.shape[0]
    tile = pl.BlockSpec((TOKEN_TILE, D_MODEL), lambda i: (i, 0))
    part_specs = [pl.BlockSpec((TOKEN_TILE, p.shape[-1]), lambda i: (i, 0)) for p in parts]
    picks = [wo, g_mix, g_ffn, wg, wu, wd]
    return pl.pallas_call(
        functools.partial(_mix_ffn_kernel, len(parts)),
        grid=(t // TOKEN_TILE,),
        in_specs=[tile] + part_specs + [p.spec for p in picks],
        out_specs=tile,
        out_shape=jax.ShapeDtypeStruct(x.shape, F32),
        compiler_params=_params("parallel"),
        name="mix_ffn",
    )(x, *parts, *[p.arr for p in picks])


def _even_pre_kernel(x_ref, g_ref, win_ref, gq_ref, wuq_ref, gkv_ref, wukv_ref,
                     c_ref, sa_ref, sb_ref, q_ref, k_ref, v_ref, zh_ref):
    h = _rms(x_ref[...], g_ref[0:1]).astype(BF16)
    z = _dot(h, win_ref[...])
    o_kv = MLA_Q_RANK
    o_pe = o_kv + MLA_KV_RANK
    o_h = o_pe + LANES
    zh_ref[...] = z[:, o_h:]
    c, sa, sb = c_ref[...], sa_ref[...], sb_ref[...]
    half = MLA_ROPE // 2
    scale = (MLA_NOPE + MLA_ROPE) ** -0.5

    cq = _rms(z[:, :o_kv], gq_ref[...]).astype(BF16)
    q = _dot(cq, wuq_ref[...]) * scale
    for hd in range(MLA_HEADS):
        base = hd * MLA_QK_PAD
        q_ref[:, base:base + LANES] = q[:, base:base + LANES].astype(BF16)
        q_ref[:, base + LANES:base + 2 * LANES] = _rope_slab(
            q[:, base + LANES:base + 2 * LANES], c, sa, sb, half).astype(BF16)

    ckv = _rms(z[:, o_kv:o_pe], gkv_ref[...]).astype(BF16)
    kv = _dot(ckv, wukv_ref[...])
    kpe = _rope_slab(z[:, o_pe:o_h], c, sa, sb, half).astype(BF16)
    for hd in range(MLA_HEADS):
        base = hd * MLA_QK_PAD
        k_ref[:, base:base + LANES] = kv[:, hd * LANES:(hd + 1) * LANES].astype(BF16)
        k_ref[:, base + LANES:base + 2 * LANES] = kpe
    _store_values(v_ref, kv[:, MLA_HEADS * MLA_NOPE:], MLA_HEADS)


def _even_pre_call(x, g, win, gq, wuq, gkv, wukv, tabs):
    t = x.shape[0]
    picks = [g, win, gq, wuq, gkv, wukv]
    row = lambda w: pl.BlockSpec((TOKEN_TILE, w), lambda i: (i, 0))
    qk_w = MLA_HEADS * MLA_QK_PAD
    vt_spec, vt_shape = _vt_out(MLA_HEADS, t)
    zh_w = 4 * HG_HEADS * HG_DK
    return pl.pallas_call(
        _even_pre_kernel,
        grid=(t // TOKEN_TILE,),
        in_specs=[row(D_MODEL)] + [p.spec for p in picks] + [row(LANES), row(LANES), row(LANES)],
        out_specs=[row(qk_w), row(qk_w), vt_spec, row(zh_w)],
        out_shape=[jax.ShapeDtypeStruct((t, qk_w), BF16), jax.ShapeDtypeStruct((t, qk_w), BF16),
                   vt_shape, jax.ShapeDtypeStruct((t, zh_w), F32)],
        compiler_params=_params("parallel"),
        name="even_pre",
    )(x, *[p.arr for p in picks], *tabs)


def _odd_pre_kernel(x_ref, g_ref, win_ref, c_ref, sa_ref, sb_ref, q_ref, k_ref, v_ref):
    h = _rms(x_ref[...], g_ref[0:1]).astype(BF16)
    c, sa, sb = c_ref[...], sa_ref[...], sb_ref[...]
    half = DF_ROT // 2
    width = DF_HEADS * 2 * DF_DH
    scale = DF_DH ** -0.5
    q = _dot(h, win_ref[:, :width]) * scale
    k = _dot(h, win_ref[:, width:2 * width])
    for j in range(width // LANES):
        sl = slice(j * LANES, (j + 1) * LANES)
        q_ref[:, sl] = _rope_slab(q[:, sl], c, sa, sb, half).astype(BF16)
        k_ref[:, sl] = _rope_slab(k[:, sl], c, sa, sb, half).astype(BF16)
    _store_values(v_ref, _dot(h, win_ref[:, 2 * width:]), DF_HEADS)


def _odd_pre_call(x, g, win, tabs):
    t = x.shape[0]
    width = DF_HEADS * 2 * DF_DH
    row = lambda w: pl.BlockSpec((TOKEN_TILE, w), lambda i: (i, 0))
    vt_spec, vt_shape = _vt_out(DF_HEADS, t)
    return pl.pallas_call(
        _odd_pre_kernel,
        grid=(t // TOKEN_TILE,),
        in_specs=[row(D_MODEL), g.spec, win.spec, row(LANES), row(LANES), row(LANES)],
        out_specs=[row(width), row(width), vt_spec],
        out_shape=[jax.ShapeDtypeStruct((t, width), BF16), jax.ShapeDtypeStruct((t, width), BF16), vt_shape],
        compiler_params=_params("parallel"),
        name="odd_pre",
    )(x, g.arr, win.arr, *tabs)


def _flash(streams, q_chunk, k_ref, vt_ref, row0, tq):
    m_rows = streams[0][0].shape[0]

    def step(blk, width, carries, masked):
        start = pl.multiple_of(blk * KEY_BLOCK, KEY_BLOCK)
        scores = [_dot_nt(k_ref[0, pl.ds(start, width), ksl], q) for q, ksl, _ in streams]
        probs = []
        for s, (m_i, _) in zip(scores, carries):
            if masked:
                k_chunk = (start + (width - tq) + lax.broadcasted_iota(jnp.int32, (tq, 1), 0)) // CHUNK
                tail = jnp.where(k_chunk <= q_chunk, s[width - tq:], NEG_INF)
                s = tail if width == tq else jnp.concatenate([s[:width - tq], tail], axis=0)
            m_new = jnp.maximum(m_i, jnp.max(s, axis=0, keepdims=True))
            probs.append((m_new, jnp.exp(m_i - m_new), jnp.exp(s - m_new).astype(BF16)))
        outs = []
        for (m_new, alpha, p), (_, _, hd), (_, acc) in zip(probs, streams, carries):
            outs.append((m_new, alpha * acc + _dot(vt_ref[hd, blk, :, :width], p)))
        return tuple(outs)

    init = tuple((jnp.full((1, m_rows), NEG_INF, F32), jnp.zeros((V_ROWS, m_rows), F32)) for _ in streams)
    n_full = row0 // KEY_BLOCK
    carries = lax.fori_loop(0, n_full, lambda blk, c: step(blk, KEY_BLOCK, c, False), init)
    diag = [functools.partial(step, n_full, tq * (p + 1), masked=True) for p in range(KEY_BLOCK // tq)]
    carries = lax.switch((row0 % KEY_BLOCK) // tq, diag, carries)
    return [acc[:LANES] / acc[LANES:LANES + 1] for _, acc in carries]


def _row_chunks(row0, n_rows):
    rows = row0 + lax.broadcasted_iota(jnp.int32, (1, n_rows), 1)
    return rows // CHUNK


def _vt_spec(hp, s):
    return pl.BlockSpec((hp, s // KEY_BLOCK, V_ROWS, KEY_BLOCK), lambda bi, h, i: (h, bi, 0, 0))


def _mla_attn_kernel(q_ref, k_ref, vt_ref, o_ref):
    row0 = pl.program_id(2) * MLA_Q_TILE
    streams = [(q_ref[0, :, h * MLA_QK_PAD:(h + 1) * MLA_QK_PAD],
                slice(h * MLA_QK_PAD, (h + 1) * MLA_QK_PAD), h)
               for h in range(ATTN_HEADS_PER_STEP)]
    outs = _flash(streams, _row_chunks(row0, MLA_Q_TILE), k_ref, vt_ref, row0, MLA_Q_TILE)
    for h, o_t in enumerate(outs):
        o_ref[0, :, h * MLA_V:(h + 1) * MLA_V] = o_t.T.astype(BF16)


def _mla_attn_call(q, k, vt):
    b, s, _ = q.shape
    hp = ATTN_HEADS_PER_STEP
    return pl.pallas_call(
        _mla_attn_kernel,
        grid=(b, MLA_HEADS // hp, s // MLA_Q_TILE),
        in_specs=[pl.BlockSpec((1, MLA_Q_TILE, hp * MLA_QK_PAD), lambda bi, h, i: (bi, i, h)),
                  pl.BlockSpec((1, s, hp * MLA_QK_PAD), lambda bi, h, i: (bi, 0, h)),
                  _vt_spec(hp, s)],
        out_specs=pl.BlockSpec((1, MLA_Q_TILE, hp * MLA_V), lambda bi, h, i: (bi, i, h)),
        out_shape=jax.ShapeDtypeStruct((b, s, MLA_HEADS * MLA_V), BF16),
        compiler_params=_params("parallel", "parallel", "arbitrary"),
        name="mla_attn",
    )(q, k, vt)


def _diff_attn_kernel(lambda_init, q_ref, k_ref, vt_ref, lam_ref, gh_ref, o_ref):
    row0 = pl.program_id(2) * DF_Q_TILE
    dv = 2 * DF_DH
    lane = lax.broadcasted_iota(jnp.int32, (1, dv), 1)
    streams = []
    for h in range(ATTN_HEADS_PER_STEP):
        hs = slice(h * dv, (h + 1) * dv)
        q = q_ref[0, :, hs]
        zero = jnp.zeros_like(q)
        q2 = jnp.concatenate([jnp.where(lane < DF_DH, q, zero), jnp.where(lane >= DF_DH, q, zero)], axis=0)
        streams.append((q2, hs, h))
    chunks = _row_chunks(row0, DF_Q_TILE)
    outs = _flash(streams, jnp.concatenate([chunks, chunks], axis=1), k_ref, vt_ref, row0, DF_Q_TILE)
    lp = lam_ref[...]
    lam = (jnp.exp(jnp.sum(lp[0:1] * lp[1:2], axis=-1, keepdims=True))
           - jnp.exp(jnp.sum(lp[2:3] * lp[3:4], axis=-1, keepdims=True)) + lambda_init)
    for h, o2_t in enumerate(outs):
        hs = slice(h * dv, (h + 1) * dv)
        o = (o2_t[:, :DF_Q_TILE] - lam * o2_t[:, DF_Q_TILE:]).T
        o_ref[0, :, hs] = (_rms(o, gh_ref[:, hs]) * (1.0 - lambda_init)).astype(BF16)


def _diff_attn_call(q, k, vt, lam_p, g_head, j, lambda_init):
    b, s, _ = q.shape
    hp = ATTN_HEADS_PER_STEP
    w = hp * 2 * DF_DH
    g_head = g_head.reshape(g_head.shape[0], DF_HEADS // hp, 1, w)
    return pl.pallas_call(
        functools.partial(_diff_attn_kernel, lambda_init),
        grid=(b, DF_HEADS // hp, s // DF_Q_TILE),
        in_specs=[pl.BlockSpec((1, DF_Q_TILE, w), lambda bi, h, i: (bi, i, h)),
                  pl.BlockSpec((1, s, w), lambda bi, h, i: (bi, 0, h)),
                  _vt_spec(hp, s),
                  _Pick(lam_p, j).spec,
                  pl.BlockSpec((None, None, 1, w), lambda bi, h, i: (j, h, 0, 0))],
        out_specs=pl.BlockSpec((1, DF_Q_TILE, w), lambda bi, h, i: (bi, i, h)),
        out_shape=jax.ShapeDtypeStruct((b, s, DF_HEADS * 2 * DF_DH), BF16),
        compiler_params=_params("parallel", "parallel", "arbitrary"),
        name="diff_attn",
    )(q, k, vt, lam_p, g_head)


def _hgrn_kernel(hq_ref, hf_ref, hi_ref, hg_ref, lb_ref, go_ref, o_ref,
                 st_ref, b_ref, kk_ref, bpad_ref, kpad_ref, vpad_ref):
    n_chunks = HG_SEQ_TILE // CHUNK
    mid = CHUNK // 2 - 1

    @pl.when(pl.program_id(1) == 0)
    def _():
        st_ref[...] = jnp.zeros_like(st_ref)

    row_in_chunk = lax.broadcasted_iota(jnp.int32, (HG_SEQ_TILE, 1), 0) % CHUNK
    for hd in range(HG_HEADS):
        ls = slice(hd * HG_DK, (hd + 1) * HG_DK)
        lb = lb_ref[:, ls]
        sig = jax.nn.sigmoid(hf_ref[:, ls])
        kk_ref[hd] = (1.0 - lb) * (1.0 - sig)
        b = jnp.log(jnp.maximum(lb + (1.0 - lb) * sig, TINY))
        step = 1
        while step < CHUNK:
            b = b + jnp.where(row_in_chunk >= step, pltpu.roll(b, step, 0), 0.0)
            step *= 2
        b_ref[hd] = b

    b_mid = b_ref[:, pl.ds(mid, n_chunks, stride=CHUNK), :]
    b_end = b_ref[:, pl.ds(CHUNK - 1, n_chunks, stride=CHUNK), :]
    worst = jnp.max(jnp.maximum(-b_mid, b_mid - b_end))

    row = lax.broadcasted_iota(jnp.int32, (CHUNK, 1), 0)

    def load(ci, hd):
        rows = pl.ds(pl.multiple_of(ci * CHUNK, CHUNK), CHUNK)
        ls = slice(hd * HG_DK, (hd + 1) * HG_DK)
        return rows, ls, hq_ref[rows, ls], hi_ref[rows, ls], b_ref[hd, rows, :], kk_ref[hd, rows, :]

    def carry_state(hd, qh, vh, b, kk):
        st = st_ref[hd]
        o = _dot_nt((qh * jnp.exp(b)).astype(BF16), st.astype(BF16))
        b_last = b[CHUNK - 1:CHUNK]
        kdec = kk * jnp.exp(b_last - b)
        st_ref[hd] = st * jnp.exp(b_last) + _dot_tn(vh.astype(BF16), kdec.astype(BF16))
        return o

    def finish(rows, ls, o):
        on = _rms(o, go_ref[:, ls]) * jax.nn.silu(hg_ref[rows, ls])
        o_ref[rows, ls] = on.astype(BF16)

    @pl.when(worst < HG_SAFE_LOG)
    def _():
        causal = row >= lax.broadcasted_iota(jnp.int32, (1, CHUNK), 1)

        def chunk_body(cj, carry):
            for u in range(HG_UNROLL):
                ci = cj * HG_UNROLL + u
                for hd in range(HG_HEADS):
                    rows, ls, qh, vh, b, kk = load(ci, hd)
                    o = carry_state(hd, qh, vh, b, kk)
                    b_m = b[mid:mid + 1]
                    qf = (qh * jnp.exp(b - b_m)).astype(BF16)
                    kf = (kk * jnp.exp(b_m - b)).astype(BF16)
                    a = jnp.where(causal, _dot_nt(qf, kf), 0.0)
                    finish(rows, ls, o + _dot(a.astype(BF16), vh.astype(BF16)))
            return carry

        lax.fori_loop(0, n_chunks // HG_UNROLL, chunk_body, 0)

    @pl.when(jnp.logical_not(worst < HG_SAFE_LOG))
    def _():
        n_sub = CHUNK // HG_SUB
        off_w = HG_SUB * (n_sub * (n_sub - 1) // 2)
        zpad = jnp.zeros((HG_HEADS, HG_SUB, HG_DK), F32)
        bpad_ref[:, :HG_SUB, :] = zpad
        kpad_ref[:, :HG_SUB, :] = zpad
        vpad_ref[:, :HG_SUB, :] = zpad
        row_sub = row % HG_SUB
        r2 = lax.broadcasted_iota(jnp.int32, (2 * HG_DK, 2 * HG_DK), 0) // HG_DK
        c2 = lax.broadcasted_iota(jnp.int32, (2 * HG_DK, 2 * HG_DK), 1) // HG_DK
        ones2 = jnp.where(r2 == c2, 1.0, 0.0).astype(BF16)
        col = lax.broadcasted_iota(jnp.int32, (1, off_w), 1)
        col_blk = jnp.zeros((1, off_w), jnp.int32)
        for i in range(1, n_sub):
            col_blk = col_blk + jnp.where(col >= HG_SUB * (i * (i - 1) // 2), 1, 0)
        off_mask = col_blk == (row // HG_SUB)

        def chunk_body(ci, carry):
            for hd in range(HG_HEADS):
                rows, ls, qh, vh, b, kk = load(ci, hd)
                o = carry_state(hd, qh, vh, b, kk)

                refs = [b[i * HG_SUB - 1:i * HG_SUB] for i in range(1, n_sub)]
                bref = jnp.concatenate(
                    [jnp.zeros((HG_SUB, HG_DK), F32)]
                    + [jnp.broadcast_to(r, (HG_SUB, HG_DK)) for r in refs], axis=0)
                qs = qh * jnp.exp(b - bref)
                kst = jnp.concatenate(
                    [kk[:i * HG_SUB] * jnp.exp(refs[i - 1] - b[:i * HG_SUB]) for i in range(1, n_sub)],
                    axis=0)
                vst = jnp.concatenate([vh[:i * HG_SUB] for i in range(1, n_sub)], axis=0)
                a_off = jnp.where(off_mask, _dot_nt(qs.astype(BF16), kst.astype(BF16)), 0.0)
                o = o + _dot(a_off.astype(BF16), vst.astype(BF16))

                bpad_ref[hd, HG_SUB:, :] = b
                kpad_ref[hd, HG_SUB:, :] = kk
                vpad_ref[hd, HG_SUB:, :] = vh
                for dp in range(HG_SUB // 2):
                    terms = []
                    for d in (2 * dp, 2 * dp + 1):
                        lo = HG_SUB - d
                        w = qh * jnp.exp(b - bpad_ref[hd, lo:lo + CHUNK, :]) * kpad_ref[hd, lo:lo + CHUNK, :]
                        terms.append(jnp.where(row_sub >= d, w, 0.0))
                    dsum = _dot(jnp.concatenate(terms, axis=1).astype(BF16), ones2)
                    for j, d in enumerate((2 * dp, 2 * dp + 1)):
                        lo = HG_SUB - d
                        o = o + dsum[:, j * HG_DK:(j + 1) * HG_DK] * vpad_ref[hd, lo:lo + CHUNK, :]
                finish(rows, ls, o)
            return carry

        lax.fori_loop(0, n_chunks, chunk_body, 0)


def _hgrn_call(zh, lb, g_out, b, s):
    width = HG_HEADS * HG_DK
    n_seq = s // HG_SEQ_TILE
    part = lambda j: pl.BlockSpec((HG_SEQ_TILE, width), lambda bi, si: (bi * n_seq + si, j))
    pad = pltpu.VMEM((HG_HEADS, HG_SUB + CHUNK, HG_DK), F32)
    return pl.pallas_call(
        _hgrn_kernel,
        grid=(b, n_seq),
        in_specs=[part(0), part(1), part(2), part(3), lb.spec, g_out.spec],
        out_specs=pl.BlockSpec((HG_SEQ_TILE, width), lambda bi, si: (bi * n_seq + si, 0)),
        out_shape=jax.ShapeDtypeStruct((b * s, width), BF16),
        scratch_shapes=[pltpu.VMEM((HG_HEADS, HG_DV, HG_DK), F32),
                        pltpu.VMEM((HG_HEADS, HG_SEQ_TILE, HG_DK), F32),
                        pltpu.VMEM((HG_HEADS, HG_SEQ_TILE, HG_DK), F32),
                        pad, pad, pad],
        compiler_params=_params("parallel", "arbitrary"),
        name="hgrn2",
    )(zh, zh, zh, zh, lb.arr, g_out.arr)


def _rope_tables(positions, dim, group):
    half = dim // 2
    inv_freq = ROPE_THETA ** (-jnp.arange(0, dim, 2, dtype=F32) / dim)
    ang = positions.astype(F32).reshape(-1, 1) * inv_freq
    cos, sin = jnp.cos(ang), jnp.sin(ang)
    t = ang.shape[0]
    rest = group - dim
    c = jnp.concatenate([cos, cos, jnp.ones((t, rest), F32)], axis=1)
    sa = jnp.concatenate([-sin, jnp.zeros((t, half + rest), F32)], axis=1)
    sb = jnp.concatenate([jnp.zeros((t, half), F32), sin, jnp.zeros((t, rest), F32)], axis=1)
    rep = LANES // group
    return tuple(jnp.tile(a, (1, rep)) for a in (c, sa, sb))


def kernel(x, positions, norm_g, ffn_w_gate, ffn_w_up, ffn_w_down, ev_w_in, ev_g_q, ev_w_uq, ev_g_kv, ev_w_ukv, ev_lb_logits, ev_g_out, ev_w_out, od_w_in, od_lambda, od_g_head, od_w_out):
    b, s, d = x.shape
    t = b * s
    xt = x.reshape(t, d)
    tabs_m = _rope_tables(positions, MLA_ROPE, LANES)
    tabs_d = _rope_tables(positions, DF_ROT, DF_DH)
    lb_w = jax.nn.softmax(ev_lb_logits.astype(F32), axis=0)
    lb_all = jnp.cumsum(lb_w, axis=0) - lb_w[0:1]

    wg = ffn_w_gate.astype(BF16)
    wu = ffn_w_up.astype(BF16)
    wd = ffn_w_down.astype(BF16)
    n_even = ev_w_in.shape[0]
    o_pe = MLA_Q_RANK + MLA_KV_RANK + MLA_ROPE
    w_in_e = jnp.concatenate(
        [ev_w_in[..., :o_pe].astype(BF16), jnp.zeros((n_even, d, LANES - MLA_ROPE), BF16),
         ev_w_in[..., o_pe:].astype(BF16)], axis=-1)
    w_uq = ev_w_uq.astype(BF16).reshape(n_even, MLA_Q_RANK, MLA_HEADS, MLA_NOPE + MLA_ROPE)
    w_uq = jnp.pad(w_uq, ((0, 0), (0, 0), (0, 0), (0, MLA_QK_PAD - MLA_NOPE - MLA_ROPE)))
    w_uq = w_uq.reshape(n_even, MLA_Q_RANK, MLA_HEADS * MLA_QK_PAD)
    w_ukv = ev_w_ukv.astype(BF16).reshape(n_even, MLA_KV_RANK, MLA_HEADS, 2, MLA_NOPE)
    w_ukv = w_ukv.transpose(0, 1, 3, 2, 4).reshape(n_even, MLA_KV_RANK, 2 * MLA_HEADS * MLA_NOPE)
    w_out_e = ev_w_out.astype(BF16)
    w_in_o = od_w_in.astype(BF16)
    w_out_o = od_w_out.astype(BF16)
    g_q = ev_g_q.reshape(n_even, 1, -1)
    g_kv = ev_g_kv.reshape(n_even, 1, -1)
    g_out = ev_g_out.reshape(n_even, 1, -1)
    lb_all = lb_all.reshape(n_even, 1, -1)

    for l in range(DEPTH):
        xt = _ffn_call(xt, _Pick(norm_g, l, 0), _Pick(wg, l, 0), _Pick(wu, l, 0), _Pick(wd, l, 0))
        j = l // 2
        g_mix = _Pick(norm_g, l, 1)
        if l % 2 == 0:
            q, k, v, zh = _even_pre_call(
                xt, g_mix, _Pick(w_in_e, j), _Pick(g_q, j), _Pick(w_uq, j), _Pick(g_kv, j),
                _Pick(w_ukv, j), tabs_m)
            o_a = _mla_attn_call(q.reshape(b, s, -1), k.reshape(b, s, -1), v)
            o_b = _hgrn_call(zh, _Pick(lb_all, j), _Pick(g_out, j), b, s)
            parts = [o_a.reshape(t, -1), o_b]
            w_out = _Pick(w_out_e, j)
        else:
            lambda_init = 0.8 - 0.6 * math.exp(-0.3 * l)
            q, k, v = _odd_pre_call(xt, g_mix, _Pick(w_in_o, j), tabs_d)
            o = _diff_attn_call(q.reshape(b, s, -1), k.reshape(b, s, -1), v, od_lambda, od_g_head, j,
                                lambda_init)
            parts = [o.reshape(t, -1)]
            w_out = _Pick(w_out_o, j)
        xt = _mix_ffn_call(xt, parts, w_out, g_mix, _Pick(norm_g, l, 2),
                           _Pick(wg, l, 1), _Pick(wu, l, 1), _Pick(wd, l, 1))
    return xt.reshape(b, s, d)
```

```python
import functools
import math

import jax
import jax.numpy as jnp
from jax import lax
from jax.experimental import pallas as pl
from jax.experimental.pallas import tpu as pltpu

D_MODEL = 1024
DEPTH = 4
CHUNK = 64
ROPE_THETA = 500000.0
EPS = 1e-6
NEG_INF = -1e30
LOG2_E = math.log2(math.e)
TINY = 1e-30
D_FF = 2816
MLA_HEADS = 4
MLA_NOPE = 128
MLA_ROPE = 64
MLA_V = 128
MLA_Q_RANK = 384
MLA_KV_RANK = 256
HG_HEADS = 4
HG_DK = 128
HG_DV = 128
DF_HEADS = 8
DF_DH = 64
DF_ROT = DF_DH // 4

LANES = 128
MXU_DIM = 256
VMEM_LIMIT = 56 * 1024 * 1024

TOKEN_TILE = 512
MLA_Q_TILE = 512
DF_Q_TILE = 256
KEY_BLOCK = 1024
ATTN_HEADS_PER_STEP = 2
HG_SEQ_TILE = 1024
HG_SUB = 16
HG_GATE_ROWS = 128
HG_UNROLL = 4
HG_SAFE_LOG = 75.0
FF_CHUNKS = ((0, 1024), (1024, 2048), (2048, D_FF))

MLA_QK_PAD = 2 * LANES
V_ROWS = LANES + 16
EVEN_Z = MLA_Q_RANK + MLA_KV_RANK + LANES + 4 * HG_HEADS * HG_DK

F32 = jnp.float32
BF16 = jnp.bfloat16


def _rms(x, g):
    ms = jnp.mean(x * x, axis=-1, keepdims=True)
    return x * lax.rsqrt(ms + EPS) * g


def _dot(a, b):
    return jnp.dot(a, b, preferred_element_type=F32)


def _dot_nt(a, b):
    return lax.dot_general(a, b, (((1,), (1,)), ((), ())), preferred_element_type=F32)


def _dot_tn(a, b):
    return lax.dot_general(a, b, (((0,), (0,)), ((), ())), preferred_element_type=F32)


def _rope_slab(x, c, sa, sb, half):
    return x * c + pltpu.roll(x, LANES - half, 1) * sa + pltpu.roll(x, half, 1) * sb


def _store_values(vt_ref, v, heads):
    ones = jnp.ones((V_ROWS - LANES, v.shape[0]), BF16)
    for hd in range(heads):
        vt_ref[hd, :LANES, :] = v[:, hd * LANES:(hd + 1) * LANES].T.astype(BF16)
        vt_ref[hd, LANES:, :] = ones


def _vt_out(heads, t):
    per_blk = KEY_BLOCK // TOKEN_TILE
    spec = pl.BlockSpec((heads, None, V_ROWS, TOKEN_TILE), lambda i: (0, i // per_blk, 0, i % per_blk))
    return spec, jax.ShapeDtypeStruct((heads, t // KEY_BLOCK, V_ROWS, KEY_BLOCK), BF16)


class _Pick:
    def __init__(self, arr, *idx):
        self.arr = arr
        nd = arr.ndim - len(idx)
        self.spec = pl.BlockSpec((None,) * len(idx) + arr.shape[len(idx):],
                                 lambda *_: idx + (0,) * nd, pipeline_mode=pl.Buffered(1))


def _params(*sem):
    return pltpu.CompilerParams(dimension_semantics=sem, vmem_limit_bytes=VMEM_LIMIT)


def _ffn_apply(x, gpre, gpost, wg_ref, wu_ref, wd_ref):
    xn = _rms(x, gpre).astype(BF16)
    acc = None
    for lo, hi in FF_CHUNKS:
        g = _dot(xn, wg_ref[:, lo:hi])
        u = _dot(xn, wu_ref[:, lo:hi])
        a = (jax.nn.silu(g) * u).astype(BF16)
        h = _dot(a, wd_ref[lo:hi, :])
        acc = h if acc is None else acc + h
    return x + 0.5 * _rms(acc, gpost)


def _ffn_kernel(x_ref, g_ref, wg_ref, wu_ref, wd_ref, o_ref):
    g = g_ref[...]
    o_ref[...] = _ffn_apply(x_ref[...], g[0:1], g[1:2], wg_ref, wu_ref, wd_ref)


def _ffn_call(x, g2, wg, wu, wd):
    t = x.shape[0]
    tile = pl.BlockSpec((TOKEN_TILE, D_MODEL), lambda i: (i, 0))
    picks = [g2, wg, wu, wd]
    return pl.pallas_call(
        _ffn_kernel,
        grid=(t // TOKEN_TILE,),
        in_specs=[tile] + [p.spec for p in picks],
        out_specs=tile,
        out_shape=jax.ShapeDtypeStruct(x.shape, F32),
        compiler_params=_params("parallel"),
        name="ffn",
    )(x, *[p.arr for p in picks])


def _mix_ffn_kernel(n_parts, *refs):
    x_ref = refs[0]
    part_refs = refs[1:1 + n_parts]
    wo_ref, gm_ref, gf_ref, wg_ref, wu_ref, wd_ref, o_ref = refs[1 + n_parts:]
    gf = gf_ref[...]
    m = None
    row = 0
    for p_ref in part_refs:
        w = p_ref.shape[-1]
        d = _dot(p_ref[...], wo_ref[row:row + w, :])
        m = d if m is None else m + d
        row += w
    x1 = x_ref[...] + _rms(m, gm_ref[1:2])
    o_ref[...] = _ffn_apply(x1, gf[0:1], gf[1:2], wg_ref, wu_ref, wd_ref)


def _mix_ffn_call(x, parts, wo, g_mix, g_ffn, wg, wu, wd):
    t = x.shape[0]
    tile = pl.BlockSpec((TOKEN_TILE, D_MODEL), lambda i: (i, 0))
    part_specs = [pl.BlockSpec((TOKEN_TILE, p.shape[-1]), lambda i: (i, 0)) for p in parts]
    picks = [wo, g_mix, g_ffn, wg, wu, wd]
    return pl.pallas_call(
        functools.partial(_mix_ffn_kernel, len(parts)),
        grid=(t // TOKEN_TILE,),
        in_specs=[tile] + part_specs + [p.spec for p in picks],
        out_specs=tile,
        out_shape=jax.ShapeDtypeStruct(x.shape, F32),
        compiler_params=_params("parallel"),
        name="mix_ffn",
    )(x, *parts, *[p.arr for p in picks])


def _even_pre_kernel(x_ref, g_ref, win_ref, gq_ref, wuq_ref, gkv_ref, wukv_ref,
                     c_ref, sa_ref, sb_ref, q_ref, k_ref, v_ref, zh_ref):
    h = _rms(x_ref[...], g_ref[0:1]).astype(BF16)
    z = _dot(h, win_ref[...])
    o_kv = MLA_Q_RANK
    o_pe = o_kv + MLA_KV_RANK
    o_h = o_pe + LANES
    zh_ref[...] = z[:, o_h:]
    c, sa, sb = c_ref[...], sa_ref[...], sb_ref[...]
    half = MLA_ROPE // 2
    scale = LOG2_E * (MLA_NOPE + MLA_ROPE) ** -0.5

    cq = _rms(z[:, :o_kv], gq_ref[...]).astype(BF16)
    q = _dot(cq, wuq_ref[...]) * scale
    for hd in range(MLA_HEADS):
        base = hd * MLA_QK_PAD
        q_ref[:, base:base + LANES] = q[:, base:base + LANES].astype(BF16)
        q_ref[:, base + LANES:base + 2 * LANES] = _rope_slab(
            q[:, base + LANES:base + 2 * LANES], c, sa, sb, half).astype(BF16)

    ckv = _rms(z[:, o_kv:o_pe], gkv_ref[...]).astype(BF16)
    kv = _dot(ckv, wukv_ref[...])
    kpe = _rope_slab(z[:, o_pe:o_h], c, sa, sb, half).astype(BF16)
    for hd in range(MLA_HEADS):
        base = hd * MLA_QK_PAD
        k_ref[:, base:base + LANES] = kv[:, hd * LANES:(hd + 1) * LANES].astype(BF16)
        k_ref[:, base + LANES:base + 2 * LANES] = kpe
    _store_values(v_ref, kv[:, MLA_HEADS * MLA_NOPE:], MLA_HEADS)


def _even_pre_call(x, g, win, gq, wuq, gkv, wukv, tabs):
    t = x.shape[0]
    picks = [g, win, gq, wuq, gkv, wukv]
    row = lambda w: pl.BlockSpec((TOKEN_TILE, w), lambda i: (i, 0))
    qk_w = MLA_HEADS * MLA_QK_PAD
    vt_spec, vt_shape = _vt_out(MLA_HEADS, t)
    zh_w = 4 * HG_HEADS * HG_DK
    return pl.pallas_call(
        _even_pre_kernel,
        grid=(t // TOKEN_TILE,),
        in_specs=[row(D_MODEL)] + [p.spec for p in picks] + [row(LANES), row(LANES), row(LANES)],
        out_specs=[row(qk_w), row(qk_w), vt_spec, row(zh_w)],
        out_shape=[jax.ShapeDtypeStruct((t, qk_w), BF16), jax.ShapeDtypeStruct((t, qk_w), BF16),
                   vt_shape, jax.ShapeDtypeStruct((t, zh_w), F32)],
        compiler_params=_params("parallel"),
        name="even_pre",
    )(x, *[p.arr for p in picks], *tabs)


def _odd_pre_kernel(x_ref, g_ref, win_ref, c_ref, sa_ref, sb_ref, q_ref, k_ref, v_ref):
    h = _rms(x_ref[...], g_ref[0:1]).astype(BF16)
    c, sa, sb = c_ref[...], sa_ref[...], sb_ref[...]
    half = DF_ROT // 2
    width = DF_HEADS * 2 * DF_DH
    scale = LOG2_E * DF_DH ** -0.5
    q = _dot(h, win_ref[:, :width]) * scale
    k = _dot(h, win_ref[:, width:2 * width])
    for j in range(width // LANES):
        sl = slice(j * LANES, (j + 1) * LANES)
        q_ref[:, sl] = _rope_slab(q[:, sl], c, sa, sb, half).astype(BF16)
        k_ref[:, sl] = _rope_slab(k[:, sl], c, sa, sb, half).astype(BF16)
    _store_values(v_ref, _dot(h, win_ref[:, 2 * width:]), DF_HEADS)


def _odd_pre_call(x, g, win, tabs):
    t = x.shape[0]
    width = DF_HEADS * 2 * DF_DH
    row = lambda w: pl.BlockSpec((TOKEN_TILE, w), lambda i: (i, 0))
    vt_spec, vt_shape = _vt_out(DF_HEADS, t)
    return pl.pallas_call(
        _odd_pre_kernel,
        grid=(t // TOKEN_TILE,),
        in_specs=[row(D_MODEL), g.spec, win.spec, row(LANES), row(LANES), row(LANES)],
        out_specs=[row(width), row(width), vt_spec],
        out_shape=[jax.ShapeDtypeStruct((t, width), BF16), jax.ShapeDtypeStruct((t, width), BF16), vt_shape],
        compiler_params=_params("parallel"),
        name="odd_pre",
    )(x, g.arr, win.arr, *tabs)


def _flash(streams, q_chunk, k_ref, vt_ref, row0, tq):
    m_rows = streams[0][0].shape[0]

    def step(blk, width, carries, masked):
        start = pl.multiple_of(blk * KEY_BLOCK, KEY_BLOCK)
        scores = [_dot_nt(k_ref[0, pl.ds(start, width), ksl], q) for q, ksl, _ in streams]
        probs = []
        for s, (m_i, _) in zip(scores, carries):
            if masked:
                k_chunk = (start + (width - tq) + lax.broadcasted_iota(jnp.int32, (tq, 1), 0)) // CHUNK
                tail = jnp.where(k_chunk <= q_chunk, s[width - tq:], NEG_INF)
                s = tail if width == tq else jnp.concatenate([s[:width - tq], tail], axis=0)
            m_new = jnp.maximum(m_i, jnp.max(s, axis=0, keepdims=True))
            probs.append((m_new, jnp.exp2(m_i - m_new), jnp.exp2((s - m_new).astype(BF16))))
        outs = []
        for (m_new, alpha, p), (_, _, hd), (_, acc) in zip(probs, streams, carries):
            outs.append((m_new, alpha * acc + _dot(vt_ref[hd, blk, :, :width], p)))
        return tuple(outs)

    init = tuple((jnp.full((1, m_rows), NEG_INF, F32), jnp.zeros((V_ROWS, m_rows), F32)) for _ in streams)
    n_full = row0 // KEY_BLOCK
    carries = lax.fori_loop(0, n_full, lambda blk, c: step(blk, KEY_BLOCK, c, False), init)
    diag = [functools.partial(step, n_full, tq * (p + 1), masked=True) for p in range(KEY_BLOCK // tq)]
    carries = lax.switch((row0 % KEY_BLOCK) // tq, diag, carries)
    return [acc[:LANES] * (1.0 / acc[LANES:LANES + 1]) for _, acc in carries]


def _row_chunks(row0, n_rows):
    rows = row0 + lax.broadcasted_iota(jnp.int32, (1, n_rows), 1)
    return rows // CHUNK


def _vt_spec(hp, s):
    return pl.BlockSpec((hp, s // KEY_BLOCK, V_ROWS, KEY_BLOCK), lambda bi, h, i: (h, bi, 0, 0))


def _mla_attn_kernel(q_ref, k_ref, vt_ref, o_ref):
    row0 = pl.program_id(2) * MLA_Q_TILE
    streams = [(q_ref[0, :, h * MLA_QK_PAD:(h + 1) * MLA_QK_PAD],
                slice(h * MLA_QK_PAD, (h + 1) * MLA_QK_PAD), h)
               for h in range(ATTN_HEADS_PER_STEP)]
    outs = _flash(streams, _row_chunks(row0, MLA_Q_TILE), k_ref, vt_ref, row0, MLA_Q_TILE)
    for h, o_t in enumerate(outs):
        o_ref[0, :, h * MLA_V:(h + 1) * MLA_V] = o_t.T.astype(BF16)


def _mla_attn_call(q, k, vt):
    b, s, _ = q.shape
    hp = ATTN_HEADS_PER_STEP
    return pl.pallas_call(
        _mla_attn_kernel,
        grid=(b, MLA_HEADS // hp, s // MLA_Q_TILE),
        in_specs=[pl.BlockSpec((1, MLA_Q_TILE, hp * MLA_QK_PAD), lambda bi, h, i: (bi, i, h)),
                  pl.BlockSpec((1, s, hp * MLA_QK_PAD), lambda bi, h, i: (bi, 0, h)),
                  _vt_spec(hp, s)],
        out_specs=pl.BlockSpec((1, MLA_Q_TILE, hp * MLA_V), lambda bi, h, i: (bi, i, h)),
        out_shape=jax.ShapeDtypeStruct((b, s, MLA_HEADS * MLA_V), BF16),
        compiler_params=_params("parallel", "parallel", "arbitrary"),
        name="mla_attn",
    )(q, k, vt)


def _diff_attn_kernel(lambda_init, q_ref, k_ref, vt_ref, lam_ref, gh_ref, o_ref):
    row0 = pl.program_id(2) * DF_Q_TILE
    dv = 2 * DF_DH
    lane = lax.broadcasted_iota(jnp.int32, (1, dv), 1)
    streams = []
    for h in range(ATTN_HEADS_PER_STEP):
        hs = slice(h * dv, (h + 1) * dv)
        q = q_ref[0, :, hs]
        zero = jnp.zeros_like(q)
        q2 = jnp.concatenate([jnp.where(lane < DF_DH, q, zero), jnp.where(lane >= DF_DH, q, zero)], axis=0)
        streams.append((q2, hs, h))
    chunks = _row_chunks(row0, DF_Q_TILE)
    outs = _flash(streams, jnp.concatenate([chunks, chunks], axis=1), k_ref, vt_ref, row0, DF_Q_TILE)
    lp = lam_ref[...]
    lam = (jnp.exp(jnp.sum(lp[0:1] * lp[1:2], axis=-1, keepdims=True))
           - jnp.exp(jnp.sum(lp[2:3] * lp[3:4], axis=-1, keepdims=True)) + lambda_init)
    for h, o2_t in enumerate(outs):
        hs = slice(h * dv, (h + 1) * dv)
        o = (o2_t[:, :DF_Q_TILE] - lam * o2_t[:, DF_Q_TILE:]).T
        o_ref[0, :, hs] = (_rms(o, gh_ref[:, hs]) * (1.0 - lambda_init)).astype(BF16)


def _diff_attn_call(q, k, vt, lam_p, g_head, j, lambda_init):
    b, s, _ = q.shape
    hp = ATTN_HEADS_PER_STEP
    w = hp * 2 * DF_DH
    g_head = g_head.reshape(g_head.shape[0], DF_HEADS // hp, 1, w)
    return pl.pallas_call(
        functools.partial(_diff_attn_kernel, lambda_init),
        grid=(b, DF_HEADS // hp, s // DF_Q_TILE),
        in_specs=[pl.BlockSpec((1, DF_Q_TILE, w), lambda bi, h, i: (bi, i, h)),
                  pl.BlockSpec((1, s, w), lambda bi, h, i: (bi, 0, h)),
                  _vt_spec(hp, s),
                  _Pick(lam_p, j).spec,
                  pl.BlockSpec((None, None, 1, w), lambda bi, h, i: (j, h, 0, 0))],
        out_specs=pl.BlockSpec((1, DF_Q_TILE, w), lambda bi, h, i: (bi, i, h)),
        out_shape=jax.ShapeDtypeStruct((b, s, DF_HEADS * 2 * DF_DH), BF16),
        compiler_params=_params("parallel", "parallel", "arbitrary"),
        name="diff_attn",
    )(q, k, vt, lam_p, g_head)


def _hgrn_kernel(hq_ref, hf_ref, hi_ref, hg_ref, lb_ref, go_ref, o_ref,
                 st_ref, b_ref, kk_ref, bpad_ref, kpad_ref, vpad_ref):
    n_chunks = HG_SEQ_TILE // CHUNK
    mid = CHUNK // 2 - 1

    @pl.when(pl.program_id(1) == 0)
    def _():
        st_ref[...] = jnp.zeros_like(st_ref)

    row_in_chunk = lax.broadcasted_iota(jnp.int32, (HG_GATE_ROWS, 1), 0) % CHUNK
    for hd in range(HG_HEADS):
        ls = slice(hd * HG_DK, (hd + 1) * HG_DK)
        lb = lb_ref[:, ls]
        for r0 in range(0, HG_SEQ_TILE, HG_GATE_ROWS):
            rs = slice(r0, r0 + HG_GATE_ROWS)
            sig = jax.nn.sigmoid(hf_ref[rs, ls])
            kk_ref[hd, rs, :] = (1.0 - lb) * (1.0 - sig)
            b = jnp.log(jnp.maximum(lb + (1.0 - lb) * sig, TINY))
            step = 1
            while step < CHUNK:
                b = b + jnp.where(row_in_chunk >= step, pltpu.roll(b, step, 0), 0.0)
                step *= 2
            b_ref[hd, rs, :] = b

    b_mid = b_ref[:, pl.ds(mid, n_chunks, stride=CHUNK), :]
    b_end = b_ref[:, pl.ds(CHUNK - 1, n_chunks, stride=CHUNK), :]
    worst = jnp.max(jnp.maximum(-b_mid, b_mid - b_end))

    row = lax.broadcasted_iota(jnp.int32, (CHUNK, 1), 0)

    def load(ci, hd):
        rows = pl.ds(pl.multiple_of(ci * CHUNK, CHUNK), CHUNK)
        ls = slice(hd * HG_DK, (hd + 1) * HG_DK)
        return rows, ls, hq_ref[rows, ls], hi_ref[rows, ls], b_ref[hd, rows, :], kk_ref[hd, rows, :]

    def carry_state(hd, qh, vh, b, kk):
        st = st_ref[hd]
        o = _dot_nt((qh * jnp.exp(b)).astype(BF16), st.astype(BF16))
        b_last = b[CHUNK - 1:CHUNK]
        kdec = kk * jnp.exp(b_last - b)
        st_ref[hd] = st * jnp.exp(b_last) + _dot_tn(vh.astype(BF16), kdec.astype(BF16))
        return o

    def finish(rows, ls, o):
        on = _rms(o, go_ref[:, ls]) * jax.nn.silu(hg_ref[rows, ls])
        o_ref[rows, ls] = on.astype(BF16)

    @pl.when(worst < HG_SAFE_LOG)
    def _():
        causal = row >= lax.broadcasted_iota(jnp.int32, (1, CHUNK), 1)

        def chunk_body(cj, carry):
            for u in range(HG_UNROLL):
                ci = cj * HG_UNROLL + u
                for hd in range(HG_HEADS):
                    rows, ls, qh, vh, b, kk = load(ci, hd)
                    o = carry_state(hd, qh, vh, b, kk)
                    b_m = b[mid:mid + 1]
                    qf = (qh * jnp.exp(b - b_m)).astype(BF16)
                    kf = (kk * jnp.exp(b_m - b)).astype(BF16)
                    a = jnp.where(causal, _dot_nt(qf, kf), 0.0)
                    finish(rows, ls, o + _dot(a.astype(BF16), vh.astype(BF16)))
            return carry

        lax.fori_loop(0, n_chunks // HG_UNROLL, chunk_body, 0)

    @pl.when(jnp.logical_not(worst < HG_SAFE_LOG))
    def _():
        n_sub = CHUNK // HG_SUB
        off_w = HG_SUB * (n_sub * (n_sub - 1) // 2)
        zpad = jnp.zeros((HG_HEADS, HG_SUB, HG_DK), F32)
        bpad_ref[:, :HG_SUB, :] = zpad
        kpad_ref[:, :HG_SUB, :] = zpad
        vpad_ref[:, :HG_SUB, :] = zpad
        row_sub = row % HG_SUB
        r2 = lax.broadcasted_iota(jnp.int32, (2 * HG_DK, 2 * HG_DK), 0) // HG_DK
        c2 = lax.broadcasted_iota(jnp.int32, (2 * HG_DK, 2 * HG_DK), 1) // HG_DK
        ones2 = jnp.where(r2 == c2, 1.0, 0.0).astype(BF16)
        col = lax.broadcasted_iota(jnp.int32, (1, off_w), 1)
        col_blk = jnp.zeros((1, off_w), jnp.int32)
        for i in range(1, n_sub):
            col_blk = col_blk + jnp.where(col >= HG_SUB * (i * (i - 1) // 2), 1, 0)
        off_mask = col_blk == (row // HG_SUB)

        def chunk_body(ci, carry):
            for hd in range(HG_HEADS):
                rows, ls, qh, vh, b, kk = load(ci, hd)
                o = carry_state(hd, qh, vh, b, kk)

                refs = [b[i * HG_SUB - 1:i * HG_SUB] for i in range(1, n_sub)]
                bref = jnp.concatenate(
                    [jnp.zeros((HG_SUB, HG_DK), F32)]
                    + [jnp.broadcast_to(r, (HG_SUB, HG_DK)) for r in refs], axis=0)
                qs = qh * jnp.exp(b - bref)
                kst = jnp.concatenate(
                    [kk[:i * HG_SUB] * jnp.exp(refs[i - 1] - b[:i * HG_SUB]) for i in range(1, n_sub)],
                    axis=0)
                vst = jnp.concatenate([vh[:i * HG_SUB] for i in range(1, n_sub)], axis=0)
                a_off = jnp.where(off_mask, _dot_nt(qs.astype(BF16), kst.astype(BF16)), 0.0)
                o = o + _dot(a_off.astype(BF16), vst.astype(BF16))

                bpad_ref[hd, HG_SUB:, :] = b
                kpad_ref[hd, HG_SUB:, :] = kk
                vpad_ref[hd, HG_SUB:, :] = vh
                for dp in range(HG_SUB // 2):
                    terms = []
                    for d in (2 * dp, 2 * dp + 1):
                        lo = HG_SUB - d
                        w = qh * jnp.exp(b - bpad_ref[hd, lo:lo + CHUNK, :]) * kpad_ref[hd, lo:lo + CHUNK, :]
                        terms.append(jnp.where(row_sub >= d, w, 0.0))
                    dsum = _dot(jnp.concatenate(terms, axis=1).astype(BF16), ones2)
                    for j, d in enumerate((2 * dp, 2 * dp + 1)):
                        lo = HG_SUB - d
                        o = o + dsum[:, j * HG_DK:(j + 1) * HG_DK] * vpad_ref[hd, lo:lo + CHUNK, :]
                finish(rows, ls, o)
            return carry

        lax.fori_loop(0, n_chunks, chunk_body, 0)


def _hgrn_call(zh, lb, g_out, b, s):
    width = HG_HEADS * HG_DK
    n_seq = s // HG_SEQ_TILE
    part = lambda j: pl.BlockSpec((HG_SEQ_TILE, width), lambda bi, si: (bi * n_seq + si, j))
    pad = pltpu.VMEM((HG_HEADS, HG_SUB + CHUNK, HG_DK), F32)
    return pl.pallas_call(
        _hgrn_kernel,
        grid=(b, n_seq),
        in_specs=[part(0), part(1), part(2), part(3), lb.spec, g_out.spec],
        out_specs=pl.BlockSpec((HG_SEQ_TILE, width), lambda bi, si: (bi * n_seq + si, 0)),
        out_shape=jax.ShapeDtypeStruct((b * s, width), BF16),
        scratch_shapes=[pltpu.VMEM((HG_HEADS, HG_DV, HG_DK), F32),
                        pltpu.VMEM((HG_HEADS, HG_SEQ_TILE, HG_DK), F32),
                        pltpu.VMEM((HG_HEADS, HG_SEQ_TILE, HG_DK), F32),
                        pad, pad, pad],
        compiler_params=_params("parallel", "arbitrary"),
        name="hgrn2",
    )(zh, zh, zh, zh, lb.arr, g_out.arr)


def _rope_tables(positions, dim, group):
    half = dim // 2
    inv_freq = ROPE_THETA ** (-jnp.arange(0, dim, 2, dtype=F32) / dim)
    ang = positions.astype(F32).reshape(-1, 1) * inv_freq
    cos, sin = jnp.cos(ang), jnp.sin(ang)
    t = ang.shape[0]
    rest = group - dim
    c = jnp.concatenate([cos, cos, jnp.ones((t, rest), F32)], axis=1)
    sa = jnp.concatenate([-sin, jnp.zeros((t, half + rest), F32)], axis=1)
    sb = jnp.concatenate([jnp.zeros((t, half), F32), sin, jnp.zeros((t, rest), F32)], axis=1)
    rep = LANES // group
    return tuple(jnp.tile(a, (1, rep)) for a in (c, sa, sb))


def kernel(x, positions, norm_g, ffn_w_gate, ffn_w_up, ffn_w_down, ev_w_in, ev_g_q, ev_w_uq, ev_g_kv, ev_w_ukv, ev_lb_logits, ev_g_out, ev_w_out, od_w_in, od_lambda, od_g_head, od_w_out):
    b, s, d = x.shape
    t = b * s
    xt = x.reshape(t, d)
    tabs_m = _rope_tables(positions, MLA_ROPE, LANES)
    tabs_d = _rope_tables(positions, DF_ROT, DF_DH)
    lb_w = jax.nn.softmax(ev_lb_logits.astype(F32), axis=0)
    lb_all = jnp.cumsum(lb_w, axis=0) - lb_w[0:1]

    wg = ffn_w_gate.astype(BF16)
    wu = ffn_w_up.astype(BF16)
    wd = ffn_w_down.astype(BF16)
    n_even = ev_w_in.shape[0]
    o_pe = MLA_Q_RANK + MLA_KV_RANK + MLA_ROPE
    w_in_e = jnp.concatenate(
        [ev_w_in[..., :o_pe].astype(BF16), jnp.zeros((n_even, d, LANES - MLA_ROPE), BF16),
         ev_w_in[..., o_pe:].astype(BF16)], axis=-1)
    w_uq = ev_w_uq.astype(BF16).reshape(n_even, MLA_Q_RANK, MLA_HEADS, MLA_NOPE + MLA_ROPE)
    w_uq = jnp.pad(w_uq, ((0, 0), (0, 0), (0, 0), (0, MLA_QK_PAD - MLA_NOPE - MLA_ROPE)))
    w_uq = w_uq.reshape(n_even, MLA_Q_RANK, MLA_HEADS * MLA_QK_PAD)
    w_ukv = ev_w_ukv.astype(BF16).reshape(n_even, MLA_KV_RANK, MLA_HEADS, 2, MLA_NOPE)
    w_ukv = w_ukv.transpose(0, 1, 3, 2, 4).reshape(n_even, MLA_KV_RANK, 2 * MLA_HEADS * MLA_NOPE)
    w_out_e = ev_w_out.astype(BF16)
    w_in_o = od_w_in.astype(BF16)
    w_out_o = od_w_out.astype(BF16)
    g_q = ev_g_q.reshape(n_even, 1, -1)
    g_kv = ev_g_kv.reshape(n_even, 1, -1)
    g_out = ev_g_out.reshape(n_even, 1, -1)
    lb_all = lb_all.reshape(n_even, 1, -1)

    for l in range(DEPTH):
        xt = _ffn_call(xt, _Pick(norm_g, l, 0), _Pick(wg, l, 0), _Pick(wu, l, 0), _Pick(wd, l, 0))
        j = l // 2
        g_mix = _Pick(norm_g, l, 1)
        if l % 2 == 0:
            q, k, v, zh = _even_pre_call(
                xt, g_mix, _Pick(w_in_e, j), _Pick(g_q, j), _Pick(w_uq, j), _Pick(g_kv, j),
                _Pick(w_ukv, j), tabs_m)
            o_a = _mla_attn_call(q.reshape(b, s, -1), k.reshape(b, s, -1), v)
            o_b = _hgrn_call(zh, _Pick(lb_all, j), _Pick(g_out, j), b, s)
            parts = [o_a.reshape(t, -1), o_b]
            w_out = _Pick(w_out_e, j)
        else:
            lambda_init = 0.8 - 0.6 * math.exp(-0.3 * l)
            q, k, v = _odd_pre_call(xt, g_mix, _Pick(w_in_o, j), tabs_d)
            o = _diff_attn_call(q.reshape(b, s, -1), k.reshape(b, s, -1), v, od_lambda, od_g_head, j,
                                lambda_init)
            parts = [o.reshape(t, -1)]
            w_out = _Pick(w_out_o, j)
        xt = _mix_ffn_call(xt, parts, w_out, g_mix, _Pick(norm_g, l, 2),
                           _Pick(wg, l, 1), _Pick(wu, l, 1), _Pick(wd, l, 1))
    return xt.reshape(b, s, d)
```

```python
import functools
import math

import jax
import jax.numpy as jnp
from jax import lax
from jax.experimental import pallas as pl
from jax.experimental.pallas import tpu as pltpu

D_MODEL = 1024
DEPTH = 4
CHUNK = 64
ROPE_THETA = 500000.0
EPS = 1e-6
NEG_INF = -1e30
LOG2_E = math.log2(math.e)
TINY = 1e-30
D_FF = 2816
MLA_HEADS = 4
MLA_NOPE = 128
MLA_ROPE = 64
MLA_V = 128
MLA_Q_RANK = 384
MLA_KV_RANK = 256
HG_HEADS = 4
HG_DK = 128
HG_DV = 128
DF_HEADS = 8
DF_DH = 64
DF_ROT = DF_DH // 4

LANES = 128
MXU_DIM = 256
VMEM_LIMIT = 56 * 1024 * 1024

TOKEN_TILE = 512
MLA_Q_TILE = 512
DF_Q_TILE = 256
KEY_BLOCK = 1024
ATTN_HEADS_PER_STEP = 2
HG_SEQ_TILE = 1024
HG_SUB = 16
HG_GATE_ROWS = 128
HG_UNROLL = 4
HG_SAFE_LOG = 75.0
FFN_ROW_GROUPS = 2
PRE_ROW_GROUPS = 2
FF_CHUNKS =((0, 1024), (1024, 2048), (2048, D_FF))

MLA_QK_PAD = 2 * LANES
V_ROWS = LANES + 16
EVEN_Z = MLA_Q_RANK + MLA_KV_RANK + LANES + 4 * HG_HEADS * HG_DK

F32 = jnp.float32
BF16 = jnp.bfloat16


def _rms(x, g):
    ms = jnp.mean(x * x, axis=-1, keepdims=True)
    return x * lax.rsqrt(ms + EPS) * g


def _dot(a, b):
    return jnp.dot(a, b, preferred_element_type=F32)


def _dot_nt(a, b):
    return lax.dot_general(a, b, (((1,), (1,)), ((), ())), preferred_element_type=F32)


def _dot_tn(a, b):
    return lax.dot_general(a, b, (((0,), (0,)), ((), ())), preferred_element_type=F32)


def _rope_slab(x, c, sa, sb, half):
    return x * c + pltpu.roll(x, LANES - half, 1) * sa + pltpu.roll(x, half, 1) * sb


def _store_values(vt_ref, rs, v, heads):
    ones = jnp.ones((V_ROWS - LANES, v.shape[0]), BF16)
    for hd in range(heads):
        vt_ref[hd, :LANES, rs] = v[:, hd * LANES:(hd + 1) * LANES].T.astype(BF16)
        vt_ref[hd, LANES:, rs] = ones


def _vt_out(heads, t):
    per_blk = KEY_BLOCK // TOKEN_TILE
    spec = pl.BlockSpec((heads, None, V_ROWS, TOKEN_TILE), lambda i: (0, i // per_blk, 0, i % per_blk))
    return spec, jax.ShapeDtypeStruct((heads, t // KEY_BLOCK, V_ROWS, KEY_BLOCK), BF16)


class _Pick:
    def __init__(self, arr, *idx):
        self.arr = arr
        nd = arr.ndim - len(idx)
        self.spec = pl.BlockSpec((None,) * len(idx) + arr.shape[len(idx):],
                                 lambda *_: idx + (0,) * nd, pipeline_mode=pl.Buffered(1))


def _params(*sem):
    return pltpu.CompilerParams(dimension_semantics=sem, vmem_limit_bytes=VMEM_LIMIT)


def _ffn_apply(xs, gpre, gpost, wg_ref, wu_ref, wd_ref):
    xn = [_rms(x, gpre).astype(BF16) for x in xs]
    acc = [None] * len(xs)
    for lo, hi in FF_CHUNKS:
        for i in range(len(xs)):
            g = _dot(xn[i], wg_ref[:, lo:hi])
            u = _dot(xn[i], wu_ref[:, lo:hi])
            a = (jax.nn.silu(g) * u).astype(BF16)
            h = _dot(a, wd_ref[lo:hi, :])
            acc[i] = h if acc[i] is None else acc[i] + h
    return [x + 0.5 * _rms(a, gpost) for x, a in zip(xs, acc)]


def _row_groups(n):
    step = TOKEN_TILE // n
    return [slice(i * step, (i + 1) * step) for i in range(n)]


def _ffn_kernel(x_ref, g_ref, wg_ref, wu_ref, wd_ref, o_ref):
    g = g_ref[...]
    groups = _row_groups(FFN_ROW_GROUPS)
    outs = _ffn_apply([x_ref[rs, :] for rs in groups], g[0:1], g[1:2], wg_ref, wu_ref, wd_ref)
    for rs, o in zip(groups, outs):
        o_ref[rs, :] = o


def _ffn_call(x, g2, wg, wu, wd):
    t = x.shape[0]
    tile = pl.BlockSpec((TOKEN_TILE, D_MODEL), lambda i: (i, 0))
    picks = [g2, wg, wu, wd]
    return pl.pallas_call(
        _ffn_kernel,
        grid=(t // TOKEN_TILE,),
        in_specs=[tile] + [p.spec for p in picks],
        out_specs=tile,
        out_shape=jax.ShapeDtypeStruct(x.shape, F32),
        compiler_params=_params("parallel"),
        name="ffn",
    )(x, *[p.arr for p in picks])


def _mix_ffn_kernel(n_parts, *refs):
    x_ref = refs[0]
    part_refs = refs[1:1 + n_parts]
    wo_ref, gm_ref, gf_ref, wg_ref, wu_ref, wd_ref, o_ref = refs[1 + n_parts:]
    gf = gf_ref[...]
    groups = _row_groups(FFN_ROW_GROUPS)
    xs = []
    for rs in groups:
        m = None
        row = 0
        for p_ref in part_refs:
            w = p_ref.shape[-1]
            d = _dot(p_ref[rs, :], wo_ref[row:row + w, :])
            m = d if m is None else m + d
            row += w
        xs.append(x_ref[rs, :] + _rms(m, gm_ref[1:2]))
    outs = _ffn_apply(xs, gf[0:1], gf[1:2], wg_ref, wu_ref, wd_ref)
    for rs, o in zip(groups, outs):
        o_ref[rs, :] = o


def _mix_ffn_call(x, parts, wo, g_mix, g_ffn, wg, wu, wd):
    t = x.shape[0]
    tile = pl.BlockSpec((TOKEN_TILE, D_MODEL), lambda i: (i, 0))
    part_specs = [pl.BlockSpec((TOKEN_TILE, p.shape[-1]), lambda i: (i, 0)) for p in parts]
    picks = [wo, g_mix, g_ffn, wg, wu, wd]
    return pl.pallas_call(
        functools.partial(_mix_ffn_kernel, len(parts)),
        grid=(t // TOKEN_TILE,),
        in_specs=[tile] + part_specs + [p.spec for p in picks],
        out_specs=tile,
        out_shape=jax.ShapeDtypeStruct(x.shape, F32),
        compiler_params=_params("parallel"),
        name="mix_ffn",
    )(x, *parts, *[p.arr for p in picks])


def _even_pre_kernel(x_ref, g_ref, win_ref, gq_ref, wuq_ref, gkv_ref, wukv_ref,
                     c_ref, sa_ref, sb_ref, q_ref, k_ref, v_ref, zh_ref):
    o_kv = MLA_Q_RANK
    o_pe = o_kv + MLA_KV_RANK
    o_h = o_pe + LANES
    half = MLA_ROPE // 2
    scale = LOG2_E * (MLA_NOPE + MLA_ROPE) ** -0.5
    for rs in _row_groups(PRE_ROW_GROUPS):
        h = _rms(x_ref[rs, :], g_ref[0:1]).astype(BF16)
        z = _dot(h, win_ref[...])
        zh_ref[rs, :] = z[:, o_h:]
        c, sa, sb = c_ref[rs, :], sa_ref[rs, :], sb_ref[rs, :]

        cq = _rms(z[:, :o_kv], gq_ref[...]).astype(BF16)
        q = _dot(cq, wuq_ref[...]) * scale
        for hd in range(MLA_HEADS):
            base = hd * MLA_QK_PAD
            q_ref[rs, base:base + LANES] = q[:, base:base + LANES].astype(BF16)
            q_ref[rs, base + LANES:base + 2 * LANES] = _rope_slab(
                q[:, base + LANES:base + 2 * LANES], c, sa, sb, half).astype(BF16)

        ckv = _rms(z[:, o_kv:o_pe], gkv_ref[...]).astype(BF16)
        kv = _dot(ckv, wukv_ref[...])
        kpe = _rope_slab(z[:, o_pe:o_h], c, sa, sb, half).astype(BF16)
        for hd in range(MLA_HEADS):
            base = hd * MLA_QK_PAD
            k_ref[rs, base:base + LANES] = kv[:, hd * LANES:(hd + 1) * LANES].astype(BF16)
            k_ref[rs, base + LANES:base + 2 * LANES] = kpe
        _store_values(v_ref, rs, kv[:, MLA_HEADS * MLA_NOPE:], MLA_HEADS)


def _even_pre_call(x, g, win, gq, wuq, gkv, wukv, tabs):
    t = x.shape[0]
    picks = [g, win, gq, wuq, gkv, wukv]
    row = lambda w: pl.BlockSpec((TOKEN_TILE, w), lambda i: (i, 0))
    qk_w = MLA_HEADS * MLA_QK_PAD
    vt_spec, vt_shape = _vt_out(MLA_HEADS, t)
    zh_w = 4 * HG_HEADS * HG_DK
    return pl.pallas_call(
        _even_pre_kernel,
        grid=(t // TOKEN_TILE,),
        in_specs=[row(D_MODEL)] + [p.spec for p in picks] + [row(LANES), row(LANES), row(LANES)],
        out_specs=[row(qk_w), row(qk_w), vt_spec, row(zh_w)],
        out_shape=[jax.ShapeDtypeStruct((t, qk_w), BF16), jax.ShapeDtypeStruct((t, qk_w), BF16),
                   vt_shape, jax.ShapeDtypeStruct((t, zh_w), F32)],
        compiler_params=_params("parallel"),
        name="even_pre",
    )(x, *[p.arr for p in picks], *tabs)


def _odd_pre_kernel(x_ref, g_ref, win_ref, c_ref, sa_ref, sb_ref, q_ref, k_ref, v_ref):
    half = DF_ROT // 2
    width = DF_HEADS * 2 * DF_DH
    scale = LOG2_E * DF_DH ** -0.5
    for rs in _row_groups(PRE_ROW_GROUPS):
        h = _rms(x_ref[rs, :], g_ref[0:1]).astype(BF16)
        c, sa, sb = c_ref[rs, :], sa_ref[rs, :], sb_ref[rs, :]
        q = _dot(h, win_ref[:, :width]) * scale
        k = _dot(h, win_ref[:, width:2 * width])
        for j in range(width // LANES):
            sl = slice(j * LANES, (j + 1) * LANES)
            q_ref[rs, sl] = _rope_slab(q[:, sl], c, sa, sb, half).astype(BF16)
            k_ref[rs, sl] = _rope_slab(k[:, sl], c, sa, sb, half).astype(BF16)
        _store_values(v_ref, rs, _dot(h, win_ref[:, 2 * width:]), DF_HEADS)


def _odd_pre_call(x, g, win, tabs):
    t = x.shape[0]
    width = DF_HEADS * 2 * DF_DH
    row = lambda w: pl.BlockSpec((TOKEN_TILE, w), lambda i: (i, 0))
    vt_spec, vt_shape = _vt_out(DF_HEADS, t)
    return pl.pallas_call(
        _odd_pre_kernel,
        grid=(t // TOKEN_TILE,),
        in_specs=[row(D_MODEL), g.spec, win.spec, row(LANES), row(LANES), row(LANES)],
        out_specs=[row(width), row(width), vt_spec],
        out_shape=[jax.ShapeDtypeStruct((t, width), BF16), jax.ShapeDtypeStruct((t, width), BF16), vt_shape],
        compiler_params=_params("parallel"),
        name="odd_pre",
    )(x, g.arr, win.arr, *tabs)


def _flash(streams, q_chunk, k_ref, vt_ref, m_ref, acc_ref, row0, tq):
    m_rows = streams[0][0].shape[0]
    for i in range(len(streams)):
        m_ref[i] = jnp.full((1, m_rows), NEG_INF, F32)
        acc_ref[i] = jnp.zeros((V_ROWS, m_rows), F32)

    def step(blk, width, masked):
        start = pl.multiple_of(blk * KEY_BLOCK, KEY_BLOCK)
        scores = [_dot_nt(k_ref[0, pl.ds(start, width), ksl], q) for q, ksl, _ in streams]
        for i, (s, (_, _, hd)) in enumerate(zip(scores, streams)):
            if masked:
                k_chunk = (start + (width - tq) + lax.broadcasted_iota(jnp.int32, (tq, 1), 0)) // CHUNK
                tail = jnp.where(k_chunk <= q_chunk, s[width - tq:], NEG_INF)
                s = tail if width == tq else jnp.concatenate([s[:width - tq], tail], axis=0)
            m_i = m_ref[i]
            m_new = jnp.maximum(m_i, jnp.max(s, axis=0, keepdims=True))
            m_ref[i] = m_new
            p = jnp.exp2((s - m_new).astype(BF16))
            acc_ref[i] = jnp.exp2(m_i - m_new) * acc_ref[i] + _dot(vt_ref[hd, blk, :, :width], p)

    n_full = row0 // KEY_BLOCK

    def unmasked(blk, carry):
        step(blk, KEY_BLOCK, False)
        return carry

    lax.fori_loop(0, n_full, unmasked, 0)
    for p in range(KEY_BLOCK // tq):
        pl.when((row0 % KEY_BLOCK) // tq == p)(functools.partial(step, n_full, tq * (p + 1), True))
    outs = []
    for i in range(len(streams)):
        acc = acc_ref[i]
        outs.append(acc[:LANES] * (1.0 / acc[LANES:LANES + 1]))
    return outs


def _row_chunks(row0, n_rows):
    rows = row0 + lax.broadcasted_iota(jnp.int32, (1, n_rows), 1)
    return rows // CHUNK


def _flash_scratch(hp, m_rows):
    return [pltpu.VMEM((hp, 1, m_rows), F32), pltpu.VMEM((hp, V_ROWS, m_rows), F32)]


def _vt_spec(hp, s):
    return pl.BlockSpec((hp, s // KEY_BLOCK, V_ROWS, KEY_BLOCK), lambda bi, h, i: (h, bi, 0, 0))


def _mla_attn_kernel(q_ref, k_ref, vt_ref, o_ref, m_ref, acc_ref):
    row0 = pl.program_id(2) * MLA_Q_TILE
    streams = [(q_ref[0, :, h * MLA_QK_PAD:(h + 1) * MLA_QK_PAD],
                slice(h * MLA_QK_PAD, (h + 1) * MLA_QK_PAD), h)
               for h in range(ATTN_HEADS_PER_STEP)]
    outs = _flash(streams, _row_chunks(row0, MLA_Q_TILE), k_ref, vt_ref, m_ref, acc_ref, row0, MLA_Q_TILE)
    for h, o_t in enumerate(outs):
        o_ref[0, :, h * MLA_V:(h + 1) * MLA_V] = o_t.T.astype(BF16)


def _mla_attn_call(q, k, vt):
    b, s, _ = q.shape
    hp = ATTN_HEADS_PER_STEP
    return pl.pallas_call(
        _mla_attn_kernel,
        grid=(b, MLA_HEADS // hp, s // MLA_Q_TILE),
        in_specs=[pl.BlockSpec((1, MLA_Q_TILE, hp * MLA_QK_PAD), lambda bi, h, i: (bi, i, h)),
                  pl.BlockSpec((1, s, hp * MLA_QK_PAD), lambda bi, h, i: (bi, 0, h)),
                  _vt_spec(hp, s)],
        out_specs=pl.BlockSpec((1, MLA_Q_TILE, hp * MLA_V), lambda bi, h, i: (bi, i, h)),
        out_shape=jax.ShapeDtypeStruct((b, s, MLA_HEADS * MLA_V), BF16),
        scratch_shapes=_flash_scratch(hp, MLA_Q_TILE),
        compiler_params=_params("parallel", "parallel", "arbitrary"),
        name="mla_attn",
    )(q, k, vt)


def _diff_attn_kernel(lambda_init, q_ref, k_ref, vt_ref, lam_ref, gh_ref, o_ref, m_ref, acc_ref):
    row0 = pl.program_id(2) * DF_Q_TILE
    dv = 2 * DF_DH
    lane = lax.broadcasted_iota(jnp.int32, (1, dv), 1)
    streams = []
    for h in range(ATTN_HEADS_PER_STEP):
        hs = slice(h * dv, (h + 1) * dv)
        q = q_ref[0, :, hs]
        zero = jnp.zeros_like(q)
        q2 = jnp.concatenate([jnp.where(lane < DF_DH, q, zero), jnp.where(lane >= DF_DH, q, zero)], axis=0)
        streams.append((q2, hs, h))
    chunks = _row_chunks(row0, DF_Q_TILE)
    outs = _flash(streams, jnp.concatenate([chunks, chunks], axis=1), k_ref, vt_ref, m_ref, acc_ref, row0,
                  DF_Q_TILE)
    lp = lam_ref[...]
    lam = (jnp.exp(jnp.sum(lp[0:1] * lp[1:2], axis=-1, keepdims=True))
           - jnp.exp(jnp.sum(lp[2:3] * lp[3:4], axis=-1, keepdims=True)) + lambda_init)
    for h, o2_t in enumerate(outs):
        hs = slice(h * dv, (h + 1) * dv)
        o = (o2_t[:, :DF_Q_TILE] - lam * o2_t[:, DF_Q_TILE:]).T
        o_ref[0, :, hs] = (_rms(o, gh_ref[:, hs]) * (1.0 - lambda_init)).astype(BF16)


def _diff_attn_call(q, k, vt, lam_p, g_head, j, lambda_init):
    b, s, _ = q.shape
    hp = ATTN_HEADS_PER_STEP
    w = hp * 2 * DF_DH
    g_head = g_head.reshape(g_head.shape[0], DF_HEADS // hp, 1, w)
    return pl.pallas_call(
        functools.partial(_diff_attn_kernel, lambda_init),
        grid=(b, DF_HEADS // hp, s // DF_Q_TILE),
        in_specs=[pl.BlockSpec((1, DF_Q_TILE, w), lambda bi, h, i: (bi, i, h)),
                  pl.BlockSpec((1, s, w), lambda bi, h, i: (bi, 0, h)),
                  _vt_spec(hp, s),
                  _Pick(lam_p, j).spec,
                  pl.BlockSpec((None, None, 1, w), lambda bi, h, i: (j, h, 0, 0))],
        out_specs=pl.BlockSpec((1, DF_Q_TILE, w), lambda bi, h, i: (bi, i, h)),
        out_shape=jax.ShapeDtypeStruct((b, s, DF_HEADS * 2 * DF_DH), BF16),
        scratch_shapes=_flash_scratch(hp, 2 * DF_Q_TILE),
        compiler_params=_params("parallel", "parallel", "arbitrary"),
        name="diff_attn",
    )(q, k, vt, lam_p, g_head)


def _hgrn_kernel(hq_ref, hf_ref, hi_ref, hg_ref, lb_ref, go_ref, o_ref,
                 st_ref, b_ref, kk_ref, bpad_ref, kpad_ref, vpad_ref):
    n_chunks = HG_SEQ_TILE // CHUNK
    mid = CHUNK // 2 - 1

    @pl.when(pl.program_id(1) == 0)
    def _():
        st_ref[...] = jnp.zeros_like(st_ref)

    row_in_chunk = lax.broadcasted_iota(jnp.int32, (HG_GATE_ROWS, 1), 0) % CHUNK
    for hd in range(HG_HEADS):
        ls = slice(hd * HG_DK, (hd + 1) * HG_DK)
        lb = lb_ref[:, ls]
        for r0 in range(0, HG_SEQ_TILE, HG_GATE_ROWS):
            rs = slice(r0, r0 + HG_GATE_ROWS)
            sig = jax.nn.sigmoid(hf_ref[rs, ls])
            kk_ref[hd, rs, :] = (1.0 - lb) * (1.0 - sig)
            b = jnp.log(jnp.maximum(lb + (1.0 - lb) * sig, TINY))
            step = 1
            while step < CHUNK:
                b = b + jnp.where(row_in_chunk >= step, pltpu.roll(b, step, 0), 0.0)
                step *= 2
            b_ref[hd, rs, :] = b

    b_mid = b_ref[:, pl.ds(mid, n_chunks, stride=CHUNK), :]
    b_end = b_ref[:, pl.ds(CHUNK - 1, n_chunks, stride=CHUNK), :]
    worst = jnp.max(jnp.maximum(-b_mid, b_mid - b_end))

    row = lax.broadcasted_iota(jnp.int32, (CHUNK, 1), 0)

    def load(ci, hd):
        rows = pl.ds(pl.multiple_of(ci * CHUNK, CHUNK), CHUNK)
        ls = slice(hd * HG_DK, (hd + 1) * HG_DK)
        return rows, ls, hq_ref[rows, ls], hi_ref[rows, ls], b_ref[hd, rows, :], kk_ref[hd, rows, :]

    def carry_state(hd, qh, vh, b, kk):
        st = st_ref[hd]
        o = _dot_nt((qh * jnp.exp(b)).astype(BF16), st.astype(BF16))
        b_last = b[CHUNK - 1:CHUNK]
        kdec = kk * jnp.exp(b_last - b)
        st_ref[hd] = st * jnp.exp(b_last) + _dot_tn(vh.astype(BF16), kdec.astype(BF16))
        return o

    def finish(rows, ls, o):
        on = _rms(o, go_ref[:, ls]) * jax.nn.silu(hg_ref[rows, ls])
        o_ref[rows, ls] = on.astype(BF16)

    @pl.when(worst < HG_SAFE_LOG)
    def _():
        causal = row >= lax.broadcasted_iota(jnp.int32, (1, CHUNK), 1)

        def chunk_body(cj, carry):
            for u in range(HG_UNROLL):
                ci = cj * HG_UNROLL + u
                for hd in range(HG_HEADS):
                    rows, ls, qh, vh, b, kk = load(ci, hd)
                    o = carry_state(hd, qh, vh, b, kk)
                    b_m = b[mid:mid + 1]
                    qf = (qh * jnp.exp(b - b_m)).astype(BF16)
                    kf = (kk * jnp.exp(b_m - b)).astype(BF16)
                    a = jnp.where(causal, _dot_nt(qf, kf), 0.0)
                    finish(rows, ls, o + _dot(a.astype(BF16), vh.astype(BF16)))
            return carry

        lax.fori_loop(0, n_chunks // HG_UNROLL, chunk_body, 0)

    @pl.when(jnp.logical_not(worst < HG_SAFE_LOG))
    def _():
        n_sub = CHUNK // HG_SUB
        off_w = HG_SUB * (n_sub * (n_sub - 1) // 2)
        zpad = jnp.zeros((HG_HEADS, HG_SUB, HG_DK), F32)
        bpad_ref[:, :HG_SUB, :] = zpad
        kpad_ref[:, :HG_SUB, :] = zpad
        vpad_ref[:, :HG_SUB, :] = zpad
        row_sub = row % HG_SUB
        r2 = lax.broadcasted_iota(jnp.int32, (2 * HG_DK, 2 * HG_DK), 0) // HG_DK
        c2 = lax.broadcasted_iota(jnp.int32, (2 * HG_DK, 2 * HG_DK), 1) // HG_DK
        ones2 = jnp.where(r2 == c2, 1.0, 0.0).astype(BF16)
        col = lax.broadcasted_iota(jnp.int32, (1, off_w), 1)
        col_blk = jnp.zeros((1, off_w), jnp.int32)
        for i in range(1, n_sub):
            col_blk = col_blk + jnp.where(col >= HG_SUB * (i * (i - 1) // 2), 1, 0)
        off_mask = col_blk == (row // HG_SUB)

        def chunk_body(ci, carry):
            for hd in range(HG_HEADS):
                rows, ls, qh, vh, b, kk = load(ci, hd)
                o = carry_state(hd, qh, vh, b, kk)

                refs = [b[i * HG_SUB - 1:i * HG_SUB] for i in range(1, n_sub)]
                bref = jnp.concatenate(
                    [jnp.zeros((HG_SUB, HG_DK), F32)]
                    + [jnp.broadcast_to(r, (HG_SUB, HG_DK)) for r in refs], axis=0)
                qs = qh * jnp.exp(b - bref)
                kst = jnp.concatenate(
                    [kk[:i * HG_SUB] * jnp.exp(refs[i - 1] - b[:i * HG_SUB]) for i in range(1, n_sub)],
                    axis=0)
                vst = jnp.concatenate([vh[:i * HG_SUB] for i in range(1, n_sub)], axis=0)
                a_off = jnp.where(off_mask, _dot_nt(qs.astype(BF16), kst.astype(BF16)), 0.0)
                o = o + _dot(a_off.astype(BF16), vst.astype(BF16))

                bpad_ref[hd, HG_SUB:, :] = b
                kpad_ref[hd, HG_SUB:, :] = kk
                vpad_ref[hd, HG_SUB:, :] = vh
                for dp in range(HG_SUB // 2):
                    terms = []
                    for d in (2 * dp, 2 * dp + 1):
                        lo = HG_SUB - d
                        w = qh * jnp.exp(b - bpad_ref[hd, lo:lo + CHUNK, :]) * kpad_ref[hd, lo:lo + CHUNK, :]
                        terms.append(jnp.where(row_sub >= d, w, 0.0))
                    dsum = _dot(jnp.concatenate(terms, axis=1).astype(BF16), ones2)
                    for j, d in enumerate((2 * dp, 2 * dp + 1)):
                        lo = HG_SUB - d
                        o = o + dsum[:, j * HG_DK:(j + 1) * HG_DK] * vpad_ref[hd, lo:lo + CHUNK, :]
                finish(rows, ls, o)
            return carry

        lax.fori_loop(0, n_chunks, chunk_body, 0)


def _hgrn_call(zh, lb, g_out, b, s):
    width = HG_HEADS * HG_DK
    n_seq = s // HG_SEQ_TILE
    part = lambda j: pl.BlockSpec((HG_SEQ_TILE, width), lambda bi, si: (bi * n_seq + si, j))
    pad = pltpu.VMEM((HG_HEADS, HG_SUB + CHUNK, HG_DK), F32)
    return pl.pallas_call(
        _hgrn_kernel,
        grid=(b, n_seq),
        in_specs=[part(0), part(1), part(2), part(3), lb.spec, g_out.spec],
        out_specs=pl.BlockSpec((HG_SEQ_TILE, width), lambda bi, si: (bi * n_seq + si, 0)),
        out_shape=jax.ShapeDtypeStruct((b * s, width), BF16),
        scratch_shapes=[pltpu.VMEM((HG_HEADS, HG_DV, HG_DK), F32),
                        pltpu.VMEM((HG_HEADS, HG_SEQ_TILE, HG_DK), F32),
                        pltpu.VMEM((HG_HEADS, HG_SEQ_TILE, HG_DK), F32),
                        pad, pad, pad],
        compiler_params=_params("parallel", "arbitrary"),
        name="hgrn2",
    )(zh, zh, zh, zh, lb.arr, g_out.arr)


def _rope_tables(positions, dim, group):
    half = dim // 2
    inv_freq = ROPE_THETA ** (-jnp.arange(0, dim, 2, dtype=F32) / dim)
    ang = positions.astype(F32).reshape(-1, 1) * inv_freq
    cos, sin = jnp.cos(ang), jnp.sin(ang)
    t = ang.shape[0]
    rest = group - dim
    c = jnp.concatenate([cos, cos, jnp.ones((t, rest), F32)], axis=1)
    sa = jnp.concatenate([-sin, jnp.zeros((t, half + rest), F32)], axis=1)
    sb = jnp.concatenate([jnp.zeros((t, half), F32), sin, jnp.zeros((t, rest), F32)], axis=1)
    rep = LANES // group
    return tuple(jnp.tile(a, (1, rep)) for a in (c, sa, sb))


def kernel(x, positions, norm_g, ffn_w_gate, ffn_w_up, ffn_w_down, ev_w_in, ev_g_q, ev_w_uq, ev_g_kv, ev_w_ukv, ev_lb_logits, ev_g_out, ev_w_out, od_w_in, od_lambda, od_g_head, od_w_out):
    b, s, d = x.shape
    t = b * s
    xt = x.reshape(t, d)
    tabs_m = _rope_tables(positions, MLA_ROPE, LANES)
    tabs_d = _rope_tables(positions, DF_ROT, DF_DH)
    lb_w = jax.nn.softmax(ev_lb_logits.astype(F32), axis=0)
    lb_all = jnp.cumsum(lb_w, axis=0) - lb_w[0:1]

    wg = ffn_w_gate.astype(BF16)
    wu = ffn_w_up.astype(BF16)
    wd = ffn_w_down.astype(BF16)
    n_even = ev_w_in.shape[0]
    o_pe = MLA_Q_RANK + MLA_KV_RANK + MLA_ROPE
    w_in_e = jnp.concatenate(
        [ev_w_in[..., :o_pe].astype(BF16), jnp.zeros((n_even, d, LANES - MLA_ROPE), BF16),
         ev_w_in[..., o_pe:].astype(BF16)], axis=-1)
    w_uq = ev_w_uq.astype(BF16).reshape(n_even, MLA_Q_RANK, MLA_HEADS, MLA_NOPE + MLA_ROPE)
    w_uq = jnp.pad(w_uq, ((0, 0), (0, 0), (0, 0), (0, MLA_QK_PAD - MLA_NOPE - MLA_ROPE)))
    w_uq = w_uq.reshape(n_even, MLA_Q_RANK, MLA_HEADS * MLA_QK_PAD)
    w_ukv = ev_w_ukv.astype(BF16).reshape(n_even, MLA_KV_RANK, MLA_HEADS, 2, MLA_NOPE)
    w_ukv = w_ukv.transpose(0, 1, 3, 2, 4).reshape(n_even, MLA_KV_RANK, 2 * MLA_HEADS * MLA_NOPE)
    w_out_e = ev_w_out.astype(BF16)
    w_in_o = od_w_in.astype(BF16)
    w_out_o = od_w_out.astype(BF16)
    g_q = ev_g_q.reshape(n_even, 1, -1)
    g_kv = ev_g_kv.reshape(n_even, 1, -1)
    g_out = ev_g_out.reshape(n_even, 1, -1)
    lb_all = lb_all.reshape(n_even, 1, -1)

    for l in range(DEPTH):
        xt = _ffn_call(xt, _Pick(norm_g, l, 0), _Pick(wg, l, 0), _Pick(wu, l, 0), _Pick(wd, l, 0))
        j = l // 2
        g_mix = _Pick(norm_g, l, 1)
        if l % 2 == 0:
            q, k, v, zh = _even_pre_call(
                xt, g_mix, _Pick(w_in_e, j), _Pick(g_q, j), _Pick(w_uq, j), _Pick(g_kv, j),
                _Pick(w_ukv, j), tabs_m)
            o_a = _mla_attn_call(q.reshape(b, s, -1), k.reshape(b, s, -1), v)
            o_b = _hgrn_call(zh, _Pick(lb_all, j), _Pick(g_out, j), b, s)
            parts = [o_a.reshape(t, -1), o_b]
            w_out = _Pick(w_out_e, j)
        else:
            lambda_init = 0.8 - 0.6 * math.exp(-0.3 * l)
            q, k, v = _odd_pre_call(xt, g_mix, _Pick(w_in_o, j), tabs_d)
            o = _diff_attn_call(q.reshape(b, s, -1), k.reshape(b, s, -1), v, od_lambda, od_g_head, j,
                                lambda_init)
            parts = [o.reshape(t, -1)]
            w_out = _Pick(w_out_o, j)
        xt = _mix_ffn_call(xt, parts, w_out, g_mix, _Pick(norm_g, l, 2),
                           _Pick(wg, l, 1), _Pick(wu, l, 1), _Pick(wd, l, 1))
    return xt.reshape(b, s, d)
```

```python
import functools
import math

import jax
import jax.numpy as jnp
from jax import lax
from jax.experimental import pallas as pl
from jax.experimental.pallas import tpu as pltpu

D_MODEL = 1024
DEPTH = 4
CHUNK = 64
ROPE_THETA = 500000.0
EPS = 1e-6
NEG_INF = -1e30
LOG2_E = math.log2(math.e)
TINY = 1e-30
D_FF = 2816
MLA_HEADS = 4
MLA_NOPE = 128
MLA_ROPE = 64
MLA_V = 128
MLA_Q_RANK = 384
MLA_KV_RANK = 256
HG_HEADS = 4
HG_DK = 128
HG_DV = 128
DF_HEADS = 8
DF_DH = 64
DF_ROT = DF_DH // 4

LANES = 128
MXU_DIM = 256
VMEM_LIMIT = 56 * 1024 * 1024

TOKEN_TILE = 512
MLA_Q_TILE = 512
DF_Q_TILE = 256
KEY_BLOCK = 1024
ATTN_HEADS_PER_STEP = 2
HG_SEQ_TILE = 1024
HG_SUB = 16
HG_GATE_ROWS = 128
HG_UNROLL = 4
HG_SAFE_LOG = 75.0
FFN_ROW_GROUPS = 2
PRE_ROW_GROUPS = 2
FF_CHUNKS =((0, 1024), (1024, 2048), (2048, D_FF))

MLA_QK_PAD = 2 * LANES
V_ROWS = LANES + 16
EVEN_Z = MLA_Q_RANK + MLA_KV_RANK + LANES + 4 * HG_HEADS * HG_DK

F32 = jnp.float32
BF16 = jnp.bfloat16


def _rms(x, g):
    ms = jnp.mean(x * x, axis=-1, keepdims=True)
    return x * lax.rsqrt(ms + EPS) * g


def _dot(a, b):
    return jnp.dot(a, b, preferred_element_type=F32)


def _dot_nt(a, b):
    return lax.dot_general(a, b, (((1,), (1,)), ((), ())), preferred_element_type=F32)


def _dot_tn(a, b):
    return lax.dot_general(a, b, (((0,), (0,)), ((), ())), preferred_element_type=F32)


def _rope_slab(x, c, sa, sb, half):
    return x * c + pltpu.roll(x, LANES - half, 1) * sa + pltpu.roll(x, half, 1) * sb


def _store_values(vt_ref, rs, v, heads):
    ones = jnp.ones((V_ROWS - LANES, v.shape[0]), BF16)
    for hd in range(heads):
        vt_ref[hd, :LANES, rs] = v[:, hd * LANES:(hd + 1) * LANES].T.astype(BF16)
        vt_ref[hd, LANES:, rs] = ones


def _vt_out(heads, t):
    per_blk = KEY_BLOCK // TOKEN_TILE
    spec = pl.BlockSpec((heads, None, V_ROWS, TOKEN_TILE), lambda i: (0, i // per_blk, 0, i % per_blk))
    return spec, jax.ShapeDtypeStruct((heads, t // KEY_BLOCK, V_ROWS, KEY_BLOCK), BF16)


class _Pick:
    def __init__(self, arr, *idx):
        self.arr = arr
        nd = arr.ndim - len(idx)
        self.spec = pl.BlockSpec((None,) * len(idx) + arr.shape[len(idx):],
                                 lambda *_: idx + (0,) * nd, pipeline_mode=pl.Buffered(1))


def _params(*sem):
    return pltpu.CompilerParams(dimension_semantics=sem, vmem_limit_bytes=VMEM_LIMIT)


def _ffn_apply(xs, gpre, gpost, wg_ref, wu_ref, wd_ref):
    xn = [_rms(x, gpre).astype(BF16) for x in xs]
    acc = [None] * len(xs)
    for lo, hi in FF_CHUNKS:
        for i in range(len(xs)):
            g = _dot(xn[i], wg_ref[:, lo:hi])
            u = _dot(xn[i], wu_ref[:, lo:hi])
            a = (jax.nn.silu(g) * u).astype(BF16)
            h = _dot(a, wd_ref[lo:hi, :])
            acc[i] = h if acc[i] is None else acc[i] + h
    return [x + 0.5 * _rms(a, gpost) for x, a in zip(xs, acc)]


def _row_groups(n):
    step = TOKEN_TILE // n
    return [slice(i * step, (i + 1) * step) for i in range(n)]


def _ffn_kernel(x_ref, g_ref, wg_ref, wu_ref, wd_ref, o_ref):
    g = g_ref[...]
    groups = _row_groups(FFN_ROW_GROUPS)
    outs = _ffn_apply([x_ref[rs, :] for rs in groups], g[0:1], g[1:2], wg_ref, wu_ref, wd_ref)
    for rs, o in zip(groups, outs):
        o_ref[rs, :] = o


def _ffn_call(x, g2, wg, wu, wd):
    t = x.shape[0]
    tile = pl.BlockSpec((TOKEN_TILE, D_MODEL), lambda i: (i, 0))
    picks = [g2, wg, wu, wd]
    return pl.pallas_call(
        _ffn_kernel,
        grid=(t // TOKEN_TILE,),
        in_specs=[tile] + [p.spec for p in picks],
        out_specs=tile,
        out_shape=jax.ShapeDtypeStruct(x.shape, F32),
        compiler_params=_params("parallel"),
        name="ffn",
    )(x, *[p.arr for p in picks])


def _mix_ffn_kernel(n_parts, head_scale, *refs):
    x_ref = refs[0]
    part_refs = refs[1:1 + n_parts]
    rest = refs[1 + n_parts:]
    if head_scale is not None:
        gh_ref, rest = rest[0], rest[1:]
    wo_ref, gm_ref, gf_ref, wg_ref, wu_ref, wd_ref, o_ref = rest
    gf = gf_ref[...]
    groups = _row_groups(FFN_ROW_GROUPS)
    xs = []
    for rs in groups:
        cols = []
        for p_ref in part_refs:
            if len(p_ref.shape) == 2:
                cols.append(p_ref[rs, :])
                continue
            for hd in range(p_ref.shape[0]):
                o_t = p_ref[hd, :, rs].astype(F32)
                if head_scale is not None:
                    o_t = o_t * lax.rsqrt(jnp.mean(o_t * o_t, axis=0, keepdims=True) + EPS)
                o = o_t.T
                if head_scale is not None:
                    o = o * (gh_ref[:, hd * LANES:(hd + 1) * LANES] * head_scale)
                cols.append(o.astype(BF16))
        m = _dot(jnp.concatenate(cols, axis=1), wo_ref[...])
        xs.append(x_ref[rs, :] + _rms(m, gm_ref[1:2]))
    outs = _ffn_apply(xs, gf[0:1], gf[1:2], wg_ref, wu_ref, wd_ref)
    for rs, o in zip(groups, outs):
        o_ref[rs, :] = o


def _mix_ffn_call(x, parts, wo, g_mix, g_ffn, wg, wu, wd, g_head=None, head_scale=None):
    t = x.shape[0]
    tile = pl.BlockSpec((TOKEN_TILE, D_MODEL), lambda i: (i, 0))
    part_specs = [pl.BlockSpec((TOKEN_TILE, p.shape[-1]), lambda i: (i, 0)) if p.ndim == 2
                  else pl.BlockSpec(p.shape[:2] + (TOKEN_TILE,), lambda i: (0, 0, i)) for p in parts]
    picks = ([] if g_head is None else [g_head]) + [wo, g_mix, g_ffn, wg, wu, wd]
    return pl.pallas_call(
        functools.partial(_mix_ffn_kernel, len(parts), head_scale),
        grid=(t // TOKEN_TILE,),
        in_specs=[tile] + part_specs + [p.spec for p in picks],
        out_specs=tile,
        out_shape=jax.ShapeDtypeStruct(x.shape, F32),
        compiler_params=_params("parallel"),
        name="mix_ffn",
    )(x, *parts, *[p.arr for p in picks])


def _even_pre_kernel(x_ref, g_ref, win_ref, gq_ref, wuq_ref, gkv_ref, wukv_ref,
                     c_ref, sa_ref, sb_ref, q_ref, k_ref, v_ref, zh_ref):
    o_kv = MLA_Q_RANK
    o_pe = o_kv + MLA_KV_RANK
    o_h = o_pe + LANES
    half = MLA_ROPE // 2
    scale = LOG2_E * (MLA_NOPE + MLA_ROPE) ** -0.5
    for rs in _row_groups(PRE_ROW_GROUPS):
        h = _rms(x_ref[rs, :], g_ref[0:1]).astype(BF16)
        z = _dot(h, win_ref[...])
        zh_ref[rs, :] = z[:, o_h:]
        c, sa, sb = c_ref[rs, :], sa_ref[rs, :], sb_ref[rs, :]

        cq = _rms(z[:, :o_kv], gq_ref[...]).astype(BF16)
        q = _dot(cq, wuq_ref[...]) * scale
        for hd in range(MLA_HEADS):
            base = hd * MLA_QK_PAD
            q_ref[rs, base:base + LANES] = q[:, base:base + LANES].astype(BF16)
            q_ref[rs, base + LANES:base + 2 * LANES] = _rope_slab(
                q[:, base + LANES:base + 2 * LANES], c, sa, sb, half).astype(BF16)

        ckv = _rms(z[:, o_kv:o_pe], gkv_ref[...]).astype(BF16)
        kv = _dot(ckv, wukv_ref[...])
        kpe = _rope_slab(z[:, o_pe:o_h], c, sa, sb, half).astype(BF16)
        for hd in range(MLA_HEADS):
            base = hd * MLA_QK_PAD
            k_ref[rs, base:base + LANES] = kv[:, hd * LANES:(hd + 1) * LANES].astype(BF16)
            k_ref[rs, base + LANES:base + 2 * LANES] = kpe
        _store_values(v_ref, rs, kv[:, MLA_HEADS * MLA_NOPE:], MLA_HEADS)


def _even_pre_call(x, g, win, gq, wuq, gkv, wukv, tabs):
    t = x.shape[0]
    picks = [g, win, gq, wuq, gkv, wukv]
    row = lambda w: pl.BlockSpec((TOKEN_TILE, w), lambda i: (i, 0))
    qk_w = MLA_HEADS * MLA_QK_PAD
    vt_spec, vt_shape = _vt_out(MLA_HEADS, t)
    zh_w = 4 * HG_HEADS * HG_DK
    return pl.pallas_call(
        _even_pre_kernel,
        grid=(t // TOKEN_TILE,),
        in_specs=[row(D_MODEL)] + [p.spec for p in picks] + [row(LANES), row(LANES), row(LANES)],
        out_specs=[row(qk_w), row(qk_w), vt_spec, row(zh_w)],
        out_shape=[jax.ShapeDtypeStruct((t, qk_w), BF16), jax.ShapeDtypeStruct((t, qk_w), BF16),
                   vt_shape, jax.ShapeDtypeStruct((t, zh_w), F32)],
        compiler_params=_params("parallel"),
        name="even_pre",
    )(x, *[p.arr for p in picks], *tabs)


def _odd_pre_kernel(x_ref, g_ref, win_ref, c_ref, sa_ref, sb_ref, q_ref, k_ref, v_ref):
    half = DF_ROT // 2
    width = DF_HEADS * 2 * DF_DH
    scale = LOG2_E * DF_DH ** -0.5
    for rs in _row_groups(PRE_ROW_GROUPS):
        h = _rms(x_ref[rs, :], g_ref[0:1]).astype(BF16)
        c, sa, sb = c_ref[rs, :], sa_ref[rs, :], sb_ref[rs, :]
        q = _dot(h, win_ref[:, :width]) * scale
        k = _dot(h, win_ref[:, width:2 * width])
        for j in range(width // LANES):
            sl = slice(j * LANES, (j + 1) * LANES)
            q_ref[rs, sl] = _rope_slab(q[:, sl], c, sa, sb, half).astype(BF16)
            k_ref[rs, sl] = _rope_slab(k[:, sl], c, sa, sb, half).astype(BF16)
        _store_values(v_ref, rs, _dot(h, win_ref[:, 2 * width:]), DF_HEADS)


def _odd_pre_call(x, g, win, tabs):
    t = x.shape[0]
    width = DF_HEADS * 2 * DF_DH
    row = lambda w: pl.BlockSpec((TOKEN_TILE, w), lambda i: (i, 0))
    vt_spec, vt_shape = _vt_out(DF_HEADS, t)
    return pl.pallas_call(
        _odd_pre_kernel,
        grid=(t // TOKEN_TILE,),
        in_specs=[row(D_MODEL), g.spec, win.spec, row(LANES), row(LANES), row(LANES)],
        out_specs=[row(width), row(width), vt_spec],
        out_shape=[jax.ShapeDtypeStruct((t, width), BF16), jax.ShapeDtypeStruct((t, width), BF16), vt_shape],
        compiler_params=_params("parallel"),
        name="odd_pre",
    )(x, g.arr, win.arr, *tabs)


def _flash(streams, q_chunk, k_ref, vt_ref, m_ref, acc_ref, row0, tq):
    m_rows = streams[0][0].shape[0]
    for i in range(len(streams)):
        m_ref[i] = jnp.full((1, m_rows), NEG_INF, F32)
        acc_ref[i] = jnp.zeros((V_ROWS, m_rows), F32)

    def step(blk, width, masked):
        start = pl.multiple_of(blk * KEY_BLOCK, KEY_BLOCK)
        scores = [_dot_nt(k_ref[0, pl.ds(start, width), ksl], q) for q, ksl, _ in streams]
        for i, (s, (_, _, hd)) in enumerate(zip(scores, streams)):
            if masked:
                k_chunk = (start + (width - tq) + lax.broadcasted_iota(jnp.int32, (tq, 1), 0)) // CHUNK
                tail = jnp.where(k_chunk <= q_chunk, s[width - tq:], NEG_INF)
                s = tail if width == tq else jnp.concatenate([s[:width - tq], tail], axis=0)
            m_i = m_ref[i]
            m_new = jnp.maximum(m_i, jnp.max(s, axis=0, keepdims=True))
            m_ref[i] = m_new
            p = jnp.exp2((s - m_new).astype(BF16))
            acc_ref[i] = jnp.exp2(m_i - m_new) * acc_ref[i] + _dot(vt_ref[hd, blk, :, :width], p)

    n_full = row0 // KEY_BLOCK

    def unmasked(blk, carry):
        step(blk, KEY_BLOCK, False)
        return carry

    lax.fori_loop(0, n_full, unmasked, 0)
    for p in range(KEY_BLOCK // tq):
        pl.when((row0 % KEY_BLOCK) // tq == p)(functools.partial(step, n_full, tq * (p + 1), True))
    outs = []
    for i in range(len(streams)):
        acc = acc_ref[i]
        outs.append(acc[:LANES] * (1.0 / acc[LANES:LANES + 1]))
    return outs


def _row_chunks(row0, n_rows):
    rows = row0 + lax.broadcasted_iota(jnp.int32, (1, n_rows), 1)
    return rows // CHUNK


def _flash_scratch(hp, m_rows):
    return [pltpu.VMEM((hp, 1, m_rows), F32), pltpu.VMEM((hp, V_ROWS, m_rows), F32)]


def _ot_spec(hp, tq, s):
    return pl.BlockSpec((hp, LANES, tq), lambda bi, h, i: (h, 0, bi * (s // tq) + i))


def _vt_spec(hp, s):
    return pl.BlockSpec((hp, s // KEY_BLOCK, V_ROWS, KEY_BLOCK), lambda bi, h, i: (h, bi, 0, 0))


def _mla_attn_kernel(q_ref, k_ref, vt_ref, o_ref, m_ref, acc_ref):
    row0 = pl.program_id(2) * MLA_Q_TILE
    streams = [(q_ref[0, :, h * MLA_QK_PAD:(h + 1) * MLA_QK_PAD],
                slice(h * MLA_QK_PAD, (h + 1) * MLA_QK_PAD), h)
               for h in range(ATTN_HEADS_PER_STEP)]
    outs = _flash(streams, _row_chunks(row0, MLA_Q_TILE), k_ref, vt_ref, m_ref, acc_ref, row0, MLA_Q_TILE)
    for h, o_t in enumerate(outs):
        o_ref[h] = o_t.astype(BF16)


def _mla_attn_call(q, k, vt):
    b, s, _ = q.shape
    hp = ATTN_HEADS_PER_STEP
    return pl.pallas_call(
        _mla_attn_kernel,
        grid=(b, MLA_HEADS // hp, s // MLA_Q_TILE),
        in_specs=[pl.BlockSpec((1, MLA_Q_TILE, hp * MLA_QK_PAD), lambda bi, h, i: (bi, i, h)),
                  pl.BlockSpec((1, s, hp * MLA_QK_PAD), lambda bi, h, i: (bi, 0, h)),
                  _vt_spec(hp, s)],
        out_specs=_ot_spec(hp, MLA_Q_TILE, s),
        out_shape=jax.ShapeDtypeStruct((MLA_HEADS, MLA_V, b * s), BF16),
        scratch_shapes=_flash_scratch(hp, MLA_Q_TILE),
        compiler_params=_params("parallel", "parallel", "arbitrary"),
        name="mla_attn",
    )(q, k, vt)


def _diff_attn_kernel(lambda_init, q_ref, k_ref, vt_ref, lam_ref, o_ref, m_ref, acc_ref):
    row0 = pl.program_id(2) * DF_Q_TILE
    dv = 2 * DF_DH
    lane = lax.broadcasted_iota(jnp.int32, (1, dv), 1)
    streams = []
    for h in range(ATTN_HEADS_PER_STEP):
        hs = slice(h * dv, (h + 1) * dv)
        q = q_ref[0, :, hs]
        zero = jnp.zeros_like(q)
        q2 = jnp.concatenate([jnp.where(lane < DF_DH, q, zero), jnp.where(lane >= DF_DH, q, zero)], axis=0)
        streams.append((q2, hs, h))
    chunks = _row_chunks(row0, DF_Q_TILE)
    outs = _flash(streams, jnp.concatenate([chunks, chunks], axis=1), k_ref, vt_ref, m_ref, acc_ref, row0,
                  DF_Q_TILE)
    lp = lam_ref[...]
    lam = (jnp.exp(jnp.sum(lp[0:1] * lp[1:2], axis=-1, keepdims=True))
           - jnp.exp(jnp.sum(lp[2:3] * lp[3:4], axis=-1, keepdims=True)) + lambda_init)
    for h, o2_t in enumerate(outs):
        o_ref[h] = (o2_t[:, :DF_Q_TILE] - lam * o2_t[:, DF_Q_TILE:]).astype(BF16)


def _diff_attn_call(q, k, vt, lam_p, lambda_init):
    b, s, _ = q.shape
    hp = ATTN_HEADS_PER_STEP
    w = hp * 2 * DF_DH
    return pl.pallas_call(
        functools.partial(_diff_attn_kernel, lambda_init),
        grid=(b, DF_HEADS // hp, s // DF_Q_TILE),
        in_specs=[pl.BlockSpec((1, DF_Q_TILE, w), lambda bi, h, i: (bi, i, h)),
                  pl.BlockSpec((1, s, w), lambda bi, h, i: (bi, 0, h)),
                  _vt_spec(hp, s),
                  lam_p.spec],
        out_specs=_ot_spec(hp, DF_Q_TILE, s),
        out_shape=jax.ShapeDtypeStruct((DF_HEADS, 2 * DF_DH, b * s), BF16),
        scratch_shapes=_flash_scratch(hp, 2 * DF_Q_TILE),
        compiler_params=_params("parallel", "parallel", "arbitrary"),
        name="diff_attn",
    )(q, k, vt, lam_p.arr)


def _hgrn_kernel(hq_ref, hf_ref, hi_ref, hg_ref, lb_ref, go_ref, o_ref,
                 st_ref, b_ref, kk_ref, bpad_ref, kpad_ref, vpad_ref):
    n_chunks = HG_SEQ_TILE // CHUNK
    mid = CHUNK // 2 - 1

    @pl.when(pl.program_id(1) == 0)
    def _():
        st_ref[...] = jnp.zeros_like(st_ref)

    row_in_chunk = lax.broadcasted_iota(jnp.int32, (HG_GATE_ROWS, 1), 0) % CHUNK
    for hd in range(HG_HEADS):
        ls = slice(hd * HG_DK, (hd + 1) * HG_DK)
        lb = lb_ref[:, ls]
        for r0 in range(0, HG_SEQ_TILE, HG_GATE_ROWS):
            rs = slice(r0, r0 + HG_GATE_ROWS)
            sig = jax.nn.sigmoid(hf_ref[rs, ls])
            kk_ref[hd, rs, :] = (1.0 - lb) * (1.0 - sig)
            b = jnp.log(jnp.maximum(lb + (1.0 - lb) * sig, TINY))
            step = 1
            while step < CHUNK:
                b = b + jnp.where(row_in_chunk >= step, pltpu.roll(b, step, 0), 0.0)
                step *= 2
            b_ref[hd, rs, :] = b

    b_mid = b_ref[:, pl.ds(mid, n_chunks, stride=CHUNK), :]
    b_end = b_ref[:, pl.ds(CHUNK - 1, n_chunks, stride=CHUNK), :]
    worst = jnp.max(jnp.maximum(-b_mid, b_mid - b_end))

    row = lax.broadcasted_iota(jnp.int32, (CHUNK, 1), 0)

    def load(ci, hd):
        rows = pl.ds(pl.multiple_of(ci * CHUNK, CHUNK), CHUNK)
        ls = slice(hd * HG_DK, (hd + 1) * HG_DK)
        return rows, ls, hq_ref[rows, ls], hi_ref[rows, ls], b_ref[hd, rows, :], kk_ref[hd, rows, :]

    def carry_state(hd, qh, vh, b, kk):
        st = st_ref[hd]
        o = _dot_nt((qh * jnp.exp(b)).astype(BF16), st.astype(BF16))
        b_last = b[CHUNK - 1:CHUNK]
        kdec = kk * jnp.exp(b_last - b)
        st_ref[hd] = st * jnp.exp(b_last) + _dot_tn(vh.astype(BF16), kdec.astype(BF16))
        return o

    def finish(rows, ls, o):
        on = _rms(o, go_ref[:, ls]) * jax.nn.silu(hg_ref[rows, ls])
        o_ref[rows, ls] = on.astype(BF16)

    @pl.when(worst < HG_SAFE_LOG)
    def _():
        causal = row >= lax.broadcasted_iota(jnp.int32, (1, CHUNK), 1)

        def chunk_body(cj, carry):
            for u in range(HG_UNROLL):
                ci = cj * HG_UNROLL + u
                for hd in range(HG_HEADS):
                    rows, ls, qh, vh, b, kk = load(ci, hd)
                    o = carry_state(hd, qh, vh, b, kk)
                    b_m = b[mid:mid + 1]
                    qf = (qh * jnp.exp(b - b_m)).astype(BF16)
                    kf = (kk * jnp.exp(b_m - b)).astype(BF16)
                    a = jnp.where(causal, _dot_nt(qf, kf), 0.0)
                    finish(rows, ls, o + _dot(a.astype(BF16), vh.astype(BF16)))
            return carry

        lax.fori_loop(0, n_chunks // HG_UNROLL, chunk_body, 0)

    @pl.when(jnp.logical_not(worst < HG_SAFE_LOG))
    def _():
        n_sub = CHUNK // HG_SUB
        off_w = HG_SUB * (n_sub * (n_sub - 1) // 2)
        zpad = jnp.zeros((HG_HEADS, HG_SUB, HG_DK), F32)
        bpad_ref[:, :HG_SUB, :] = zpad
        kpad_ref[:, :HG_SUB, :] = zpad
        vpad_ref[:, :HG_SUB, :] = zpad
        row_sub = row % HG_SUB
        r2 = lax.broadcasted_iota(jnp.int32, (2 * HG_DK, 2 * HG_DK), 0) // HG_DK
        c2 = lax.broadcasted_iota(jnp.int32, (2 * HG_DK, 2 * HG_DK), 1) // HG_DK
        ones2 = jnp.where(r2 == c2, 1.0, 0.0).astype(BF16)
        col = lax.broadcasted_iota(jnp.int32, (1, off_w), 1)
        col_blk = jnp.zeros((1, off_w), jnp.int32)
        for i in range(1, n_sub):
            col_blk = col_blk + jnp.where(col >= HG_SUB * (i * (i - 1) // 2), 1, 0)
        off_mask = col_blk == (row // HG_SUB)

        def chunk_body(ci, carry):
            for hd in range(HG_HEADS):
                rows, ls, qh, vh, b, kk = load(ci, hd)
                o = carry_state(hd, qh, vh, b, kk)

                refs = [b[i * HG_SUB - 1:i * HG_SUB] for i in range(1, n_sub)]
                bref = jnp.concatenate(
                    [jnp.zeros((HG_SUB, HG_DK), F32)]
                    + [jnp.broadcast_to(r, (HG_SUB, HG_DK)) for r in refs], axis=0)
                qs = qh * jnp.exp(b - bref)
                kst = jnp.concatenate(
                    [kk[:i * HG_SUB] * jnp.exp(refs[i - 1] - b[:i * HG_SUB]) for i in range(1, n_sub)],
                    axis=0)
                vst = jnp.concatenate([vh[:i * HG_SUB] for i in range(1, n_sub)], axis=0)
                a_off = jnp.where(off_mask, _dot_nt(qs.astype(BF16), kst.astype(BF16)), 0.0)
                o = o + _dot(a_off.astype(BF16), vst.astype(BF16))

                bpad_ref[hd, HG_SUB:, :] = b
                kpad_ref[hd, HG_SUB:, :] = kk
                vpad_ref[hd, HG_SUB:, :] = vh
                for dp in range(HG_SUB // 2):
                    terms = []
                    for d in (2 * dp, 2 * dp + 1):
                        lo = HG_SUB - d
                        w = qh * jnp.exp(b - bpad_ref[hd, lo:lo + CHUNK, :]) * kpad_ref[hd, lo:lo + CHUNK, :]
                        terms.append(jnp.where(row_sub >= d, w, 0.0))
                    dsum = _dot(jnp.concatenate(terms, axis=1).astype(BF16), ones2)
                    for j, d in enumerate((2 * dp, 2 * dp + 1)):
                        lo = HG_SUB - d
                        o = o + dsum[:, j * HG_DK:(j + 1) * HG_DK] * vpad_ref[hd, lo:lo + CHUNK, :]
                finish(rows, ls, o)
            return carry

        lax.fori_loop(0, n_chunks, chunk_body, 0)


def _hgrn_call(zh, lb, g_out, b, s):
    width = HG_HEADS * HG_DK
    n_seq = s // HG_SEQ_TILE
    part = lambda j: pl.BlockSpec((HG_SEQ_TILE, width), lambda bi, si: (bi * n_seq + si, j))
    pad = pltpu.VMEM((HG_HEADS, HG_SUB + CHUNK, HG_DK), F32)
    return pl.pallas_call(
        _hgrn_kernel,
        grid=(b, n_seq),
        in_specs=[part(0), part(1), part(2), part(3), lb.spec, g_out.spec],
        out_specs=pl.BlockSpec((HG_SEQ_TILE, width), lambda bi, si: (bi * n_seq + si, 0)),
        out_shape=jax.ShapeDtypeStruct((b * s, width), BF16),
        scratch_shapes=[pltpu.VMEM((HG_HEADS, HG_DV, HG_DK), F32),
                        pltpu.VMEM((HG_HEADS, HG_SEQ_TILE, HG_DK), F32),
                        pltpu.VMEM((HG_HEADS, HG_SEQ_TILE, HG_DK), F32),
                        pad, pad, pad],
        compiler_params=_params("parallel", "arbitrary"),
        name="hgrn2",
    )(zh, zh, zh, zh, lb.arr, g_out.arr)


def _rope_tables(positions, dim, group):
    half = dim // 2
    inv_freq = ROPE_THETA ** (-jnp.arange(0, dim, 2, dtype=F32) / dim)
    ang = positions.astype(F32).reshape(-1, 1) * inv_freq
    cos, sin = jnp.cos(ang), jnp.sin(ang)
    t = ang.shape[0]
    rest = group - dim
    c = jnp.concatenate([cos, cos, jnp.ones((t, rest), F32)], axis=1)
    sa = jnp.concatenate([-sin, jnp.zeros((t, half + rest), F32)], axis=1)
    sb = jnp.concatenate([jnp.zeros((t, half), F32), sin, jnp.zeros((t, rest), F32)], axis=1)
    rep = LANES // group
    return tuple(jnp.tile(a, (1, rep)) for a in (c, sa, sb))


def kernel(x, positions, norm_g, ffn_w_gate, ffn_w_up, ffn_w_down, ev_w_in, ev_g_q, ev_w_uq, ev_g_kv, ev_w_ukv, ev_lb_logits, ev_g_out, ev_w_out, od_w_in, od_lambda, od_g_head, od_w_out):
    b, s, d = x.shape
    t = b * s
    xt = x.reshape(t, d)
    tabs_m = _rope_tables(positions, MLA_ROPE, LANES)
    tabs_d = _rope_tables(positions, DF_ROT, DF_DH)
    lb_w = jax.nn.softmax(ev_lb_logits.astype(F32), axis=0)
    lb_all = jnp.cumsum(lb_w, axis=0) - lb_w[0:1]

    wg = ffn_w_gate.astype(BF16)
    wu = ffn_w_up.astype(BF16)
    wd = ffn_w_down.astype(BF16)
    n_even = ev_w_in.shape[0]
    o_pe = MLA_Q_RANK + MLA_KV_RANK + MLA_ROPE
    w_in_e = jnp.concatenate(
        [ev_w_in[..., :o_pe].astype(BF16), jnp.zeros((n_even, d, LANES - MLA_ROPE), BF16),
         ev_w_in[..., o_pe:].astype(BF16)], axis=-1)
    w_uq = ev_w_uq.astype(BF16).reshape(n_even, MLA_Q_RANK, MLA_HEADS, MLA_NOPE + MLA_ROPE)
    w_uq = jnp.pad(w_uq, ((0, 0), (0, 0), (0, 0), (0, MLA_QK_PAD - MLA_NOPE - MLA_ROPE)))
    w_uq = w_uq.reshape(n_even, MLA_Q_RANK, MLA_HEADS * MLA_QK_PAD)
    w_ukv = ev_w_ukv.astype(BF16).reshape(n_even, MLA_KV_RANK, MLA_HEADS, 2, MLA_NOPE)
    w_ukv = w_ukv.transpose(0, 1, 3, 2, 4).reshape(n_even, MLA_KV_RANK, 2 * MLA_HEADS * MLA_NOPE)
    w_out_e = ev_w_out.astype(BF16)
    w_in_o = od_w_in.astype(BF16)
    w_out_o = od_w_out.astype(BF16)
    g_q = ev_g_q.reshape(n_even, 1, -1)
    g_kv = ev_g_kv.reshape(n_even, 1, -1)
    g_out = ev_g_out.reshape(n_even, 1, -1)
    lb_all = lb_all.reshape(n_even, 1, -1)
    g_head_o = od_g_head.reshape(od_g_head.shape[0], 1, -1)

    for l in range(DEPTH):
        xt = _ffn_call(xt, _Pick(norm_g, l, 0), _Pick(wg, l, 0), _Pick(wu, l, 0), _Pick(wd, l, 0))
        j = l // 2
        g_mix = _Pick(norm_g, l, 1)
        if l % 2 == 0:
            q, k, v, zh = _even_pre_call(
                xt, g_mix, _Pick(w_in_e, j), _Pick(g_q, j), _Pick(w_uq, j), _Pick(g_kv, j),
                _Pick(w_ukv, j), tabs_m)
            o_a = _mla_attn_call(q.reshape(b, s, -1), k.reshape(b, s, -1), v)
            o_b = _hgrn_call(zh, _Pick(lb_all, j), _Pick(g_out, j), b, s)
            parts = [o_a, o_b]
            head_norm = {}
            w_out = _Pick(w_out_e, j)
        else:
            lambda_init = 0.8 - 0.6 * math.exp(-0.3 * l)
            q, k, v = _odd_pre_call(xt, g_mix, _Pick(w_in_o, j), tabs_d)
            o = _diff_attn_call(q.reshape(b, s, -1), k.reshape(b, s, -1), v, _Pick(od_lambda, j),
                                lambda_init)
            parts = [o]
            head_norm = dict(g_head=_Pick(g_head_o, j), head_scale=1.0 - lambda_init)
            w_out = _Pick(w_out_o, j)
        xt = _mix_ffn_call(xt, parts, w_out, g_mix, _Pick(norm_g, l, 2),
                           _Pick(wg, l, 1), _Pick(wu, l, 1), _Pick(wd, l, 1), **head_norm)
    return xt.reshape(b, s, d)
```

```python
import functools
import math

import jax
import jax.numpy as jnp
from jax import lax
from jax.experimental import pallas as pl
from jax.experimental.pallas import tpu as pltpu

D_MODEL = 1024
DEPTH = 4
CHUNK = 64
ROPE_THETA = 500000.0
EPS = 1e-6
NEG_INF = -1e30
LOG2_E = math.log2(math.e)
TINY = 1e-30
D_FF = 2816
MLA_HEADS = 4
MLA_NOPE = 128
MLA_ROPE = 64
MLA_V = 128
MLA_Q_RANK = 384
MLA_KV_RANK = 256
HG_HEADS = 4
HG_DK = 128
HG_DV = 128
DF_HEADS = 8
DF_DH = 64
DF_ROT = DF_DH // 4

LANES = 128
MXU_DIM = 256
VMEM_LIMIT = 56 * 1024 * 1024

TOKEN_TILE = 512
MLA_Q_TILE = 512
DF_Q_TILE = 256
KEY_BLOCK = 1024
ATTN_HEADS_PER_STEP = 4
HG_SEQ_TILE = 1024
HG_SUB = 16
HG_GATE_ROWS = 128
HG_UNROLL = 4
HG_SAFE_LOG = 75.0
FFN_ROW_GROUPS = 2
PRE_ROW_GROUPS = 2
FF_CHUNKS =((0, 1024), (1024, 2048), (2048, D_FF))

MLA_QK_PAD = 2 * LANES
V_ROWS = LANES + 16
EVEN_Z = MLA_Q_RANK + MLA_KV_RANK + LANES + 4 * HG_HEADS * HG_DK

F32 = jnp.float32
BF16 = jnp.bfloat16


def _rms(x, g):
    ms = jnp.mean(x * x, axis=-1, keepdims=True)
    return x * lax.rsqrt(ms + EPS) * g


def _dot(a, b):
    return jnp.dot(a, b, preferred_element_type=F32)


def _dot_nt(a, b):
    return lax.dot_general(a, b, (((1,), (1,)), ((), ())), preferred_element_type=F32)


def _dot_tn(a, b):
    return lax.dot_general(a, b, (((0,), (0,)), ((), ())), preferred_element_type=F32)


def _rope_slab(x, c, sa, sb, half):
    return x * c + pltpu.roll(x, LANES - half, 1) * sa + pltpu.roll(x, half, 1) * sb


def _store_values(vt_ref, rs, v, heads):
    ones = jnp.ones((V_ROWS - LANES, v.shape[0]), BF16)
    for hd in range(heads):
        vt_ref[hd, :LANES, rs] = v[:, hd * LANES:(hd + 1) * LANES].T.astype(BF16)
        vt_ref[hd, LANES:, rs] = ones


def _vt_out(heads, t):
    per_blk = KEY_BLOCK // TOKEN_TILE
    spec = pl.BlockSpec((heads, None, V_ROWS, TOKEN_TILE), lambda i: (0, i // per_blk, 0, i % per_blk))
    return spec, jax.ShapeDtypeStruct((heads, t // KEY_BLOCK, V_ROWS, KEY_BLOCK), BF16)


class _Pick:
    def __init__(self, arr, *idx):
        self.arr = arr
        nd = arr.ndim - len(idx)
        self.spec = pl.BlockSpec((None,) * len(idx) + arr.shape[len(idx):],
                                 lambda *_: idx + (0,) * nd, pipeline_mode=pl.Buffered(1))


def _params(*sem):
    return pltpu.CompilerParams(dimension_semantics=sem, vmem_limit_bytes=VMEM_LIMIT)


def _ffn_apply(xs, gpre, gpost, wg_ref, wu_ref, wd_ref):
    xn = [_rms(x, gpre).astype(BF16) for x in xs]
    acc = [None] * len(xs)
    for lo, hi in FF_CHUNKS:
        for i in range(len(xs)):
            g = _dot(xn[i], wg_ref[:, lo:hi])
            u = _dot(xn[i], wu_ref[:, lo:hi])
            a = (jax.nn.silu(g) * u).astype(BF16)
            h = _dot(a, wd_ref[lo:hi, :])
            acc[i] = h if acc[i] is None else acc[i] + h
    return [x + 0.5 * _rms(a, gpost) for x, a in zip(xs, acc)]


def _row_groups(n):
    step = TOKEN_TILE // n
    return [slice(i * step, (i + 1) * step) for i in range(n)]


def _ffn_kernel(x_ref, g_ref, wg_ref, wu_ref, wd_ref, o_ref):
    g = g_ref[...]
    groups = _row_groups(FFN_ROW_GROUPS)
    outs = _ffn_apply([x_ref[rs, :] for rs in groups], g[0:1], g[1:2], wg_ref, wu_ref, wd_ref)
    for rs, o in zip(groups, outs):
        o_ref[rs, :] = o


def _ffn_call(x, g2, wg, wu, wd):
    t = x.shape[0]
    tile = pl.BlockSpec((TOKEN_TILE, D_MODEL), lambda i: (i, 0))
    picks = [g2, wg, wu, wd]
    return pl.pallas_call(
        _ffn_kernel,
        grid=(t // TOKEN_TILE,),
        in_specs=[tile] + [p.spec for p in picks],
        out_specs=tile,
        out_shape=jax.ShapeDtypeStruct(x.shape, F32),
        compiler_params=_params("parallel"),
        name="ffn",
    )(x, *[p.arr for p in picks])


def _mix_ffn_kernel(n_parts, head_scale, *refs):
    x_ref = refs[0]
    part_refs = refs[1:1 + n_parts]
    rest = refs[1 + n_parts:]
    if head_scale is not None:
        gh_ref, rest = rest[0], rest[1:]
    wo_ref, gm_ref, gf_ref, wg_ref, wu_ref, wd_ref, o_ref = rest
    gf = gf_ref[...]
    groups = _row_groups(FFN_ROW_GROUPS)
    xs = []
    for rs in groups:
        cols = []
        for p_ref in part_refs:
            if len(p_ref.shape) == 2:
                cols.append(p_ref[rs, :])
                continue
            for hd in range(p_ref.shape[0]):
                o_t = p_ref[hd, :, rs].astype(F32)
                if head_scale is not None:
                    o_t = o_t * lax.rsqrt(jnp.mean(o_t * o_t, axis=0, keepdims=True) + EPS)
                o = o_t.T
                if head_scale is not None:
                    o = o * (gh_ref[:, hd * LANES:(hd + 1) * LANES] * head_scale)
                cols.append(o.astype(BF16))
        m = _dot(jnp.concatenate(cols, axis=1), wo_ref[...])
        xs.append(x_ref[rs, :] + _rms(m, gm_ref[1:2]))
    outs = _ffn_apply(xs, gf[0:1], gf[1:2], wg_ref, wu_ref, wd_ref)
    for rs, o in zip(groups, outs):
        o_ref[rs, :] = o


def _mix_ffn_call(x, parts, wo, g_mix, g_ffn, wg, wu, wd, g_head=None, head_scale=None):
    t = x.shape[0]
    tile = pl.BlockSpec((TOKEN_TILE, D_MODEL), lambda i: (i, 0))
    part_specs = [pl.BlockSpec((TOKEN_TILE, p.shape[-1]), lambda i: (i, 0)) if p.ndim == 2
                  else pl.BlockSpec(p.shape[:2] + (TOKEN_TILE,), lambda i: (0, 0, i)) for p in parts]
    picks = ([] if g_head is None else [g_head]) + [wo, g_mix, g_ffn, wg, wu, wd]
    return pl.pallas_call(
        functools.partial(_mix_ffn_kernel, len(parts), head_scale),
        grid=(t // TOKEN_TILE,),
        in_specs=[tile] + part_specs + [p.spec for p in picks],
        out_specs=tile,
        out_shape=jax.ShapeDtypeStruct(x.shape, F32),
        compiler_params=_params("parallel"),
        name="mix_ffn",
    )(x, *parts, *[p.arr for p in picks])


def _even_pre_kernel(x_ref, g_ref, win_ref, gq_ref, wuq_ref, gkv_ref, wukv_ref,
                     c_ref, sa_ref, sb_ref, q_ref, k_ref, v_ref, zh_ref):
    o_kv = MLA_Q_RANK
    o_pe = o_kv + MLA_KV_RANK
    o_h = o_pe + LANES
    half = MLA_ROPE // 2
    scale = LOG2_E * (MLA_NOPE + MLA_ROPE) ** -0.5
    for rs in _row_groups(PRE_ROW_GROUPS):
        h = _rms(x_ref[rs, :], g_ref[0:1]).astype(BF16)
        z = _dot(h, win_ref[...])
        zh_ref[rs, :] = z[:, o_h:]
        c, sa, sb = c_ref[rs, :], sa_ref[rs, :], sb_ref[rs, :]

        cq = _rms(z[:, :o_kv], gq_ref[...]).astype(BF16)
        q = _dot(cq, wuq_ref[...]) * scale
        for hd in range(MLA_HEADS):
            base = hd * MLA_QK_PAD
            q_ref[rs, base:base + LANES] = q[:, base:base + LANES].astype(BF16)
            q_ref[rs, base + LANES:base + 2 * LANES] = _rope_slab(
                q[:, base + LANES:base + 2 * LANES], c, sa, sb, half).astype(BF16)

        ckv = _rms(z[:, o_kv:o_pe], gkv_ref[...]).astype(BF16)
        kv = _dot(ckv, wukv_ref[...])
        kpe = _rope_slab(z[:, o_pe:o_h], c, sa, sb, half).astype(BF16)
        for hd in range(MLA_HEADS):
            base = hd * MLA_QK_PAD
            k_ref[rs, base:base + LANES] = kv[:, hd * LANES:(hd + 1) * LANES].astype(BF16)
            k_ref[rs, base + LANES:base + 2 * LANES] = kpe
        _store_values(v_ref, rs, kv[:, MLA_HEADS * MLA_NOPE:], MLA_HEADS)


def _even_pre_call(x, g, win, gq, wuq, gkv, wukv, tabs):
    t = x.shape[0]
    picks = [g, win, gq, wuq, gkv, wukv]
    row = lambda w: pl.BlockSpec((TOKEN_TILE, w), lambda i: (i, 0))
    qk_w = MLA_HEADS * MLA_QK_PAD
    vt_spec, vt_shape = _vt_out(MLA_HEADS, t)
    zh_w = 4 * HG_HEADS * HG_DK
    return pl.pallas_call(
        _even_pre_kernel,
        grid=(t // TOKEN_TILE,),
        in_specs=[row(D_MODEL)] + [p.spec for p in picks] + [row(LANES), row(LANES), row(LANES)],
        out_specs=[row(qk_w), row(qk_w), vt_spec, row(zh_w)],
        out_shape=[jax.ShapeDtypeStruct((t, qk_w), BF16), jax.ShapeDtypeStruct((t, qk_w), BF16),
                   vt_shape, jax.ShapeDtypeStruct((t, zh_w), F32)],
        compiler_params=_params("parallel"),
        name="even_pre",
    )(x, *[p.arr for p in picks], *tabs)


def _odd_pre_kernel(x_ref, g_ref, win_ref, c_ref, sa_ref, sb_ref, q_ref, k_ref, v_ref):
    half = DF_ROT // 2
    width = DF_HEADS * 2 * DF_DH
    scale = LOG2_E * DF_DH ** -0.5
    for rs in _row_groups(PRE_ROW_GROUPS):
        h = _rms(x_ref[rs, :], g_ref[0:1]).astype(BF16)
        c, sa, sb = c_ref[rs, :], sa_ref[rs, :], sb_ref[rs, :]
        q = _dot(h, win_ref[:, :width]) * scale
        k = _dot(h, win_ref[:, width:2 * width])
        for j in range(width // LANES):
            sl = slice(j * LANES, (j + 1) * LANES)
            q_ref[rs, sl] = _rope_slab(q[:, sl], c, sa, sb, half).astype(BF16)
            k_ref[rs, sl] = _rope_slab(k[:, sl], c, sa, sb, half).astype(BF16)
        _store_values(v_ref, rs, _dot(h, win_ref[:, 2 * width:]), DF_HEADS)


def _odd_pre_call(x, g, win, tabs):
    t = x.shape[0]
    width = DF_HEADS * 2 * DF_DH
    row = lambda w: pl.BlockSpec((TOKEN_TILE, w), lambda i: (i, 0))
    vt_spec, vt_shape = _vt_out(DF_HEADS, t)
    return pl.pallas_call(
        _odd_pre_kernel,
        grid=(t // TOKEN_TILE,),
        in_specs=[row(D_MODEL), g.spec, win.spec, row(LANES), row(LANES), row(LANES)],
        out_specs=[row(width), row(width), vt_spec],
        out_shape=[jax.ShapeDtypeStruct((t, width), BF16), jax.ShapeDtypeStruct((t, width), BF16), vt_shape],
        compiler_params=_params("parallel"),
        name="odd_pre",
    )(x, g.arr, win.arr, *tabs)


def _flash(streams, q_chunk, k_ref, vt_ref, m_ref, acc_ref, row0, tq):
    m_rows = streams[0][0].shape[0]
    for i in range(len(streams)):
        m_ref[i] = jnp.full((1, m_rows), NEG_INF, F32)
        acc_ref[i] = jnp.zeros((V_ROWS, m_rows), F32)

    def step(blk, width, masked):
        start = pl.multiple_of(blk * KEY_BLOCK, KEY_BLOCK)
        scores = [_dot_nt(k_ref[0, pl.ds(start, width), ksl], q) for q, ksl, _ in streams]
        for i, (s, (_, _, hd)) in enumerate(zip(scores, streams)):
            if masked:
                k_chunk = (start + (width - tq) + lax.broadcasted_iota(jnp.int32, (tq, 1), 0)) // CHUNK
                tail = jnp.where(k_chunk <= q_chunk, s[width - tq:], NEG_INF)
                s = tail if width == tq else jnp.concatenate([s[:width - tq], tail], axis=0)
            m_i = m_ref[i]
            m_new = jnp.maximum(m_i, jnp.max(s, axis=0, keepdims=True))
            m_ref[i] = m_new
            p = jnp.exp2((s - m_new).astype(BF16))
            acc_ref[i] = jnp.exp2(m_i - m_new) * acc_ref[i] + _dot(vt_ref[hd, blk, :, :width], p)

    n_full = row0 // KEY_BLOCK

    def unmasked(blk, carry):
        step(blk, KEY_BLOCK, False)
        return carry

    lax.fori_loop(0, n_full, unmasked, 0)
    for p in range(KEY_BLOCK // tq):
        pl.when((row0 % KEY_BLOCK) // tq == p)(functools.partial(step, n_full, tq * (p + 1), True))
    outs = []
    for i in range(len(streams)):
        acc = acc_ref[i]
        outs.append(acc[:LANES] * (1.0 / acc[LANES:LANES + 1]))
    return outs


def _row_chunks(row0, n_rows):
    rows = row0 + lax.broadcasted_iota(jnp.int32, (1, n_rows), 1)
    return rows // CHUNK


def _flash_scratch(hp, m_rows):
    return [pltpu.VMEM((hp, 1, m_rows), F32), pltpu.VMEM((hp, V_ROWS, m_rows), F32)]


def _ot_spec(hp, tq, s):
    return pl.BlockSpec((hp, LANES, tq), lambda bi, h, i: (h, 0, bi * (s // tq) + i))


def _vt_spec(hp, s):
    return pl.BlockSpec((hp, s // KEY_BLOCK, V_ROWS, KEY_BLOCK), lambda bi, h, i: (h, bi, 0, 0))


def _mla_attn_kernel(q_ref, k_ref, vt_ref, o_ref, m_ref, acc_ref):
    row0 = pl.program_id(2) * MLA_Q_TILE
    streams = [(q_ref[0, :, h * MLA_QK_PAD:(h + 1) * MLA_QK_PAD],
                slice(h * MLA_QK_PAD, (h + 1) * MLA_QK_PAD), h)
               for h in range(ATTN_HEADS_PER_STEP)]
    outs = _flash(streams, _row_chunks(row0, MLA_Q_TILE), k_ref, vt_ref, m_ref, acc_ref, row0, MLA_Q_TILE)
    for h, o_t in enumerate(outs):
        o_ref[h] = o_t.astype(BF16)


def _mla_attn_call(q, k, vt):
    b, s, _ = q.shape
    hp = ATTN_HEADS_PER_STEP
    return pl.pallas_call(
        _mla_attn_kernel,
        grid=(b, MLA_HEADS // hp, s // MLA_Q_TILE),
        in_specs=[pl.BlockSpec((1, MLA_Q_TILE, hp * MLA_QK_PAD), lambda bi, h, i: (bi, i, h)),
                  pl.BlockSpec((1, s, hp * MLA_QK_PAD), lambda bi, h, i: (bi, 0, h)),
                  _vt_spec(hp, s)],
        out_specs=_ot_spec(hp, MLA_Q_TILE, s),
        out_shape=jax.ShapeDtypeStruct((MLA_HEADS, MLA_V, b * s), BF16),
        scratch_shapes=_flash_scratch(hp, MLA_Q_TILE),
        compiler_params=_params("parallel", "parallel", "arbitrary"),
        name="mla_attn",
    )(q, k, vt)


def _diff_attn_kernel(lambda_init, q_ref, k_ref, vt_ref, lam_ref, o_ref, m_ref, acc_ref):
    row0 = pl.program_id(2) * DF_Q_TILE
    dv = 2 * DF_DH
    lane = lax.broadcasted_iota(jnp.int32, (1, dv), 1)
    streams = []
    for h in range(ATTN_HEADS_PER_STEP):
        hs = slice(h * dv, (h + 1) * dv)
        q = q_ref[0, :, hs]
        zero = jnp.zeros_like(q)
        q2 = jnp.concatenate([jnp.where(lane < DF_DH, q, zero), jnp.where(lane >= DF_DH, q, zero)], axis=0)
        streams.append((q2, hs, h))
    chunks = _row_chunks(row0, DF_Q_TILE)
    outs = _flash(streams, jnp.concatenate([chunks, chunks], axis=1), k_ref, vt_ref, m_ref, acc_ref, row0,
                  DF_Q_TILE)
    lp = lam_ref[...]
    lam = (jnp.exp(jnp.sum(lp[0:1] * lp[1:2], axis=-1, keepdims=True))
           - jnp.exp(jnp.sum(lp[2:3] * lp[3:4], axis=-1, keepdims=True)) + lambda_init)
    for h, o2_t in enumerate(outs):
        o_ref[h] = (o2_t[:, :DF_Q_TILE] - lam * o2_t[:, DF_Q_TILE:]).astype(BF16)


def _diff_attn_call(q, k, vt, lam_p, lambda_init):
    b, s, _ = q.shape
    hp = ATTN_HEADS_PER_STEP
    w = hp * 2 * DF_DH
    return pl.pallas_call(
        functools.partial(_diff_attn_kernel, lambda_init),
        grid=(b, DF_HEADS // hp, s // DF_Q_TILE),
        in_specs=[pl.BlockSpec((1, DF_Q_TILE, w), lambda bi, h, i: (bi, i, h)),
                  pl.BlockSpec((1, s, w), lambda bi, h, i: (bi, 0, h)),
                  _vt_spec(hp, s),
                  lam_p.spec],
        out_specs=_ot_spec(hp, DF_Q_TILE, s),
        out_shape=jax.ShapeDtypeStruct((DF_HEADS, 2 * DF_DH, b * s), BF16),
        scratch_shapes=_flash_scratch(hp, 2 * DF_Q_TILE),
        compiler_params=_params("parallel", "parallel", "arbitrary"),
        name="diff_attn",
    )(q, k, vt, lam_p.arr)


def _hgrn_kernel(hq_ref, hf_ref, hi_ref, hg_ref, lb_ref, go_ref, o_ref,
                 st_ref, b_ref, kk_ref, bpad_ref, kpad_ref, vpad_ref):
    n_chunks = HG_SEQ_TILE // CHUNK
    mid = CHUNK // 2 - 1

    @pl.when(pl.program_id(1) == 0)
    def _():
        st_ref[...] = jnp.zeros_like(st_ref)

    row_in_chunk = lax.broadcasted_iota(jnp.int32, (HG_GATE_ROWS, 1), 0) % CHUNK
    for hd in range(HG_HEADS):
        ls = slice(hd * HG_DK, (hd + 1) * HG_DK)
        lb = lb_ref[:, ls]
        for r0 in range(0, HG_SEQ_TILE, HG_GATE_ROWS):
            rs = slice(r0, r0 + HG_GATE_ROWS)
            sig = jax.nn.sigmoid(hf_ref[rs, ls])
            kk_ref[hd, rs, :] = (1.0 - lb) * (1.0 - sig)
            b = jnp.log(jnp.maximum(lb + (1.0 - lb) * sig, TINY))
            step = 1
            while step < CHUNK:
                b = b + jnp.where(row_in_chunk >= step, pltpu.roll(b, step, 0), 0.0)
                step *= 2
            b_ref[hd, rs, :] = b

    b_mid = b_ref[:, pl.ds(mid, n_chunks, stride=CHUNK), :]
    b_end = b_ref[:, pl.ds(CHUNK - 1, n_chunks, stride=CHUNK), :]
    worst = jnp.max(jnp.maximum(-b_mid, b_mid - b_end))

    row = lax.broadcasted_iota(jnp.int32, (CHUNK, 1), 0)

    def load(ci, hd):
        rows = pl.ds(pl.multiple_of(ci * CHUNK, CHUNK), CHUNK)
        ls = slice(hd * HG_DK, (hd + 1) * HG_DK)
        return rows, ls, hq_ref[rows, ls], hi_ref[rows, ls], b_ref[hd, rows, :], kk_ref[hd, rows, :]

    def carry_state(hd, qh, vh, b, kk):
        st = st_ref[hd]
        o = _dot_nt((qh * jnp.exp(b)).astype(BF16), st.astype(BF16))
        b_last = b[CHUNK - 1:CHUNK]
        kdec = kk * jnp.exp(b_last - b)
        st_ref[hd] = st * jnp.exp(b_last) + _dot_tn(vh.astype(BF16), kdec.astype(BF16))
        return o

    def finish(rows, ls, o):
        on = _rms(o, go_ref[:, ls]) * jax.nn.silu(hg_ref[rows, ls])
        o_ref[rows, ls] = on.astype(BF16)

    @pl.when(worst < HG_SAFE_LOG)
    def _():
        causal = row >= lax.broadcasted_iota(jnp.int32, (1, CHUNK), 1)

        def chunk_body(cj, carry):
            for u in range(HG_UNROLL):
                ci = cj * HG_UNROLL + u
                for hd in range(HG_HEADS):
                    rows, ls, qh, vh, b, kk = load(ci, hd)
                    o = carry_state(hd, qh, vh, b, kk)
                    b_m = b[mid:mid + 1]
                    qf = (qh * jnp.exp(b - b_m)).astype(BF16)
                    kf = (kk * jnp.exp(b_m - b)).astype(BF16)
                    a = jnp.where(causal, _dot_nt(qf, kf), 0.0)
                    finish(rows, ls, o + _dot(a.astype(BF16), vh.astype(BF16)))
            return carry

        lax.fori_loop(0, n_chunks // HG_UNROLL, chunk_body, 0)

    @pl.when(jnp.logical_not(worst < HG_SAFE_LOG))
    def _():
        n_sub = CHUNK // HG_SUB
        off_w = HG_SUB * (n_sub * (n_sub - 1) // 2)
        zpad = jnp.zeros((HG_HEADS, HG_SUB, HG_DK), F32)
        bpad_ref[:, :HG_SUB, :] = zpad
        kpad_ref[:, :HG_SUB, :] = zpad
        vpad_ref[:, :HG_SUB, :] = zpad
        row_sub = row % HG_SUB
        r2 = lax.broadcasted_iota(jnp.int32, (2 * HG_DK, 2 * HG_DK), 0) // HG_DK
        c2 = lax.broadcasted_iota(jnp.int32, (2 * HG_DK, 2 * HG_DK), 1) // HG_DK
        ones2 = jnp.where(r2 == c2, 1.0, 0.0).astype(BF16)
        col = lax.broadcasted_iota(jnp.int32, (1, off_w), 1)
        col_blk = jnp.zeros((1, off_w), jnp.int32)
        for i in range(1, n_sub):
            col_blk = col_blk + jnp.where(col >= HG_SUB * (i * (i - 1) // 2), 1, 0)
        off_mask = col_blk == (row // HG_SUB)

        def chunk_body(ci, carry):
            for hd in range(HG_HEADS):
                rows, ls, qh, vh, b, kk = load(ci, hd)
                o = carry_state(hd, qh, vh, b, kk)

                refs = [b[i * HG_SUB - 1:i * HG_SUB] for i in range(1, n_sub)]
                bref = jnp.concatenate(
                    [jnp.zeros((HG_SUB, HG_DK), F32)]
                    + [jnp.broadcast_to(r, (HG_SUB, HG_DK)) for r in refs], axis=0)
                qs = qh * jnp.exp(b - bref)
                kst = jnp.concatenate(
                    [kk[:i * HG_SUB] * jnp.exp(refs[i - 1] - b[:i * HG_SUB]) for i in range(1, n_sub)],
                    axis=0)
                vst = jnp.concatenate([vh[:i * HG_SUB] for i in range(1, n_sub)], axis=0)
                a_off = jnp.where(off_mask, _dot_nt(qs.astype(BF16), kst.astype(BF16)), 0.0)
                o = o + _dot(a_off.astype(BF16), vst.astype(BF16))

                bpad_ref[hd, HG_SUB:, :] = b
                kpad_ref[hd, HG_SUB:, :] = kk
                vpad_ref[hd, HG_SUB:, :] = vh
                for dp in range(HG_SUB // 2):
                    terms = []
                    for d in (2 * dp, 2 * dp + 1):
                        lo = HG_SUB - d
                        w = qh * jnp.exp(b - bpad_ref[hd, lo:lo + CHUNK, :]) * kpad_ref[hd, lo:lo + CHUNK, :]
                        terms.append(jnp.where(row_sub >= d, w, 0.0))
                    dsum = _dot(jnp.concatenate(terms, axis=1).astype(BF16), ones2)
                    for j, d in enumerate((2 * dp, 2 * dp + 1)):
                        lo = HG_SUB - d
                        o = o + dsum[:, j * HG_DK:(j + 1) * HG_DK] * vpad_ref[hd, lo:lo + CHUNK, :]
                finish(rows, ls, o)
            return carry

        lax.fori_loop(0, n_chunks, chunk_body, 0)


def _hgrn_call(zh, lb, g_out, b, s):
    width = HG_HEADS * HG_DK
    n_seq = s // HG_SEQ_TILE
    part = lambda j: pl.BlockSpec((HG_SEQ_TILE, width), lambda bi, si: (bi * n_seq + si, j))
    pad = pltpu.VMEM((HG_HEADS, HG_SUB + CHUNK, HG_DK), F32)
    return pl.pallas_call(
        _hgrn_kernel,
        grid=(b, n_seq),
        in_specs=[part(0), part(1), part(2), part(3), lb.spec, g_out.spec],
        out_specs=pl.BlockSpec((HG_SEQ_TILE, width), lambda bi, si: (bi * n_seq + si, 0)),
        out_shape=jax.ShapeDtypeStruct((b * s, width), BF16),
        scratch_shapes=[pltpu.VMEM((HG_HEADS, HG_DV, HG_DK), F32),
                        pltpu.VMEM((HG_HEADS, HG_SEQ_TILE, HG_DK), F32),
                        pltpu.VMEM((HG_HEADS, HG_SEQ_TILE, HG_DK), F32),
                        pad, pad, pad],
        compiler_params=_params("parallel", "arbitrary"),
        name="hgrn2",
    )(zh, zh, zh, zh, lb.arr, g_out.arr)


def _rope_tables(positions, dim, group):
    half = dim // 2
    inv_freq = ROPE_THETA ** (-jnp.arange(0, dim, 2, dtype=F32) / dim)
    ang = positions.astype(F32).reshape(-1, 1) * inv_freq
    cos, sin = jnp.cos(ang), jnp.sin(ang)
    t = ang.shape[0]
    rest = group - dim
    c = jnp.concatenate([cos, cos, jnp.ones((t, rest), F32)], axis=1)
    sa = jnp.concatenate([-sin, jnp.zeros((t, half + rest), F32)], axis=1)
    sb = jnp.concatenate([jnp.zeros((t, half), F32), sin, jnp.zeros((t, rest), F32)], axis=1)
    rep = LANES // group
    return tuple(jnp.tile(a, (1, rep)) for a in (c, sa, sb))


def kernel(x, positions, norm_g, ffn_w_gate, ffn_w_up, ffn_w_down, ev_w_in, ev_g_q, ev_w_uq, ev_g_kv, ev_w_ukv, ev_lb_logits, ev_g_out, ev_w_out, od_w_in, od_lambda, od_g_head, od_w_out):
    b, s, d = x.shape
    t = b * s
    xt = x.reshape(t, d)
    tabs_m = _rope_tables(positions, MLA_ROPE, LANES)
    tabs_d = _rope_tables(positions, DF_ROT, DF_DH)
    lb_w = jax.nn.softmax(ev_lb_logits.astype(F32), axis=0)
    lb_all = jnp.cumsum(lb_w, axis=0) - lb_w[0:1]

    wg = ffn_w_gate.astype(BF16)
    wu = ffn_w_up.astype(BF16)
    wd = ffn_w_down.astype(BF16)
    n_even = ev_w_in.shape[0]
    o_pe = MLA_Q_RANK + MLA_KV_RANK + MLA_ROPE
    w_in_e = jnp.concatenate(
        [ev_w_in[..., :o_pe].astype(BF16), jnp.zeros((n_even, d, LANES - MLA_ROPE), BF16),
         ev_w_in[..., o_pe:].astype(BF16)], axis=-1)
    w_uq = ev_w_uq.astype(BF16).reshape(n_even, MLA_Q_RANK, MLA_HEADS, MLA_NOPE + MLA_ROPE)
    w_uq = jnp.pad(w_uq, ((0, 0), (0, 0), (0, 0), (0, MLA_QK_PAD - MLA_NOPE - MLA_ROPE)))
    w_uq = w_uq.reshape(n_even, MLA_Q_RANK, MLA_HEADS * MLA_QK_PAD)
    w_ukv = ev_w_ukv.astype(BF16).reshape(n_even, MLA_KV_RANK, MLA_HEADS, 2, MLA_NOPE)
    w_ukv = w_ukv.transpose(0, 1, 3, 2, 4).reshape(n_even, MLA_KV_RANK, 2 * MLA_HEADS * MLA_NOPE)
    w_out_e = ev_w_out.astype(BF16)
    w_in_o = od_w_in.astype(BF16)
    w_out_o = od_w_out.astype(BF16)
    g_q = ev_g_q.reshape(n_even, 1, -1)
    g_kv = ev_g_kv.reshape(n_even, 1, -1)
    g_out = ev_g_out.reshape(n_even, 1, -1)
    lb_all = lb_all.reshape(n_even, 1, -1)
    g_head_o = od_g_head.reshape(od_g_head.shape[0], 1, -1)

    for l in range(DEPTH):
        xt = _ffn_call(xt, _Pick(norm_g, l, 0), _Pick(wg, l, 0), _Pick(wu, l, 0), _Pick(wd, l, 0))
        j = l // 2
        g_mix = _Pick(norm_g, l, 1)
        if l % 2 == 0:
            q, k, v, zh = _even_pre_call(
                xt, g_mix, _Pick(w_in_e, j), _Pick(g_q, j), _Pick(w_uq, j), _Pick(g_kv, j),
                _Pick(w_ukv, j), tabs_m)
            o_a = _mla_attn_call(q.reshape(b, s, -1), k.reshape(b, s, -1), v)
            o_b = _hgrn_call(zh, _Pick(lb_all, j), _Pick(g_out, j), b, s)
            parts = [o_a, o_b]
            head_norm = {}
            w_out = _Pick(w_out_e, j)
        else:
            lambda_init = 0.8 - 0.6 * math.exp(-0.3 * l)
            q, k, v = _odd_pre_call(xt, g_mix, _Pick(w_in_o, j), tabs_d)
            o = _diff_attn_call(q.reshape(b, s, -1), k.reshape(b, s, -1), v, _Pick(od_lambda, j),
                                lambda_init)
            parts = [o]
            head_norm = dict(g_head=_Pick(g_head_o, j), head_scale=1.0 - lambda_init)
            w_out = _Pick(w_out_o, j)
        xt = _mix_ffn_call(xt, parts, w_out, g_mix, _Pick(norm_g, l, 2),
                           _Pick(wg, l, 1), _Pick(wu, l, 1), _Pick(wd, l, 1), **head_norm)
    return xt.reshape(b, s, d)
```

```python
import functools
import math

import jax
import jax.numpy as jnp
from jax import lax
from jax.experimental import pallas as pl
from jax.experimental.pallas import tpu as pltpu

D_MODEL = 1024
DEPTH = 4
CHUNK = 64
ROPE_THETA = 500000.0
EPS = 1e-6
NEG_INF = -1e30
LOG2_E = math.log2(math.e)
TINY = 1e-30
D_FF = 2816
MLA_HEADS = 4
MLA_NOPE = 128
MLA_ROPE = 64
MLA_V = 128
MLA_Q_RANK = 384
MLA_KV_RANK = 256
HG_HEADS = 4
HG_DK = 128
HG_DV = 128
DF_HEADS = 8
DF_DH = 64
DF_ROT = DF_DH // 4

LANES = 128
MXU_DIM = 256
VMEM_LIMIT = 56 * 1024 * 1024

TOKEN_TILE = 512
MLA_Q_TILE = 512
DF_Q_TILE = 256
KEY_BLOCK = 512
ATTN_HEADS_PER_STEP = 4
HG_SEQ_TILE = 1024
HG_SUB = 16
HG_GATE_ROWS = 128
HG_UNROLL = 4
HG_SAFE_LOG = 75.0
FFN_ROW_GROUPS = 2
PRE_ROW_GROUPS = 2
FF_CHUNKS =((0, 1024), (1024, 2048), (2048, D_FF))

MLA_QK_PAD = 2 * LANES
V_ROWS = LANES + 16
EVEN_Z = MLA_Q_RANK + MLA_KV_RANK + LANES + 4 * HG_HEADS * HG_DK

F32 = jnp.float32
BF16 = jnp.bfloat16


def _rms(x, g):
    ms = jnp.mean(x * x, axis=-1, keepdims=True)
    return x * lax.rsqrt(ms + EPS) * g


def _dot(a, b):
    return jnp.dot(a, b, preferred_element_type=F32)


def _dot_nt(a, b):
    return lax.dot_general(a, b, (((1,), (1,)), ((), ())), preferred_element_type=F32)


def _dot_tn(a, b):
    return lax.dot_general(a, b, (((0,), (0,)), ((), ())), preferred_element_type=F32)


def _rope_slab(x, c, sa, sb, half):
    return x * c + pltpu.roll(x, LANES - half, 1) * sa + pltpu.roll(x, half, 1) * sb


def _store_values(vt_ref, rs, v, heads):
    ones = jnp.ones((V_ROWS - LANES, v.shape[0]), BF16)
    for hd in range(heads):
        vt_ref[hd, :LANES, rs] = v[:, hd * LANES:(hd + 1) * LANES].T.astype(BF16)
        vt_ref[hd, LANES:, rs] = ones


def _vt_out(heads, t):
    per_blk = KEY_BLOCK // TOKEN_TILE
    spec = pl.BlockSpec((heads, None, V_ROWS, TOKEN_TILE), lambda i: (0, i // per_blk, 0, i % per_blk))
    return spec, jax.ShapeDtypeStruct((heads, t // KEY_BLOCK, V_ROWS, KEY_BLOCK), BF16)


class _Pick:
    def __init__(self, arr, *idx):
        self.arr = arr
        nd = arr.ndim - len(idx)
        self.spec = pl.BlockSpec((None,) * len(idx) + arr.shape[len(idx):],
                                 lambda *_: idx + (0,) * nd, pipeline_mode=pl.Buffered(1))


def _params(*sem):
    return pltpu.CompilerParams(dimension_semantics=sem, vmem_limit_bytes=VMEM_LIMIT)


def _ffn_apply(xs, gpre, gpost, wg_ref, wu_ref, wd_ref):
    xn = [_rms(x, gpre).astype(BF16) for x in xs]
    acc = [None] * len(xs)
    for lo, hi in FF_CHUNKS:
        for i in range(len(xs)):
            g = _dot(xn[i], wg_ref[:, lo:hi])
            u = _dot(xn[i], wu_ref[:, lo:hi])
            a = (jax.nn.silu(g) * u).astype(BF16)
            h = _dot(a, wd_ref[lo:hi, :])
            acc[i] = h if acc[i] is None else acc[i] + h
    return [x + 0.5 * _rms(a, gpost) for x, a in zip(xs, acc)]


def _row_groups(n):
    step = TOKEN_TILE // n
    return [slice(i * step, (i + 1) * step) for i in range(n)]


def _ffn_kernel(x_ref, g_ref, wg_ref, wu_ref, wd_ref, o_ref):
    g = g_ref[...]
    groups = _row_groups(FFN_ROW_GROUPS)
    outs = _ffn_apply([x_ref[rs, :] for rs in groups], g[0:1], g[1:2], wg_ref, wu_ref, wd_ref)
    for rs, o in zip(groups, outs):
        o_ref[rs, :] = o


def _ffn_call(x, g2, wg, wu, wd):
    t = x.shape[0]
    tile = pl.BlockSpec((TOKEN_TILE, D_MODEL), lambda i: (i, 0))
    picks = [g2, wg, wu, wd]
    return pl.pallas_call(
        _ffn_kernel,
        grid=(t // TOKEN_TILE,),
        in_specs=[tile] + [p.spec for p in picks],
        out_specs=tile,
        out_shape=jax.ShapeDtypeStruct(x.shape, F32),
        compiler_params=_params("parallel"),
        name="ffn",
    )(x, *[p.arr for p in picks])


def _mix_ffn_kernel(n_parts, head_scale, *refs):
    x_ref = refs[0]
    part_refs = refs[1:1 + n_parts]
    rest = refs[1 + n_parts:]
    if head_scale is not None:
        gh_ref, rest = rest[0], rest[1:]
    wo_ref, gm_ref, gf_ref, wg_ref, wu_ref, wd_ref, o_ref = rest
    gf = gf_ref[...]
    groups = _row_groups(FFN_ROW_GROUPS)
    xs = []
    for rs in groups:
        cols = []
        for p_ref in part_refs:
            if len(p_ref.shape) == 2:
                cols.append(p_ref[rs, :])
                continue
            for hd in range(p_ref.shape[0]):
                o_t = p_ref[hd, :, rs].astype(F32)
                if head_scale is not None:
                    o_t = o_t * lax.rsqrt(jnp.mean(o_t * o_t, axis=0, keepdims=True) + EPS)
                o = o_t.T
                if head_scale is not None:
                    o = o * (gh_ref[:, hd * LANES:(hd + 1) * LANES] * head_scale)
                cols.append(o.astype(BF16))
        m = _dot(jnp.concatenate(cols, axis=1), wo_ref[...])
        xs.append(x_ref[rs, :] + _rms(m, gm_ref[1:2]))
    outs = _ffn_apply(xs, gf[0:1], gf[1:2], wg_ref, wu_ref, wd_ref)
    for rs, o in zip(groups, outs):
        o_ref[rs, :] = o


def _mix_ffn_call(x, parts, wo, g_mix, g_ffn, wg, wu, wd, g_head=None, head_scale=None):
    t = x.shape[0]
    tile = pl.BlockSpec((TOKEN_TILE, D_MODEL), lambda i: (i, 0))
    part_specs = [pl.BlockSpec((TOKEN_TILE, p.shape[-1]), lambda i: (i, 0)) if p.ndim == 2
                  else pl.BlockSpec(p.shape[:2] + (TOKEN_TILE,), lambda i: (0, 0, i)) for p in parts]
    picks = ([] if g_head is None else [g_head]) + [wo, g_mix, g_ffn, wg, wu, wd]
    return pl.pallas_call(
        functools.partial(_mix_ffn_kernel, len(parts), head_scale),
        grid=(t // TOKEN_TILE,),
        in_specs=[tile] + part_specs + [p.spec for p in picks],
        out_specs=tile,
        out_shape=jax.ShapeDtypeStruct(x.shape, F32),
        compiler_params=_params("parallel"),
        name="mix_ffn",
    )(x, *parts, *[p.arr for p in picks])


def _even_pre_kernel(x_ref, g_ref, win_ref, gq_ref, wuq_ref, gkv_ref, wukv_ref,
                     c_ref, sa_ref, sb_ref, q_ref, k_ref, v_ref, zh_ref):
    o_kv = MLA_Q_RANK
    o_pe = o_kv + MLA_KV_RANK
    o_h = o_pe + LANES
    half = MLA_ROPE // 2
    scale = LOG2_E * (MLA_NOPE + MLA_ROPE) ** -0.5
    for rs in _row_groups(PRE_ROW_GROUPS):
        h = _rms(x_ref[rs, :], g_ref[0:1]).astype(BF16)
        z = _dot(h, win_ref[...])
        zh_ref[rs, :] = z[:, o_h:]
        c, sa, sb = c_ref[rs, :], sa_ref[rs, :], sb_ref[rs, :]

        cq = _rms(z[:, :o_kv], gq_ref[...]).astype(BF16)
        q = _dot(cq, wuq_ref[...]) * scale
        for hd in range(MLA_HEADS):
            base = hd * MLA_QK_PAD
            q_ref[rs, base:base + LANES] = q[:, base:base + LANES].astype(BF16)
            q_ref[rs, base + LANES:base + 2 * LANES] = _rope_slab(
                q[:, base + LANES:base + 2 * LANES], c, sa, sb, half).astype(BF16)

        ckv = _rms(z[:, o_kv:o_pe], gkv_ref[...]).astype(BF16)
        kv = _dot(ckv, wukv_ref[...])
        kpe = _rope_slab(z[:, o_pe:o_h], c, sa, sb, half).astype(BF16)
        for hd in range(MLA_HEADS):
            base = hd * MLA_QK_PAD
            k_ref[rs, base:base + LANES] = kv[:, hd * LANES:(hd + 1) * LANES].astype(BF16)
            k_ref[rs, base + LANES:base + 2 * LANES] = kpe
        _store_values(v_ref, rs, kv[:, MLA_HEADS * MLA_NOPE:], MLA_HEADS)


def _even_pre_call(x, g, win, gq, wuq, gkv, wukv, tabs):
    t = x.shape[0]
    picks = [g, win, gq, wuq, gkv, wukv]
    row = lambda w: pl.BlockSpec((TOKEN_TILE, w), lambda i: (i, 0))
    qk_w = MLA_HEADS * MLA_QK_PAD
    vt_spec, vt_shape = _vt_out(MLA_HEADS, t)
    zh_w = 4 * HG_HEADS * HG_DK
    return pl.pallas_call(
        _even_pre_kernel,
        grid=(t // TOKEN_TILE,),
        in_specs=[row(D_MODEL)] + [p.spec for p in picks] + [row(LANES), row(LANES), row(LANES)],
        out_specs=[row(qk_w), row(qk_w), vt_spec, row(zh_w)],
        out_shape=[jax.ShapeDtypeStruct((t, qk_w), BF16), jax.ShapeDtypeStruct((t, qk_w), BF16),
                   vt_shape, jax.ShapeDtypeStruct((t, zh_w), F32)],
        compiler_params=_params("parallel"),
        name="even_pre",
    )(x, *[p.arr for p in picks], *tabs)


def _odd_pre_kernel(x_ref, g_ref, win_ref, c_ref, sa_ref, sb_ref, q_ref, k_ref, v_ref):
    half = DF_ROT // 2
    width = DF_HEADS * 2 * DF_DH
    scale = LOG2_E * DF_DH ** -0.5
    for rs in _row_groups(PRE_ROW_GROUPS):
        h = _rms(x_ref[rs, :], g_ref[0:1]).astype(BF16)
        c, sa, sb = c_ref[rs, :], sa_ref[rs, :], sb_ref[rs, :]
        q = _dot(h, win_ref[:, :width]) * scale
        k = _dot(h, win_ref[:, width:2 * width])
        for j in range(width // LANES):
            sl = slice(j * LANES, (j + 1) * LANES)
            q_ref[rs, sl] = _rope_slab(q[:, sl], c, sa, sb, half).astype(BF16)
            k_ref[rs, sl] = _rope_slab(k[:, sl], c, sa, sb, half).astype(BF16)
        _store_values(v_ref, rs, _dot(h, win_ref[:, 2 * width:]), DF_HEADS)


def _odd_pre_call(x, g, win, tabs):
    t = x.shape[0]
    width = DF_HEADS * 2 * DF_DH
    row = lambda w: pl.BlockSpec((TOKEN_TILE, w), lambda i: (i, 0))
    vt_spec, vt_shape = _vt_out(DF_HEADS, t)
    return pl.pallas_call(
        _odd_pre_kernel,
        grid=(t // TOKEN_TILE,),
        in_specs=[row(D_MODEL), g.spec, win.spec, row(LANES), row(LANES), row(LANES)],
        out_specs=[row(width), row(width), vt_spec],
        out_shape=[jax.ShapeDtypeStruct((t, width), BF16), jax.ShapeDtypeStruct((t, width), BF16), vt_shape],
        compiler_params=_params("parallel"),
        name="odd_pre",
    )(x, g.arr, win.arr, *tabs)


def _flash(streams, q_chunk, k_ref, vt_ref, m_ref, acc_ref, row0, tq):
    m_rows = streams[0][0].shape[0]
    for i in range(len(streams)):
        m_ref[i] = jnp.full((1, m_rows), NEG_INF, F32)
        acc_ref[i] = jnp.zeros((V_ROWS, m_rows), F32)

    def step(blk, width, masked):
        start = pl.multiple_of(blk * KEY_BLOCK, KEY_BLOCK)
        scores = [_dot_nt(k_ref[0, pl.ds(start, width), ksl], q) for q, ksl, _ in streams]
        for i, (s, (_, _, hd)) in enumerate(zip(scores, streams)):
            if masked:
                k_chunk = (start + (width - tq) + lax.broadcasted_iota(jnp.int32, (tq, 1), 0)) // CHUNK
                tail = jnp.where(k_chunk <= q_chunk, s[width - tq:], NEG_INF)
                s = tail if width == tq else jnp.concatenate([s[:width - tq], tail], axis=0)
            m_i = m_ref[i]
            m_new = jnp.maximum(m_i, jnp.max(s, axis=0, keepdims=True))
            m_ref[i] = m_new
            p = jnp.exp2((s - m_new).astype(BF16))
            acc_ref[i] = jnp.exp2(m_i - m_new) * acc_ref[i] + _dot(vt_ref[hd, blk, :, :width], p)

    n_full = row0 // KEY_BLOCK

    def unmasked(blk, carry):
        step(blk, KEY_BLOCK, False)
        return carry

    lax.fori_loop(0, n_full, unmasked, 0)
    for p in range(KEY_BLOCK // tq):
        pl.when((row0 % KEY_BLOCK) // tq == p)(functools.partial(step, n_full, tq * (p + 1), True))
    outs = []
    for i in range(len(streams)):
        acc = acc_ref[i]
        outs.append(acc[:LANES] * (1.0 / acc[LANES:LANES + 1]))
    return outs


def _row_chunks(row0, n_rows):
    rows = row0 + lax.broadcasted_iota(jnp.int32, (1, n_rows), 1)
    return rows // CHUNK


def _flash_scratch(hp, m_rows):
    return [pltpu.VMEM((hp, 1, m_rows), F32), pltpu.VMEM((hp, V_ROWS, m_rows), F32)]


def _ot_spec(hp, tq, s):
    return pl.BlockSpec((hp, LANES, tq), lambda bi, h, i: (h, 0, bi * (s // tq) + i))


def _vt_spec(hp, s):
    return pl.BlockSpec((hp, s // KEY_BLOCK, V_ROWS, KEY_BLOCK), lambda bi, h, i: (h, bi, 0, 0))


def _mla_attn_kernel(q_ref, k_ref, vt_ref, o_ref, m_ref, acc_ref):
    row0 = pl.program_id(2) * MLA_Q_TILE
    streams = [(q_ref[0, :, h * MLA_QK_PAD:(h + 1) * MLA_QK_PAD],
                slice(h * MLA_QK_PAD, (h + 1) * MLA_QK_PAD), h)
               for h in range(ATTN_HEADS_PER_STEP)]
    outs = _flash(streams, _row_chunks(row0, MLA_Q_TILE), k_ref, vt_ref, m_ref, acc_ref, row0, MLA_Q_TILE)
    for h, o_t in enumerate(outs):
        o_ref[h] = o_t.astype(BF16)


def _mla_attn_call(q, k, vt):
    b, s, _ = q.shape
    hp = ATTN_HEADS_PER_STEP
    return pl.pallas_call(
        _mla_attn_kernel,
        grid=(b, MLA_HEADS // hp, s // MLA_Q_TILE),
        in_specs=[pl.BlockSpec((1, MLA_Q_TILE, hp * MLA_QK_PAD), lambda bi, h, i: (bi, i, h)),
                  pl.BlockSpec((1, s, hp * MLA_QK_PAD), lambda bi, h, i: (bi, 0, h)),
                  _vt_spec(hp, s)],
        out_specs=_ot_spec(hp, MLA_Q_TILE, s),
        out_shape=jax.ShapeDtypeStruct((MLA_HEADS, MLA_V, b * s), BF16),
        scratch_shapes=_flash_scratch(hp, MLA_Q_TILE),
        compiler_params=_params("parallel", "parallel", "arbitrary"),
        name="mla_attn",
    )(q, k, vt)


def _diff_attn_kernel(lambda_init, q_ref, k_ref, vt_ref, lam_ref, o_ref, m_ref, acc_ref):
    row0 = pl.program_id(2) * DF_Q_TILE
    dv = 2 * DF_DH
    lane = lax.broadcasted_iota(jnp.int32, (1, dv), 1)
    streams = []
    for h in range(ATTN_HEADS_PER_STEP):
        hs = slice(h * dv, (h + 1) * dv)
        q = q_ref[0, :, hs]
        zero = jnp.zeros_like(q)
        q2 = jnp.concatenate([jnp.where(lane < DF_DH, q, zero), jnp.where(lane >= DF_DH, q, zero)], axis=0)
        streams.append((q2, hs, h))
    chunks = _row_chunks(row0, DF_Q_TILE)
    outs = _flash(streams, jnp.concatenate([chunks, chunks], axis=1), k_ref, vt_ref, m_ref, acc_ref, row0,
                  DF_Q_TILE)
    lp = lam_ref[...]
    lam = (jnp.exp(jnp.sum(lp[0:1] * lp[1:2], axis=-1, keepdims=True))
           - jnp.exp(jnp.sum(lp[2:3] * lp[3:4], axis=-1, keepdims=True)) + lambda_init)
    for h, o2_t in enumerate(outs):
        o_ref[h] = (o2_t[:, :DF_Q_TILE] - lam * o2_t[:, DF_Q_TILE:]).astype(BF16)


def _diff_attn_call(q, k, vt, lam_p, lambda_init):
    b, s, _ = q.shape
    hp = ATTN_HEADS_PER_STEP
    w = hp * 2 * DF_DH
    return pl.pallas_call(
        functools.partial(_diff_attn_kernel, lambda_init),
        grid=(b, DF_HEADS // hp, s // DF_Q_TILE),
        in_specs=[pl.BlockSpec((1, DF_Q_TILE, w), lambda bi, h, i: (bi, i, h)),
                  pl.BlockSpec((1, s, w), lambda bi, h, i: (bi, 0, h)),
                  _vt_spec(hp, s),
                  lam_p.spec],
        out_specs=_ot_spec(hp, DF_Q_TILE, s),
        out_shape=jax.ShapeDtypeStruct((DF_HEADS, 2 * DF_DH, b * s), BF16),
        scratch_shapes=_flash_scratch(hp, 2 * DF_Q_TILE),
        compiler_params=_params("parallel", "parallel", "arbitrary"),
        name="diff_attn",
    )(q, k, vt, lam_p.arr)


def _hgrn_kernel(hq_ref, hf_ref, hi_ref, hg_ref, lb_ref, go_ref, o_ref,
                 st_ref, b_ref, kk_ref, bpad_ref, kpad_ref, vpad_ref):
    n_chunks = HG_SEQ_TILE // CHUNK
    mid = CHUNK // 2 - 1

    @pl.when(pl.program_id(1) == 0)
    def _():
        st_ref[...] = jnp.zeros_like(st_ref)

    row_in_chunk = lax.broadcasted_iota(jnp.int32, (HG_GATE_ROWS, 1), 0) % CHUNK
    for hd in range(HG_HEADS):
        ls = slice(hd * HG_DK, (hd + 1) * HG_DK)
        lb = lb_ref[:, ls]
        for r0 in range(0, HG_SEQ_TILE, HG_GATE_ROWS):
            rs = slice(r0, r0 + HG_GATE_ROWS)
            sig = jax.nn.sigmoid(hf_ref[rs, ls])
            kk_ref[hd, rs, :] = (1.0 - lb) * (1.0 - sig)
            b = jnp.log(jnp.maximum(lb + (1.0 - lb) * sig, TINY))
            step = 1
            while step < CHUNK:
                b = b + jnp.where(row_in_chunk >= step, pltpu.roll(b, step, 0), 0.0)
                step *= 2
            b_ref[hd, rs, :] = b

    b_mid = b_ref[:, pl.ds(mid, n_chunks, stride=CHUNK), :]
    b_end = b_ref[:, pl.ds(CHUNK - 1, n_chunks, stride=CHUNK), :]
    worst = jnp.max(jnp.maximum(-b_mid, b_mid - b_end))

    row = lax.broadcasted_iota(jnp.int32, (CHUNK, 1), 0)

    def load(ci, hd):
        rows = pl.ds(pl.multiple_of(ci * CHUNK, CHUNK), CHUNK)
        ls = slice(hd * HG_DK, (hd + 1) * HG_DK)
        return rows, ls, hq_ref[rows, ls], hi_ref[rows, ls], b_ref[hd, rows, :], kk_ref[hd, rows, :]

    def carry_state(hd, qh, vh, b, kk):
        st = st_ref[hd]
        o = _dot_nt((qh * jnp.exp(b)).astype(BF16), st.astype(BF16))
        b_last = b[CHUNK - 1:CHUNK]
        kdec = kk * jnp.exp(b_last - b)
        st_ref[hd] = st * jnp.exp(b_last) + _dot_tn(vh.astype(BF16), kdec.astype(BF16))
        return o

    def finish(rows, ls, o):
        on = _rms(o, go_ref[:, ls]) * jax.nn.silu(hg_ref[rows, ls])
        o_ref[rows, ls] = on.astype(BF16)

    @pl.when(worst < HG_SAFE_LOG)
    def _():
        causal = row >= lax.broadcasted_iota(jnp.int32, (1, CHUNK), 1)

        def chunk_body(cj, carry):
            for u in range(HG_UNROLL):
                ci = cj * HG_UNROLL + u
                for hd in range(HG_HEADS):
                    rows, ls, qh, vh, b, kk = load(ci, hd)
                    o = carry_state(hd, qh, vh, b, kk)
                    b_m = b[mid:mid + 1]
                    qf = (qh * jnp.exp(b - b_m)).astype(BF16)
                    kf = (kk * jnp.exp(b_m - b)).astype(BF16)
                    a = jnp.where(causal, _dot_nt(qf, kf), 0.0)
                    finish(rows, ls, o + _dot(a.astype(BF16), vh.astype(BF16)))
            return carry

        lax.fori_loop(0, n_chunks // HG_UNROLL, chunk_body, 0)

    @pl.when(jnp.logical_not(worst < HG_SAFE_LOG))
    def _():
        n_sub = CHUNK // HG_SUB
        off_w = HG_SUB * (n_sub * (n_sub - 1) // 2)
        zpad = jnp.zeros((HG_HEADS, HG_SUB, HG_DK), F32)
        bpad_ref[:, :HG_SUB, :] = zpad
        kpad_ref[:, :HG_SUB, :] = zpad
        vpad_ref[:, :HG_SUB, :] = zpad
        row_sub = row % HG_SUB
        r2 = lax.broadcasted_iota(jnp.int32, (2 * HG_DK, 2 * HG_DK), 0) // HG_DK
        c2 = lax.broadcasted_iota(jnp.int32, (2 * HG_DK, 2 * HG_DK), 1) // HG_DK
        ones2 = jnp.where(r2 == c2, 1.0, 0.0).astype(BF16)
        col = lax.broadcasted_iota(jnp.int32, (1, off_w), 1)
        col_blk = jnp.zeros((1, off_w), jnp.int32)
        for i in range(1, n_sub):
            col_blk = col_blk + jnp.where(col >= HG_SUB * (i * (i - 1) // 2), 1, 0)
        off_mask = col_blk == (row // HG_SUB)

        def chunk_body(ci, carry):
            for hd in range(HG_HEADS):
                rows, ls, qh, vh, b, kk = load(ci, hd)
                o = carry_state(hd, qh, vh, b, kk)

                refs = [b[i * HG_SUB - 1:i * HG_SUB] for i in range(1, n_sub)]
                bref = jnp.concatenate(
                    [jnp.zeros((HG_SUB, HG_DK), F32)]
                    + [jnp.broadcast_to(r, (HG_SUB, HG_DK)) for r in refs], axis=0)
                qs = qh * jnp.exp(b - bref)
                kst = jnp.concatenate(
                    [kk[:i * HG_SUB] * jnp.exp(refs[i - 1] - b[:i * HG_SUB]) for i in range(1, n_sub)],
                    axis=0)
                vst = jnp.concatenate([vh[:i * HG_SUB] for i in range(1, n_sub)], axis=0)
                a_off = jnp.where(off_mask, _dot_nt(qs.astype(BF16), kst.astype(BF16)), 0.0)
                o = o + _dot(a_off.astype(BF16), vst.astype(BF16))

                bpad_ref[hd, HG_SUB:, :] = b
                kpad_ref[hd, HG_SUB:, :] = kk
                vpad_ref[hd, HG_SUB:, :] = vh
                for dp in range(HG_SUB // 2):
                    terms = []
                    for d in (2 * dp, 2 * dp + 1):
                        lo = HG_SUB - d
                        w = qh * jnp.exp(b - bpad_ref[hd, lo:lo + CHUNK, :]) * kpad_ref[hd, lo:lo + CHUNK, :]
                        terms.append(jnp.where(row_sub >= d, w, 0.0))
                    dsum = _dot(jnp.concatenate(terms, axis=1).astype(BF16), ones2)
                    for j, d in enumerate((2 * dp, 2 * dp + 1)):
                        lo = HG_SUB - d
                        o = o + dsum[:, j * HG_DK:(j + 1) * HG_DK] * vpad_ref[hd, lo:lo + CHUNK, :]
                finish(rows, ls, o)
            return carry

        lax.fori_loop(0, n_chunks, chunk_body, 0)


def _hgrn_call(zh, lb, g_out, b, s):
    width = HG_HEADS * HG_DK
    n_seq = s // HG_SEQ_TILE
    part = lambda j: pl.BlockSpec((HG_SEQ_TILE, width), lambda bi, si: (bi * n_seq + si, j))
    pad = pltpu.VMEM((HG_HEADS, HG_SUB + CHUNK, HG_DK), F32)
    return pl.pallas_call(
        _hgrn_kernel,
        grid=(b, n_seq),
        in_specs=[part(0), part(1), part(2), part(3), lb.spec, g_out.spec],
        out_specs=pl.BlockSpec((HG_SEQ_TILE, width), lambda bi, si: (bi * n_seq + si, 0)),
        out_shape=jax.ShapeDtypeStruct((b * s, width), BF16),
        scratch_shapes=[pltpu.VMEM((HG_HEADS, HG_DV, HG_DK), F32),
                        pltpu.VMEM((HG_HEADS, HG_SEQ_TILE, HG_DK), F32),
                        pltpu.VMEM((HG_HEADS, HG_SEQ_TILE, HG_DK), F32),
                        pad, pad, pad],
        compiler_params=_params("parallel", "arbitrary"),
        name="hgrn2",
    )(zh, zh, zh, zh, lb.arr, g_out.arr)


def _rope_tables(positions, dim, group):
    half = dim // 2
    inv_freq = ROPE_THETA ** (-jnp.arange(0, dim, 2, dtype=F32) / dim)
    ang = positions.astype(F32).reshape(-1, 1) * inv_freq
    cos, sin = jnp.cos(ang), jnp.sin(ang)
    t = ang.shape[0]
    rest = group - dim
    c = jnp.concatenate([cos, cos, jnp.ones((t, rest), F32)], axis=1)
    sa = jnp.concatenate([-sin, jnp.zeros((t, half + rest), F32)], axis=1)
    sb = jnp.concatenate([jnp.zeros((t, half), F32), sin, jnp.zeros((t, rest), F32)], axis=1)
    rep = LANES // group
    return tuple(jnp.tile(a, (1, rep)) for a in (c, sa, sb))


def kernel(x, positions, norm_g, ffn_w_gate, ffn_w_up, ffn_w_down, ev_w_in, ev_g_q, ev_w_uq, ev_g_kv, ev_w_ukv, ev_lb_logits, ev_g_out, ev_w_out, od_w_in, od_lambda, od_g_head, od_w_out):
    b, s, d = x.shape
    t = b * s
    xt = x.reshape(t, d)
    tabs_m = _rope_tables(positions, MLA_ROPE, LANES)
    tabs_d = _rope_tables(positions, DF_ROT, DF_DH)
    lb_w = jax.nn.softmax(ev_lb_logits.astype(F32), axis=0)
    lb_all = jnp.cumsum(lb_w, axis=0) - lb_w[0:1]

    wg = ffn_w_gate.astype(BF16)
    wu = ffn_w_up.astype(BF16)
    wd = ffn_w_down.astype(BF16)
    n_even = ev_w_in.shape[0]
    o_pe = MLA_Q_RANK + MLA_KV_RANK + MLA_ROPE
    w_in_e = jnp.concatenate(
        [ev_w_in[..., :o_pe].astype(BF16), jnp.zeros((n_even, d, LANES - MLA_ROPE), BF16),
         ev_w_in[..., o_pe:].astype(BF16)], axis=-1)
    w_uq = ev_w_uq.astype(BF16).reshape(n_even, MLA_Q_RANK, MLA_HEADS, MLA_NOPE + MLA_ROPE)
    w_uq = jnp.pad(w_uq, ((0, 0), (0, 0), (0, 0), (0, MLA_QK_PAD - MLA_NOPE - MLA_ROPE)))
    w_uq = w_uq.reshape(n_even, MLA_Q_RANK, MLA_HEADS * MLA_QK_PAD)
    w_ukv = ev_w_ukv.astype(BF16).reshape(n_even, MLA_KV_RANK, MLA_HEADS, 2, MLA_NOPE)
    w_ukv = w_ukv.transpose(0, 1, 3, 2, 4).reshape(n_even, MLA_KV_RANK, 2 * MLA_HEADS * MLA_NOPE)
    w_out_e = ev_w_out.astype(BF16)
    w_in_o = od_w_in.astype(BF16)
    w_out_o = od_w_out.astype(BF16)
    g_q = ev_g_q.reshape(n_even, 1, -1)
    g_kv = ev_g_kv.reshape(n_even, 1, -1)
    g_out = ev_g_out.reshape(n_even, 1, -1)
    lb_all = lb_all.reshape(n_even, 1, -1)
    g_head_o = od_g_head.reshape(od_g_head.shape[0], 1, -1)

    for l in range(DEPTH):
        xt = _ffn_call(xt, _Pick(norm_g, l, 0), _Pick(wg, l, 0), _Pick(wu, l, 0), _Pick(wd, l, 0))
        j = l // 2
        g_mix = _Pick(norm_g, l, 1)
        if l % 2 == 0:
            q, k, v, zh = _even_pre_call(
                xt, g_mix, _Pick(w_in_e, j), _Pick(g_q, j), _Pick(w_uq, j), _Pick(g_kv, j),
                _Pick(w_ukv, j), tabs_m)
            o_a = _mla_attn_call(q.reshape(b, s, -1), k.reshape(b, s, -1), v)
            o_b = _hgrn_call(zh, _Pick(lb_all, j), _Pick(g_out, j), b, s)
            parts = [o_a, o_b]
            head_norm = {}
            w_out = _Pick(w_out_e, j)
        else:
            lambda_init = 0.8 - 0.6 * math.exp(-0.3 * l)
            q, k, v = _odd_pre_call(xt, g_mix, _Pick(w_in_o, j), tabs_d)
            o = _diff_attn_call(q.reshape(b, s, -1), k.reshape(b, s, -1), v, _Pick(od_lambda, j),
                                lambda_init)
            parts = [o]
            head_norm = dict(g_head=_Pick(g_head_o, j), head_scale=1.0 - lambda_init)
            w_out = _Pick(w_out_o, j)
        xt = _mix_ffn_call(xt, parts, w_out, g_mix, _Pick(norm_g, l, 2),
                           _Pick(wg, l, 1), _Pick(wu, l, 1), _Pick(wd, l, 1), **head_norm)
    return xt.reshape(b, s, d)
```

```python
import functools
import math

import jax
import jax.numpy as jnp
from jax import lax
from jax.experimental import pallas as pl
from jax.experimental.pallas import tpu as pltpu

D_MODEL = 1024
DEPTH = 4
CHUNK = 64
ROPE_THETA = 500000.0
EPS = 1e-6
NEG_INF = -1e30
LOG2_E = math.log2(math.e)
TINY = 1e-30
D_FF = 2816
MLA_HEADS = 4
MLA_NOPE = 128
MLA_ROPE = 64
MLA_V = 128
MLA_Q_RANK = 384
MLA_KV_RANK = 256
HG_HEADS = 4
HG_DK = 128
HG_DV = 128
DF_HEADS = 8
DF_DH = 64
DF_ROT = DF_DH // 4

LANES = 128
MXU_DIM = 256
VMEM_LIMIT = 56 * 1024 * 1024

TOKEN_TILE = 1024
MLA_Q_TILE = 512
DF_Q_TILE = 256
KEY_BLOCK = 1024
ATTN_HEADS_PER_STEP = 4
HG_SEQ_TILE = 1024
HG_SUB = 16
HG_GATE_ROWS = 128
HG_UNROLL = 4
HG_SAFE_LOG = 75.0
FFN_ROW_GROUPS = 4
PRE_ROW_GROUPS = 2
FF_CHUNKS =((0, 1024), (1024, 2048), (2048, D_FF))

MLA_QK_PAD = 2 * LANES
V_ROWS = LANES + 16
EVEN_Z = MLA_Q_RANK + MLA_KV_RANK + LANES + 4 * HG_HEADS * HG_DK

F32 = jnp.float32
BF16 = jnp.bfloat16


def _rms(x, g):
    ms = jnp.mean(x * x, axis=-1, keepdims=True)
    return x * lax.rsqrt(ms + EPS) * g


def _dot(a, b):
    return jnp.dot(a, b, preferred_element_type=F32)


def _dot_nt(a, b):
    return lax.dot_general(a, b, (((1,), (1,)), ((), ())), preferred_element_type=F32)


def _dot_tn(a, b):
    return lax.dot_general(a, b, (((0,), (0,)), ((), ())), preferred_element_type=F32)


def _rope_slab(x, c, sa, sb, half):
    return x * c + pltpu.roll(x, LANES - half, 1) * sa + pltpu.roll(x, half, 1) * sb


def _store_values(vt_ref, rs, v, heads):
    ones = jnp.ones((V_ROWS - LANES, v.shape[0]), BF16)
    for hd in range(heads):
        vt_ref[hd, :LANES, rs] = v[:, hd * LANES:(hd + 1) * LANES].T.astype(BF16)
        vt_ref[hd, LANES:, rs] = ones


def _vt_out(heads, t):
    per_blk = KEY_BLOCK // TOKEN_TILE
    spec = pl.BlockSpec((heads, None, V_ROWS, TOKEN_TILE), lambda i: (0, i // per_blk, 0, i % per_blk))
    return spec, jax.ShapeDtypeStruct((heads, t // KEY_BLOCK, V_ROWS, KEY_BLOCK), BF16)


class _Pick:
    def __init__(self, arr, *idx):
        self.arr = arr
        nd = arr.ndim - len(idx)
        self.spec = pl.BlockSpec((None,) * len(idx) + arr.shape[len(idx):],
                                 lambda *_: idx + (0,) * nd, pipeline_mode=pl.Buffered(1))


def _params(*sem):
    return pltpu.CompilerParams(dimension_semantics=sem, vmem_limit_bytes=VMEM_LIMIT)


def _ffn_apply(xs, gpre, gpost, wg_ref, wu_ref, wd_ref):
    xn = [_rms(x, gpre).astype(BF16) for x in xs]
    acc = [None] * len(xs)
    for lo, hi in FF_CHUNKS:
        for i in range(len(xs)):
            g = _dot(xn[i], wg_ref[:, lo:hi])
            u = _dot(xn[i], wu_ref[:, lo:hi])
            a = (jax.nn.silu(g) * u).astype(BF16)
            h = _dot(a, wd_ref[lo:hi, :])
            acc[i] = h if acc[i] is None else acc[i] + h
    return [x + 0.5 * _rms(a, gpost) for x, a in zip(xs, acc)]


def _row_groups(n):
    step = TOKEN_TILE // n
    return [slice(i * step, (i + 1) * step) for i in range(n)]


def _ffn_kernel(x_ref, g_ref, wg_ref, wu_ref, wd_ref, o_ref):
    g = g_ref[...]
    groups = _row_groups(FFN_ROW_GROUPS)
    outs = _ffn_apply([x_ref[rs, :] for rs in groups], g[0:1], g[1:2], wg_ref, wu_ref, wd_ref)
    for rs, o in zip(groups, outs):
        o_ref[rs, :] = o


def _ffn_call(x, g2, wg, wu, wd):
    t = x.shape[0]
    tile = pl.BlockSpec((TOKEN_TILE, D_MODEL), lambda i: (i, 0))
    picks = [g2, wg, wu, wd]
    return pl.pallas_call(
        _ffn_kernel,
        grid=(t // TOKEN_TILE,),
        in_specs=[tile] + [p.spec for p in picks],
        out_specs=tile,
        out_shape=jax.ShapeDtypeStruct(x.shape, F32),
        compiler_params=_params("parallel"),
        name="ffn",
    )(x, *[p.arr for p in picks])


def _mix_ffn_kernel(n_parts, head_scale, *refs):
    x_ref = refs[0]
    part_refs = refs[1:1 + n_parts]
    rest = refs[1 + n_parts:]
    if head_scale is not None:
        gh_ref, rest = rest[0], rest[1:]
    wo_ref, gm_ref, gf_ref, wg_ref, wu_ref, wd_ref, o_ref = rest
    gf = gf_ref[...]
    groups = _row_groups(FFN_ROW_GROUPS)
    xs = []
    for rs in groups:
        cols = []
        for p_ref in part_refs:
            if len(p_ref.shape) == 2:
                cols.append(p_ref[rs, :])
                continue
            for hd in range(p_ref.shape[0]):
                o_t = p_ref[hd, :, rs].astype(F32)
                if head_scale is not None:
                    o_t = o_t * lax.rsqrt(jnp.mean(o_t * o_t, axis=0, keepdims=True) + EPS)
                o = o_t.T
                if head_scale is not None:
                    o = o * (gh_ref[:, hd * LANES:(hd + 1) * LANES] * head_scale)
                cols.append(o.astype(BF16))
        m = _dot(jnp.concatenate(cols, axis=1), wo_ref[...])
        xs.append(x_ref[rs, :] + _rms(m, gm_ref[1:2]))
    outs = _ffn_apply(xs, gf[0:1], gf[1:2], wg_ref, wu_ref, wd_ref)
    for rs, o in zip(groups, outs):
        o_ref[rs, :] = o


def _mix_ffn_call(x, parts, wo, g_mix, g_ffn, wg, wu, wd, g_head=None, head_scale=None):
    t = x.shape[0]
    tile = pl.BlockSpec((TOKEN_TILE, D_MODEL), lambda i: (i, 0))
    part_specs = [pl.BlockSpec((TOKEN_TILE, p.shape[-1]), lambda i: (i, 0)) if p.ndim == 2
                  else pl.BlockSpec(p.shape[:2] + (TOKEN_TILE,), lambda i: (0, 0, i)) for p in parts]
    picks = ([] if g_head is None else [g_head]) + [wo, g_mix, g_ffn, wg, wu, wd]
    return pl.pallas_call(
        functools.partial(_mix_ffn_kernel, len(parts), head_scale),
        grid=(t // TOKEN_TILE,),
        in_specs=[tile] + part_specs + [p.spec for p in picks],
        out_specs=tile,
        out_shape=jax.ShapeDtypeStruct(x.shape, F32),
        compiler_params=_params("parallel"),
        name="mix_ffn",
    )(x, *parts, *[p.arr for p in picks])


def _even_pre_kernel(x_ref, g_ref, win_ref, gq_ref, wuq_ref, gkv_ref, wukv_ref,
                     c_ref, sa_ref, sb_ref, q_ref, k_ref, v_ref, zh_ref):
    o_kv = MLA_Q_RANK
    o_pe = o_kv + MLA_KV_RANK
    o_h = o_pe + LANES
    half = MLA_ROPE // 2
    scale = LOG2_E * (MLA_NOPE + MLA_ROPE) ** -0.5
    for rs in _row_groups(PRE_ROW_GROUPS):
        h = _rms(x_ref[rs, :], g_ref[0:1]).astype(BF16)
        z = _dot(h, win_ref[...])
        zh_ref[rs, :] = z[:, o_h:]
        c, sa, sb = c_ref[rs, :], sa_ref[rs, :], sb_ref[rs, :]

        cq = _rms(z[:, :o_kv], gq_ref[...]).astype(BF16)
        q = _dot(cq, wuq_ref[...]) * scale
        for hd in range(MLA_HEADS):
            base = hd * MLA_QK_PAD
            q_ref[rs, base:base + LANES] = q[:, base:base + LANES].astype(BF16)
            q_ref[rs, base + LANES:base + 2 * LANES] = _rope_slab(
                q[:, base + LANES:base + 2 * LANES], c, sa, sb, half).astype(BF16)

        ckv = _rms(z[:, o_kv:o_pe], gkv_ref[...]).astype(BF16)
        kv = _dot(ckv, wukv_ref[...])
        kpe = _rope_slab(z[:, o_pe:o_h], c, sa, sb, half).astype(BF16)
        for hd in range(MLA_HEADS):
            base = hd * MLA_QK_PAD
            k_ref[rs, base:base + LANES] = kv[:, hd * LANES:(hd + 1) * LANES].astype(BF16)
            k_ref[rs, base + LANES:base + 2 * LANES] = kpe
        _store_values(v_ref, rs, kv[:, MLA_HEADS * MLA_NOPE:], MLA_HEADS)


def _even_pre_call(x, g, win, gq, wuq, gkv, wukv, tabs):
    t = x.shape[0]
    picks = [g, win, gq, wuq, gkv, wukv]
    row = lambda w: pl.BlockSpec((TOKEN_TILE, w), lambda i: (i, 0))
    qk_w = MLA_HEADS * MLA_QK_PAD
    vt_spec, vt_shape = _vt_out(MLA_HEADS, t)
    zh_w = 4 * HG_HEADS * HG_DK
    return pl.pallas_call(
        _even_pre_kernel,
        grid=(t // TOKEN_TILE,),
        in_specs=[row(D_MODEL)] + [p.spec for p in picks] + [row(LANES), row(LANES), row(LANES)],
        out_specs=[row(qk_w), row(qk_w), vt_spec, row(zh_w)],
        out_shape=[jax.ShapeDtypeStruct((t, qk_w), BF16), jax.ShapeDtypeStruct((t, qk_w), BF16),
                   vt_shape, jax.ShapeDtypeStruct((t, zh_w), F32)],
        compiler_params=_params("parallel"),
        name="even_pre",
    )(x, *[p.arr for p in picks], *tabs)


def _odd_pre_kernel(x_ref, g_ref, win_ref, c_ref, sa_ref, sb_ref, q_ref, k_ref, v_ref):
    half = DF_ROT // 2
    width = DF_HEADS * 2 * DF_DH
    scale = LOG2_E * DF_DH ** -0.5
    for rs in _row_groups(PRE_ROW_GROUPS):
        h = _rms(x_ref[rs, :], g_ref[0:1]).astype(BF16)
        c, sa, sb = c_ref[rs, :], sa_ref[rs, :], sb_ref[rs, :]
        q = _dot(h, win_ref[:, :width]) * scale
        k = _dot(h, win_ref[:, width:2 * width])
        for j in range(width // LANES):
            sl = slice(j * LANES, (j + 1) * LANES)
            q_ref[rs, sl] = _rope_slab(q[:, sl], c, sa, sb, half).astype(BF16)
            k_ref[rs, sl] = _rope_slab(k[:, sl], c, sa, sb, half).astype(BF16)
        _store_values(v_ref, rs, _dot(h, win_ref[:, 2 * width:]), DF_HEADS)


def _odd_pre_call(x, g, win, tabs):
    t = x.shape[0]
    width = DF_HEADS * 2 * DF_DH
    row = lambda w: pl.BlockSpec((TOKEN_TILE, w), lambda i: (i, 0))
    vt_spec, vt_shape = _vt_out(DF_HEADS, t)
    return pl.pallas_call(
        _odd_pre_kernel,
        grid=(t // TOKEN_TILE,),
        in_specs=[row(D_MODEL), g.spec, win.spec, row(LANES), row(LANES), row(LANES)],
        out_specs=[row(width), row(width), vt_spec],
        out_shape=[jax.ShapeDtypeStruct((t, width), BF16), jax.ShapeDtypeStruct((t, width), BF16), vt_shape],
        compiler_params=_params("parallel"),
        name="odd_pre",
    )(x, g.arr, win.arr, *tabs)


def _flash(streams, q_chunk, k_ref, vt_ref, m_ref, acc_ref, row0, tq):
    m_rows = streams[0][0].shape[0]
    for i in range(len(streams)):
        m_ref[i] = jnp.full((1, m_rows), NEG_INF, F32)
        acc_ref[i] = jnp.zeros((V_ROWS, m_rows), F32)

    def step(blk, width, masked):
        start = pl.multiple_of(blk * KEY_BLOCK, KEY_BLOCK)
        scores = [_dot_nt(k_ref[0, pl.ds(start, width), ksl], q) for q, ksl, _ in streams]
        for i, (s, (_, _, hd)) in enumerate(zip(scores, streams)):
            if masked:
                k_chunk = (start + (width - tq) + lax.broadcasted_iota(jnp.int32, (tq, 1), 0)) // CHUNK
                tail = jnp.where(k_chunk <= q_chunk, s[width - tq:], NEG_INF)
                s = tail if width == tq else jnp.concatenate([s[:width - tq], tail], axis=0)
            m_i = m_ref[i]
            m_new = jnp.maximum(m_i, jnp.max(s, axis=0, keepdims=True))
            m_ref[i] = m_new
            p = jnp.exp2((s - m_new).astype(BF16))
            acc_ref[i] = jnp.exp2(m_i - m_new) * acc_ref[i] + _dot(vt_ref[hd, blk, :, :width], p)

    n_full = row0 // KEY_BLOCK

    def unmasked(blk, carry):
        step(blk, KEY_BLOCK, False)
        return carry

    lax.fori_loop(0, n_full, unmasked, 0)
    for p in range(KEY_BLOCK // tq):
        pl.when((row0 % KEY_BLOCK) // tq == p)(functools.partial(step, n_full, tq * (p + 1), True))
    outs = []
    for i in range(len(streams)):
        acc = acc_ref[i]
        outs.append(acc[:LANES] * (1.0 / acc[LANES:LANES + 1]))
    return outs


def _row_chunks(row0, n_rows):
    rows = row0 + lax.broadcasted_iota(jnp.int32, (1, n_rows), 1)
    return rows // CHUNK


def _flash_scratch(hp, m_rows):
    return [pltpu.VMEM((hp, 1, m_rows), F32), pltpu.VMEM((hp, V_ROWS, m_rows), F32)]


def _ot_spec(hp, tq, s):
    return pl.BlockSpec((hp, LANES, tq), lambda bi, h, i: (h, 0, bi * (s // tq) + i))


def _vt_spec(hp, s):
    return pl.BlockSpec((hp, s // KEY_BLOCK, V_ROWS, KEY_BLOCK), lambda bi, h, i: (h, bi, 0, 0))


def _mla_attn_kernel(q_ref, k_ref, vt_ref, o_ref, m_ref, acc_ref):
    row0 = pl.program_id(2) * MLA_Q_TILE
    streams = [(q_ref[0, :, h * MLA_QK_PAD:(h + 1) * MLA_QK_PAD],
                slice(h * MLA_QK_PAD, (h + 1) * MLA_QK_PAD), h)
               for h in range(ATTN_HEADS_PER_STEP)]
    outs = _flash(streams, _row_chunks(row0, MLA_Q_TILE), k_ref, vt_ref, m_ref, acc_ref, row0, MLA_Q_TILE)
    for h, o_t in enumerate(outs):
        o_ref[h] = o_t.astype(BF16)


def _mla_attn_call(q, k, vt):
    b, s, _ = q.shape
    hp = ATTN_HEADS_PER_STEP
    return pl.pallas_call(
        _mla_attn_kernel,
        grid=(b, MLA_HEADS // hp, s // MLA_Q_TILE),
        in_specs=[pl.BlockSpec((1, MLA_Q_TILE, hp * MLA_QK_PAD), lambda bi, h, i: (bi, i, h)),
                  pl.BlockSpec((1, s, hp * MLA_QK_PAD), lambda bi, h, i: (bi, 0, h)),
                  _vt_spec(hp, s)],
        out_specs=_ot_spec(hp, MLA_Q_TILE, s),
        out_shape=jax.ShapeDtypeStruct((MLA_HEADS, MLA_V, b * s), BF16),
        scratch_shapes=_flash_scratch(hp, MLA_Q_TILE),
        compiler_params=_params("parallel", "parallel", "arbitrary"),
        name="mla_attn",
    )(q, k, vt)


def _diff_attn_kernel(lambda_init, q_ref, k_ref, vt_ref, lam_ref, o_ref, m_ref, acc_ref):
    row0 = pl.program_id(2) * DF_Q_TILE
    dv = 2 * DF_DH
    lane = lax.broadcasted_iota(jnp.int32, (1, dv), 1)
    streams = []
    for h in range(ATTN_HEADS_PER_STEP):
        hs = slice(h * dv, (h + 1) * dv)
        q = q_ref[0, :, hs]
        zero = jnp.zeros_like(q)
        q2 = jnp.concatenate([jnp.where(lane < DF_DH, q, zero), jnp.where(lane >= DF_DH, q, zero)], axis=0)
        streams.append((q2, hs, h))
    chunks = _row_chunks(row0, DF_Q_TILE)
    outs = _flash(streams, jnp.concatenate([chunks, chunks], axis=1), k_ref, vt_ref, m_ref, acc_ref, row0,
                  DF_Q_TILE)
    lp = lam_ref[...]
    lam = (jnp.exp(jnp.sum(lp[0:1] * lp[1:2], axis=-1, keepdims=True))
           - jnp.exp(jnp.sum(lp[2:3] * lp[3:4], axis=-1, keepdims=True)) + lambda_init)
    for h, o2_t in enumerate(outs):
        o_ref[h] = (o2_t[:, :DF_Q_TILE] - lam * o2_t[:, DF_Q_TILE:]).astype(BF16)


def _diff_attn_call(q, k, vt, lam_p, lambda_init):
    b, s, _ = q.shape
    hp = ATTN_HEADS_PER_STEP
    w = hp * 2 * DF_DH
    return pl.pallas_call(
        functools.partial(_diff_attn_kernel, lambda_init),
        grid=(b, DF_HEADS // hp, s // DF_Q_TILE),
        in_specs=[pl.BlockSpec((1, DF_Q_TILE, w), lambda bi, h, i: (bi, i, h)),
                  pl.BlockSpec((1, s, w), lambda bi, h, i: (bi, 0, h)),
                  _vt_spec(hp, s),
                  lam_p.spec],
        out_specs=_ot_spec(hp, DF_Q_TILE, s),
        out_shape=jax.ShapeDtypeStruct((DF_HEADS, 2 * DF_DH, b * s), BF16),
        scratch_shapes=_flash_scratch(hp, 2 * DF_Q_TILE),
        compiler_params=_params("parallel", "parallel", "arbitrary"),
        name="diff_attn",
    )(q, k, vt, lam_p.arr)


def _hgrn_kernel(hq_ref, hf_ref, hi_ref, hg_ref, lb_ref, go_ref, o_ref,
                 st_ref, b_ref, kk_ref, bpad_ref, kpad_ref, vpad_ref):
    n_chunks = HG_SEQ_TILE // CHUNK
    mid = CHUNK // 2 - 1

    @pl.when(pl.program_id(1) == 0)
    def _():
        st_ref[...] = jnp.zeros_like(st_ref)

    row_in_chunk = lax.broadcasted_iota(jnp.int32, (HG_GATE_ROWS, 1), 0) % CHUNK
    for hd in range(HG_HEADS):
        ls = slice(hd * HG_DK, (hd + 1) * HG_DK)
        lb = lb_ref[:, ls]
        for r0 in range(0, HG_SEQ_TILE, HG_GATE_ROWS):
            rs = slice(r0, r0 + HG_GATE_ROWS)
            sig = jax.nn.sigmoid(hf_ref[rs, ls])
            kk_ref[hd, rs, :] = (1.0 - lb) * (1.0 - sig)
            b = jnp.log(jnp.maximum(lb + (1.0 - lb) * sig, TINY))
            step = 1
            while step < CHUNK:
                b = b + jnp.where(row_in_chunk >= step, pltpu.roll(b, step, 0), 0.0)
                step *= 2
            b_ref[hd, rs, :] = b

    b_mid = b_ref[:, pl.ds(mid, n_chunks, stride=CHUNK), :]
    b_end = b_ref[:, pl.ds(CHUNK - 1, n_chunks, stride=CHUNK), :]
    worst = jnp.max(jnp.maximum(-b_mid, b_mid - b_end))

    row = lax.broadcasted_iota(jnp.int32, (CHUNK, 1), 0)

    def load(ci, hd):
        rows = pl.ds(pl.multiple_of(ci * CHUNK, CHUNK), CHUNK)
        ls = slice(hd * HG_DK, (hd + 1) * HG_DK)
        return rows, ls, hq_ref[rows, ls], hi_ref[rows, ls], b_ref[hd, rows, :], kk_ref[hd, rows, :]

    def carry_state(hd, qh, vh, b, kk):
        st = st_ref[hd]
        o = _dot_nt((qh * jnp.exp(b)).astype(BF16), st.astype(BF16))
        b_last = b[CHUNK - 1:CHUNK]
        kdec = kk * jnp.exp(b_last - b)
        st_ref[hd] = st * jnp.exp(b_last) + _dot_tn(vh.astype(BF16), kdec.astype(BF16))
        return o

    def finish(rows, ls, o):
        on = _rms(o, go_ref[:, ls]) * jax.nn.silu(hg_ref[rows, ls])
        o_ref[rows, ls] = on.astype(BF16)

    @pl.when(worst < HG_SAFE_LOG)
    def _():
        causal = row >= lax.broadcasted_iota(jnp.int32, (1, CHUNK), 1)

        def chunk_body(cj, carry):
            for u in range(HG_UNROLL):
                ci = cj * HG_UNROLL + u
                for hd in range(HG_HEADS):
                    rows, ls, qh, vh, b, kk = load(ci, hd)
                    o = carry_state(hd, qh, vh, b, kk)
                    b_m = b[mid:mid + 1]
                    qf = (qh * jnp.exp(b - b_m)).astype(BF16)
                    kf = (kk * jnp.exp(b_m - b)).astype(BF16)
                    a = jnp.where(causal, _dot_nt(qf, kf), 0.0)
                    finish(rows, ls, o + _dot(a.astype(BF16), vh.astype(BF16)))
            return carry

        lax.fori_loop(0, n_chunks // HG_UNROLL, chunk_body, 0)

    @pl.when(jnp.logical_not(worst < HG_SAFE_LOG))
    def _():
        n_sub = CHUNK // HG_SUB
        off_w = HG_SUB * (n_sub * (n_sub - 1) // 2)
        zpad = jnp.zeros((HG_HEADS, HG_SUB, HG_DK), F32)
        bpad_ref[:, :HG_SUB, :] = zpad
        kpad_ref[:, :HG_SUB, :] = zpad
        vpad_ref[:, :HG_SUB, :] = zpad
        row_sub = row % HG_SUB
        r2 = lax.broadcasted_iota(jnp.int32, (2 * HG_DK, 2 * HG_DK), 0) // HG_DK
        c2 = lax.broadcasted_iota(jnp.int32, (2 * HG_DK, 2 * HG_DK), 1) // HG_DK
        ones2 = jnp.where(r2 == c2, 1.0, 0.0).astype(BF16)
        col = lax.broadcasted_iota(jnp.int32, (1, off_w), 1)
        col_blk = jnp.zeros((1, off_w), jnp.int32)
        for i in range(1, n_sub):
            col_blk = col_blk + jnp.where(col >= HG_SUB * (i * (i - 1) // 2), 1, 0)
        off_mask = col_blk == (row // HG_SUB)

        def chunk_body(ci, carry):
            for hd in range(HG_HEADS):
                rows, ls, qh, vh, b, kk = load(ci, hd)
                o = carry_state(hd, qh, vh, b, kk)

                refs = [b[i * HG_SUB - 1:i * HG_SUB] for i in range(1, n_sub)]
                bref = jnp.concatenate(
                    [jnp.zeros((HG_SUB, HG_DK), F32)]
                    + [jnp.broadcast_to(r, (HG_SUB, HG_DK)) for r in refs], axis=0)
                qs = qh * jnp.exp(b - bref)
                kst = jnp.concatenate(
                    [kk[:i * HG_SUB] * jnp.exp(refs[i - 1] - b[:i * HG_SUB]) for i in range(1, n_sub)],
                    axis=0)
                vst = jnp.concatenate([vh[:i * HG_SUB] for i in range(1, n_sub)], axis=0)
                a_off = jnp.where(off_mask, _dot_nt(qs.astype(BF16), kst.astype(BF16)), 0.0)
                o = o + _dot(a_off.astype(BF16), vst.astype(BF16))

                bpad_ref[hd, HG_SUB:, :] = b
                kpad_ref[hd, HG_SUB:, :] = kk
                vpad_ref[hd, HG_SUB:, :] = vh
                for dp in range(HG_SUB // 2):
                    terms = []
                    for d in (2 * dp, 2 * dp + 1):
                        lo = HG_SUB - d
                        w = qh * jnp.exp(b - bpad_ref[hd, lo:lo + CHUNK, :]) * kpad_ref[hd, lo:lo + CHUNK, :]
                        terms.append(jnp.where(row_sub >= d, w, 0.0))
                    dsum = _dot(jnp.concatenate(terms, axis=1).astype(BF16), ones2)
                    for j, d in enumerate((2 * dp, 2 * dp + 1)):
                        lo = HG_SUB - d
                        o = o + dsum[:, j * HG_DK:(j + 1) * HG_DK] * vpad_ref[hd, lo:lo + CHUNK, :]
                finish(rows, ls, o)
            return carry

        lax.fori_loop(0, n_chunks, chunk_body, 0)


def _hgrn_call(zh, lb, g_out, b, s):
    width = HG_HEADS * HG_DK
    n_seq = s // HG_SEQ_TILE
    part = lambda j: pl.BlockSpec((HG_SEQ_TILE, width), lambda bi, si: (bi * n_seq + si, j))
    pad = pltpu.VMEM((HG_HEADS, HG_SUB + CHUNK, HG_DK), F32)
    return pl.pallas_call(
        _hgrn_kernel,
        grid=(b, n_seq),
        in_specs=[part(0), part(1), part(2), part(3), lb.spec, g_out.spec],
        out_specs=pl.BlockSpec((HG_SEQ_TILE, width), lambda bi, si: (bi * n_seq + si, 0)),
        out_shape=jax.ShapeDtypeStruct((b * s, width), BF16),
        scratch_shapes=[pltpu.VMEM((HG_HEADS, HG_DV, HG_DK), F32),
                        pltpu.VMEM((HG_HEADS, HG_SEQ_TILE, HG_DK), F32),
                        pltpu.VMEM((HG_HEADS, HG_SEQ_TILE, HG_DK), F32),
                        pad, pad, pad],
        compiler_params=_params("parallel", "arbitrary"),
        name="hgrn2",
    )(zh, zh, zh, zh, lb.arr, g_out.arr)


def _rope_tables(positions, dim, group):
    half = dim // 2
    inv_freq = ROPE_THETA ** (-jnp.arange(0, dim, 2, dtype=F32) / dim)
    ang = positions.astype(F32).reshape(-1, 1) * inv_freq
    cos, sin = jnp.cos(ang), jnp.sin(ang)
    t = ang.shape[0]
    rest = group - dim
    c = jnp.concatenate([cos, cos, jnp.ones((t, rest), F32)], axis=1)
    sa = jnp.concatenate([-sin, jnp.zeros((t, half + rest), F32)], axis=1)
    sb = jnp.concatenate([jnp.zeros((t, half), F32), sin, jnp.zeros((t, rest), F32)], axis=1)
    rep = LANES // group
    return tuple(jnp.tile(a, (1, rep)) for a in (c, sa, sb))


def kernel(x, positions, norm_g, ffn_w_gate, ffn_w_up, ffn_w_down, ev_w_in, ev_g_q, ev_w_uq, ev_g_kv, ev_w_ukv, ev_lb_logits, ev_g_out, ev_w_out, od_w_in, od_lambda, od_g_head, od_w_out):
    b, s, d = x.shape
    t = b * s
    xt = x.reshape(t, d)
    tabs_m = _rope_tables(positions, MLA_ROPE, LANES)
    tabs_d = _rope_tables(positions, DF_ROT, DF_DH)
    lb_w = jax.nn.softmax(ev_lb_logits.astype(F32), axis=0)
    lb_all = jnp.cumsum(lb_w, axis=0) - lb_w[0:1]

    wg = ffn_w_gate.astype(BF16)
    wu = ffn_w_up.astype(BF16)
    wd = ffn_w_down.astype(BF16)
    n_even = ev_w_in.shape[0]
    o_pe = MLA_Q_RANK + MLA_KV_RANK + MLA_ROPE
    w_in_e = jnp.concatenate(
        [ev_w_in[..., :o_pe].astype(BF16), jnp.zeros((n_even, d, LANES - MLA_ROPE), BF16),
         ev_w_in[..., o_pe:].astype(BF16)], axis=-1)
    w_uq = ev_w_uq.astype(BF16).reshape(n_even, MLA_Q_RANK, MLA_HEADS, MLA_NOPE + MLA_ROPE)
    w_uq = jnp.pad(w_uq, ((0, 0), (0, 0), (0, 0), (0, MLA_QK_PAD - MLA_NOPE - MLA_ROPE)))
    w_uq = w_uq.reshape(n_even, MLA_Q_RANK, MLA_HEADS * MLA_QK_PAD)
    w_ukv = ev_w_ukv.astype(BF16).reshape(n_even, MLA_KV_RANK, MLA_HEADS, 2, MLA_NOPE)
    w_ukv = w_ukv.transpose(0, 1, 3, 2, 4).reshape(n_even, MLA_KV_RANK, 2 * MLA_HEADS * MLA_NOPE)
    w_out_e = ev_w_out.astype(BF16)
    w_in_o = od_w_in.astype(BF16)
    w_out_o = od_w_out.astype(BF16)
    g_q = ev_g_q.reshape(n_even, 1, -1)
    g_kv = ev_g_kv.reshape(n_even, 1, -1)
    g_out = ev_g_out.reshape(n_even, 1, -1)
    lb_all = lb_all.reshape(n_even, 1, -1)
    g_head_o = od_g_head.reshape(od_g_head.shape[0], 1, -1)

    for l in range(DEPTH):
        xt = _ffn_call(xt, _Pick(norm_g, l, 0), _Pick(wg, l, 0), _Pick(wu, l, 0), _Pick(wd, l, 0))
        j = l // 2
        g_mix = _Pick(norm_g, l, 1)
        if l % 2 == 0:
            q, k, v, zh = _even_pre_call(
                xt, g_mix, _Pick(w_in_e, j), _Pick(g_q, j), _Pick(w_uq, j), _Pick(g_kv, j),
                _Pick(w_ukv, j), tabs_m)
            o_a = _mla_attn_call(q.reshape(b, s, -1), k.reshape(b, s, -1), v)
            o_b = _hgrn_call(zh, _Pick(lb_all, j), _Pick(g_out, j), b, s)
            parts = [o_a, o_b]
            head_norm = {}
            w_out = _Pick(w_out_e, j)
        else:
            lambda_init = 0.8 - 0.6 * math.exp(-0.3 * l)
            q, k, v = _odd_pre_call(xt, g_mix, _Pick(w_in_o, j), tabs_d)
            o = _diff_attn_call(q.reshape(b, s, -1), k.reshape(b, s, -1), v, _Pick(od_lambda, j),
                                lambda_init)
            parts = [o]
            head_norm = dict(g_head=_Pick(g_head_o, j), head_scale=1.0 - lambda_init)
            w_out = _Pick(w_out_o, j)
        xt = _mix_ffn_call(xt, parts, w_out, g_mix, _Pick(norm_g, l, 2),
                           _Pick(wg, l, 1), _Pick(wu, l, 1), _Pick(wd, l, 1), **head_norm)
    return xt.reshape(b, s, d)
```

```python
import functools
import math

import jax
import jax.numpy as jnp
from jax import lax
from jax.experimental import pallas as pl
from jax.experimental.pallas import tpu as pltpu

D_MODEL = 1024
DEPTH = 4
CHUNK = 64
ROPE_THETA = 500000.0
EPS = 1e-6
NEG_INF = -1e30
LOG2_E = math.log2(math.e)
TINY = 1e-30
D_FF = 2816
MLA_HEADS = 4
MLA_NOPE = 128
MLA_ROPE = 64
MLA_V = 128
MLA_Q_RANK = 384
MLA_KV_RANK = 256
HG_HEADS = 4
HG_DK = 128
HG_DV = 128
DF_HEADS = 8
DF_DH = 64
DF_ROT = DF_DH // 4

LANES = 128
MXU_DIM = 256
VMEM_LIMIT = 56 * 1024 * 1024

TOKEN_TILE = 1024
MIX_TILE = 512
MLA_Q_TILE = 512
DF_Q_TILE = 256
KEY_BLOCK = 1024
ATTN_HEADS_PER_STEP = 4
HG_SEQ_TILE = 1024
HG_SUB = 16
HG_GATE_ROWS = 128
HG_UNROLL = 4
HG_SAFE_LOG = 75.0
W_CAST_STEPS = 16
MIX_ROW_GROUPS = 2
FFN_ROW_GROUPS = 4
PRE_ROW_GROUPS = 2
FF_CHUNKS =((0, 1024), (1024, 2048), (2048, D_FF))

MLA_QK_PAD = 2 * LANES
V_ROWS = LANES + 16
EVEN_Z = MLA_Q_RANK + MLA_KV_RANK + LANES + 4 * HG_HEADS * HG_DK

F32 = jnp.float32
BF16 = jnp.bfloat16


def _rms(x, g):
    ms = jnp.mean(x * x, axis=-1, keepdims=True)
    return x * lax.rsqrt(ms + EPS) * g


def _dot(a, b):
    return jnp.dot(a, b, preferred_element_type=F32)


def _dot_nt(a, b):
    return lax.dot_general(a, b, (((1,), (1,)), ((), ())), preferred_element_type=F32)


def _dot_tn(a, b):
    return lax.dot_general(a, b, (((0,), (0,)), ((), ())), preferred_element_type=F32)


def _rope_slab(x, c, sa, sb, half):
    return x * c + pltpu.roll(x, LANES - half, 1) * sa + pltpu.roll(x, half, 1) * sb


def _store_values(vt_ref, rs, v, heads):
    ones = jnp.ones((V_ROWS - LANES, v.shape[0]), BF16)
    for hd in range(heads):
        vt_ref[hd, :LANES, rs] = v[:, hd * LANES:(hd + 1) * LANES].T.astype(BF16)
        vt_ref[hd, LANES:, rs] = ones


def _vt_out(heads, t):
    per_blk = KEY_BLOCK // TOKEN_TILE
    spec = pl.BlockSpec((heads, None, V_ROWS, TOKEN_TILE), lambda i: (0, i // per_blk, 0, i % per_blk))
    return spec, jax.ShapeDtypeStruct((heads, t // KEY_BLOCK, V_ROWS, KEY_BLOCK), BF16)


class _Slabs:
    def __init__(self, arr, layer, half):
        self.arr = arr
        rows, cols = arr.shape[2:]
        self.shape = (rows, cols)
        self.spec = pl.BlockSpec((None, None, rows // W_CAST_STEPS, cols),
                                 lambda i: (layer, half, jnp.minimum(i, W_CAST_STEPS - 1), 0))


def _cast_slabs(i, pairs):
    for j in range(W_CAST_STEPS):
        @pl.when(i == j)
        def _():
            for src, dst in pairs:
                rows = src.shape[0]
                dst[j * rows:(j + 1) * rows, :] = src[...].astype(BF16)


def _token_tile(i):
    return jnp.maximum(i - W_CAST_STEPS, 0)


class _Pick:
    def __init__(self, arr, *idx):
        self.arr = arr
        nd = arr.ndim - len(idx)
        self.spec = pl.BlockSpec((None,) * len(idx) + arr.shape[len(idx):],
                                 lambda *_: idx + (0,) * nd, pipeline_mode=pl.Buffered(1))


def _params(*sem):
    return pltpu.CompilerParams(dimension_semantics=sem, vmem_limit_bytes=VMEM_LIMIT)


def _ffn_apply(xs, gpre, gpost, wg_ref, wu_ref, wd_ref):
    xn = [_rms(x, gpre).astype(BF16) for x in xs]
    acc = [None] * len(xs)
    for lo, hi in FF_CHUNKS:
        for i in range(len(xs)):
            g = _dot(xn[i], wg_ref[:, lo:hi])
            u = _dot(xn[i], wu_ref[:, lo:hi])
            a = (jax.nn.silu(g) * u).astype(BF16)
            h = _dot(a, wd_ref[lo:hi, :])
            acc[i] = h if acc[i] is None else acc[i] + h
    return [x + 0.5 * _rms(a, gpost) for x, a in zip(xs, acc)]


def _row_groups(rows, n):
    step = rows // n
    return [slice(i * step, (i + 1) * step) for i in range(n)]


def _ffn_kernel(x_ref, g_ref, wg32_ref, wu32_ref, wd32_ref, o_ref, wg_ref, wu_ref, wd_ref):
    i = pl.program_id(0)
    _cast_slabs(i, [(wg32_ref, wg_ref), (wu32_ref, wu_ref), (wd32_ref, wd_ref)])

    @pl.when(i >= W_CAST_STEPS)
    def _():
        g = g_ref[...]
        groups = _row_groups(TOKEN_TILE, FFN_ROW_GROUPS)
        outs = _ffn_apply([x_ref[rs, :] for rs in groups], g[0:1], g[1:2], wg_ref, wu_ref, wd_ref)
        for rs, o in zip(groups, outs):
            o_ref[rs, :] = o


def _ffn_call(x, g2, wg, wu, wd):
    t = x.shape[0]
    tile = pl.BlockSpec((TOKEN_TILE, D_MODEL), lambda i: (_token_tile(i), 0))
    picks = [g2, wg, wu, wd]
    return pl.pallas_call(
        _ffn_kernel,
        grid=(W_CAST_STEPS + t // TOKEN_TILE,),
        in_specs=[tile] + [p.spec for p in picks],
        out_specs=tile,
        out_shape=jax.ShapeDtypeStruct(x.shape, F32),
        scratch_shapes=[pltpu.VMEM(w.shape, BF16) for w in (wg, wu, wd)],
        compiler_params=_params("arbitrary"),
        name="ffn",
    )(x, *[p.arr for p in picks])


def _mix_ffn_kernel(n_parts, head_scale, *refs):
    x_ref = refs[0]
    part_refs = refs[1:1 + n_parts]
    rest = refs[1 + n_parts:]
    if head_scale is not None:
        gh_ref, rest = rest[0], rest[1:]
    wo_ref, gm_ref, gf_ref, wg32_ref, wu32_ref, wd32_ref, o_ref, wg_ref, wu_ref, wd_ref = rest
    i = pl.program_id(0)
    _cast_slabs(i, [(wg32_ref, wg_ref), (wu32_ref, wu_ref), (wd32_ref, wd_ref)])
    pl.when(i >= W_CAST_STEPS)(functools.partial(
        _mix_ffn_tile, head_scale, x_ref, part_refs, gh_ref if head_scale is not None else None,
        wo_ref, gm_ref, gf_ref, wg_ref, wu_ref, wd_ref, o_ref))


def _mix_ffn_tile(head_scale, x_ref, part_refs, gh_ref, wo_ref, gm_ref, gf_ref, wg_ref, wu_ref, wd_ref, o_ref):
    gf = gf_ref[...]
    groups = _row_groups(MIX_TILE, MIX_ROW_GROUPS)
    xs = []
    for rs in groups:
        cols = []
        for p_ref in part_refs:
            if len(p_ref.shape) == 2:
                cols.append(p_ref[rs, :])
                continue
            for hd in range(p_ref.shape[0]):
                o_t = p_ref[hd, :, rs].astype(F32)
                if head_scale is not None:
                    o_t = o_t * lax.rsqrt(jnp.mean(o_t * o_t, axis=0, keepdims=True) + EPS)
                o = o_t.T
                if head_scale is not None:
                    o = o * (gh_ref[:, hd * LANES:(hd + 1) * LANES] * head_scale)
                cols.append(o.astype(BF16))
        m = _dot(jnp.concatenate(cols, axis=1), wo_ref[...])
        xs.append(x_ref[rs, :] + _rms(m, gm_ref[1:2]))
    outs = _ffn_apply(xs, gf[0:1], gf[1:2], wg_ref, wu_ref, wd_ref)
    for rs, o in zip(groups, outs):
        o_ref[rs, :] = o


def _mix_ffn_call(x, parts, wo, g_mix, g_ffn, wg, wu, wd, g_head=None, head_scale=None):
    t = x.shape[0]
    tile = pl.BlockSpec((MIX_TILE, D_MODEL), lambda i: (_token_tile(i), 0))
    part_specs = [pl.BlockSpec((MIX_TILE, p.shape[-1]), lambda i: (_token_tile(i), 0)) if p.ndim == 2
                  else pl.BlockSpec(p.shape[:2] + (MIX_TILE,), lambda i: (0, 0, _token_tile(i)))
                  for p in parts]
    picks = ([] if g_head is None else [g_head]) + [wo, g_mix, g_ffn, wg, wu, wd]
    return pl.pallas_call(
        functools.partial(_mix_ffn_kernel, len(parts), head_scale),
        grid=(W_CAST_STEPS + t // MIX_TILE,),
        in_specs=[tile] + part_specs + [p.spec for p in picks],
        out_specs=tile,
        out_shape=jax.ShapeDtypeStruct(x.shape, F32),
        scratch_shapes=[pltpu.VMEM(w.shape, BF16) for w in (wg, wu, wd)],
        compiler_params=_params("arbitrary"),
        name="mix_ffn",
    )(x, *parts, *[p.arr for p in picks])


def _even_pre_kernel(x_ref, g_ref, win_ref, gq_ref, wuq_ref, gkv_ref, wukv_ref,
                     c_ref, sa_ref, sb_ref, q_ref, k_ref, v_ref, zh_ref):
    o_kv = MLA_Q_RANK
    o_pe = o_kv + MLA_KV_RANK
    o_h = o_pe + LANES
    half = MLA_ROPE // 2
    scale = LOG2_E * (MLA_NOPE + MLA_ROPE) ** -0.5
    for rs in _row_groups(TOKEN_TILE, PRE_ROW_GROUPS):
        h = _rms(x_ref[rs, :], g_ref[0:1]).astype(BF16)
        z = _dot(h, win_ref[...])
        zh_ref[rs, :] = z[:, o_h:]
        c, sa, sb = c_ref[rs, :], sa_ref[rs, :], sb_ref[rs, :]

        cq = _rms(z[:, :o_kv], gq_ref[...]).astype(BF16)
        q = _dot(cq, wuq_ref[...]) * scale
        for hd in range(MLA_HEADS):
            base = hd * MLA_QK_PAD
            q_ref[rs, base:base + LANES] = q[:, base:base + LANES].astype(BF16)
            q_ref[rs, base + LANES:base + 2 * LANES] = _rope_slab(
                q[:, base + LANES:base + 2 * LANES], c, sa, sb, half).astype(BF16)

        ckv = _rms(z[:, o_kv:o_pe], gkv_ref[...]).astype(BF16)
        kv = _dot(ckv, wukv_ref[...])
        kpe = _rope_slab(z[:, o_pe:o_h], c, sa, sb, half).astype(BF16)
        for hd in range(MLA_HEADS):
            base = hd * MLA_QK_PAD
            k_ref[rs, base:base + LANES] = kv[:, hd * LANES:(hd + 1) * LANES].astype(BF16)
            k_ref[rs, base + LANES:base + 2 * LANES] = kpe
        _store_values(v_ref, rs, kv[:, MLA_HEADS * MLA_NOPE:], MLA_HEADS)


def _even_pre_call(x, g, win, gq, wuq, gkv, wukv, tabs):
    t = x.shape[0]
    picks = [g, win, gq, wuq, gkv, wukv]
    row = lambda w: pl.BlockSpec((TOKEN_TILE, w), lambda i: (i, 0))
    qk_w = MLA_HEADS * MLA_QK_PAD
    vt_spec, vt_shape = _vt_out(MLA_HEADS, t)
    zh_w = 4 * HG_HEADS * HG_DK
    return pl.pallas_call(
        _even_pre_kernel,
        grid=(t // TOKEN_TILE,),
        in_specs=[row(D_MODEL)] + [p.spec for p in picks] + [row(LANES), row(LANES), row(LANES)],
        out_specs=[row(qk_w), row(qk_w), vt_spec, row(zh_w)],
        out_shape=[jax.ShapeDtypeStruct((t, qk_w), BF16), jax.ShapeDtypeStruct((t, qk_w), BF16),
                   vt_shape, jax.ShapeDtypeStruct((t, zh_w), F32)],
        compiler_params=_params("parallel"),
        name="even_pre",
    )(x, *[p.arr for p in picks], *tabs)


def _odd_pre_kernel(x_ref, g_ref, win_ref, c_ref, sa_ref, sb_ref, q_ref, k_ref, v_ref):
    half = DF_ROT // 2
    width = DF_HEADS * 2 * DF_DH
    scale = LOG2_E * DF_DH ** -0.5
    for rs in _row_groups(TOKEN_TILE, PRE_ROW_GROUPS):
        h = _rms(x_ref[rs, :], g_ref[0:1]).astype(BF16)
        c, sa, sb = c_ref[rs, :], sa_ref[rs, :], sb_ref[rs, :]
        q = _dot(h, win_ref[:, :width]) * scale
        k = _dot(h, win_ref[:, width:2 * width])
        for j in range(width // LANES):
            sl = slice(j * LANES, (j + 1) * LANES)
            q_ref[rs, sl] = _rope_slab(q[:, sl], c, sa, sb, half).astype(BF16)
            k_ref[rs, sl] = _rope_slab(k[:, sl], c, sa, sb, half).astype(BF16)
        _store_values(v_ref, rs, _dot(h, win_ref[:, 2 * width:]), DF_HEADS)


def _odd_pre_call(x, g, win, tabs):
    t = x.shape[0]
    width = DF_HEADS * 2 * DF_DH
    row = lambda w: pl.BlockSpec((TOKEN_TILE, w), lambda i: (i, 0))
    vt_spec, vt_shape = _vt_out(DF_HEADS, t)
    return pl.pallas_call(
        _odd_pre_kernel,
        grid=(t // TOKEN_TILE,),
        in_specs=[row(D_MODEL), g.spec, win.spec, row(LANES), row(LANES), row(LANES)],
        out_specs=[row(width), row(width), vt_spec],
        out_shape=[jax.ShapeDtypeStruct((t, width), BF16), jax.ShapeDtypeStruct((t, width), BF16), vt_shape],
        compiler_params=_params("parallel"),
        name="odd_pre",
    )(x, g.arr, win.arr, *tabs)


def _flash(streams, q_chunk, k_ref, vt_ref, m_ref, acc_ref, row0, tq):
    m_rows = streams[0][0].shape[0]
    for i in range(len(streams)):
        m_ref[i] = jnp.full((1, m_rows), NEG_INF, F32)
        acc_ref[i] = jnp.zeros((V_ROWS, m_rows), F32)

    def step(blk, width, masked):
        start = pl.multiple_of(blk * KEY_BLOCK, KEY_BLOCK)
        scores = [_dot_nt(k_ref[0, pl.ds(start, width), ksl], q) for q, ksl, _ in streams]
        for i, (s, (_, _, hd)) in enumerate(zip(scores, streams)):
            if masked:
                k_chunk = (start + (width - tq) + lax.broadcasted_iota(jnp.int32, (tq, 1), 0)) // CHUNK
                tail = jnp.where(k_chunk <= q_chunk, s[width - tq:], NEG_INF)
                s = tail if width == tq else jnp.concatenate([s[:width - tq], tail], axis=0)
            m_i = m_ref[i]
            m_new = jnp.maximum(m_i, jnp.max(s, axis=0, keepdims=True))
            m_ref[i] = m_new
            p = jnp.exp2((s - m_new).astype(BF16))
            acc_ref[i] = jnp.exp2(m_i - m_new) * acc_ref[i] + _dot(vt_ref[hd, blk, :, :width], p)

    n_full = row0 // KEY_BLOCK

    def unmasked(blk, carry):
        step(blk, KEY_BLOCK, False)
        return carry

    lax.fori_loop(0, n_full, unmasked, 0)
    for p in range(KEY_BLOCK // tq):
        pl.when((row0 % KEY_BLOCK) // tq == p)(functools.partial(step, n_full, tq * (p + 1), True))
    outs = []
    for i in range(len(streams)):
        acc = acc_ref[i]
        outs.append(acc[:LANES] * (1.0 / acc[LANES:LANES + 1]))
    return outs


def _row_chunks(row0, n_rows):
    rows = row0 + lax.broadcasted_iota(jnp.int32, (1, n_rows), 1)
    return rows // CHUNK


def _flash_scratch(hp, m_rows):
    return [pltpu.VMEM((hp, 1, m_rows), F32), pltpu.VMEM((hp, V_ROWS, m_rows), F32)]


def _ot_spec(hp, tq, s):
    return pl.BlockSpec((hp, LANES, tq), lambda bi, h, i: (h, 0, bi * (s // tq) + i))


def _vt_spec(hp, s):
    return pl.BlockSpec((hp, s // KEY_BLOCK, V_ROWS, KEY_BLOCK), lambda bi, h, i: (h, bi, 0, 0))


def _mla_attn_kernel(q_ref, k_ref, vt_ref, o_ref, m_ref, acc_ref):
    row0 = pl.program_id(2) * MLA_Q_TILE
    streams = [(q_ref[0, :, h * MLA_QK_PAD:(h + 1) * MLA_QK_PAD],
                slice(h * MLA_QK_PAD, (h + 1) * MLA_QK_PAD), h)
               for h in range(ATTN_HEADS_PER_STEP)]
    outs = _flash(streams, _row_chunks(row0, MLA_Q_TILE), k_ref, vt_ref, m_ref, acc_ref, row0, MLA_Q_TILE)
    for h, o_t in enumerate(outs):
        o_ref[h] = o_t.astype(BF16)


def _mla_attn_call(q, k, vt):
    b, s, _ = q.shape
    hp = ATTN_HEADS_PER_STEP
    return pl.pallas_call(
        _mla_attn_kernel,
        grid=(b, MLA_HEADS // hp, s // MLA_Q_TILE),
        in_specs=[pl.BlockSpec((1, MLA_Q_TILE, hp * MLA_QK_PAD), lambda bi, h, i: (bi, i, h)),
                  pl.BlockSpec((1, s, hp * MLA_QK_PAD), lambda bi, h, i: (bi, 0, h)),
                  _vt_spec(hp, s)],
        out_specs=_ot_spec(hp, MLA_Q_TILE, s),
        out_shape=jax.ShapeDtypeStruct((MLA_HEADS, MLA_V, b * s), BF16),
        scratch_shapes=_flash_scratch(hp, MLA_Q_TILE),
        compiler_params=_params("parallel", "parallel", "arbitrary"),
        name="mla_attn",
    )(q, k, vt)


def _diff_attn_kernel(lambda_init, q_ref, k_ref, vt_ref, lam_ref, o_ref, m_ref, acc_ref):
    row0 = pl.program_id(2) * DF_Q_TILE
    dv = 2 * DF_DH
    lane = lax.broadcasted_iota(jnp.int32, (1, dv), 1)
    streams = []
    for h in range(ATTN_HEADS_PER_STEP):
        hs = slice(h * dv, (h + 1) * dv)
        q = q_ref[0, :, hs]
        zero = jnp.zeros_like(q)
        q2 = jnp.concatenate([jnp.where(lane < DF_DH, q, zero), jnp.where(lane >= DF_DH, q, zero)], axis=0)
        streams.append((q2, hs, h))
    chunks = _row_chunks(row0, DF_Q_TILE)
    outs = _flash(streams, jnp.concatenate([chunks, chunks], axis=1), k_ref, vt_ref, m_ref, acc_ref, row0,
                  DF_Q_TILE)
    lp = lam_ref[...]
    lam = (jnp.exp(jnp.sum(lp[0:1] * lp[1:2], axis=-1, keepdims=True))
           - jnp.exp(jnp.sum(lp[2:3] * lp[3:4], axis=-1, keepdims=True)) + lambda_init)
    for h, o2_t in enumerate(outs):
        o_ref[h] = (o2_t[:, :DF_Q_TILE] - lam * o2_t[:, DF_Q_TILE:]).astype(BF16)


def _diff_attn_call(q, k, vt, lam_p, lambda_init):
    b, s, _ = q.shape
    hp = ATTN_HEADS_PER_STEP
    w = hp * 2 * DF_DH
    return pl.pallas_call(
        functools.partial(_diff_attn_kernel, lambda_init),
        grid=(b, DF_HEADS // hp, s // DF_Q_TILE),
        in_specs=[pl.BlockSpec((1, DF_Q_TILE, w), lambda bi, h, i: (bi, i, h)),
                  pl.BlockSpec((1, s, w), lambda bi, h, i: (bi, 0, h)),
                  _vt_spec(hp, s),
                  lam_p.spec],
        out_specs=_ot_spec(hp, DF_Q_TILE, s),
        out_shape=jax.ShapeDtypeStruct((DF_HEADS, 2 * DF_DH, b * s), BF16),
        scratch_shapes=_flash_scratch(hp, 2 * DF_Q_TILE),
        compiler_params=_params("parallel", "parallel", "arbitrary"),
        name="diff_attn",
    )(q, k, vt, lam_p.arr)


def _hgrn_kernel(hq_ref, hf_ref, hi_ref, hg_ref, lb_ref, go_ref, o_ref,
                 st_ref, b_ref, kk_ref, bpad_ref, kpad_ref, vpad_ref):
    n_chunks = HG_SEQ_TILE // CHUNK
    mid = CHUNK // 2 - 1

    @pl.when(pl.program_id(1) == 0)
    def _():
        st_ref[...] = jnp.zeros_like(st_ref)

    row_in_chunk = lax.broadcasted_iota(jnp.int32, (HG_GATE_ROWS, 1), 0) % CHUNK
    for hd in range(HG_HEADS):
        ls = slice(hd * HG_DK, (hd + 1) * HG_DK)
        lb = lb_ref[:, ls]
        for r0 in range(0, HG_SEQ_TILE, HG_GATE_ROWS):
            rs = slice(r0, r0 + HG_GATE_ROWS)
            sig = jax.nn.sigmoid(hf_ref[rs, ls])
            kk_ref[hd, rs, :] = (1.0 - lb) * (1.0 - sig)
            b = jnp.log(jnp.maximum(lb + (1.0 - lb) * sig, TINY))
            step = 1
            while step < CHUNK:
                b = b + jnp.where(row_in_chunk >= step, pltpu.roll(b, step, 0), 0.0)
                step *= 2
            b_ref[hd, rs, :] = b

    b_mid = b_ref[:, pl.ds(mid, n_chunks, stride=CHUNK), :]
    b_end = b_ref[:, pl.ds(CHUNK - 1, n_chunks, stride=CHUNK), :]
    worst = jnp.max(jnp.maximum(-b_mid, b_mid - b_end))

    row = lax.broadcasted_iota(jnp.int32, (CHUNK, 1), 0)

    def load(ci, hd):
        rows = pl.ds(pl.multiple_of(ci * CHUNK, CHUNK), CHUNK)
        ls = slice(hd * HG_DK, (hd + 1) * HG_DK)
        return rows, ls, hq_ref[rows, ls], hi_ref[rows, ls], b_ref[hd, rows, :], kk_ref[hd, rows, :]

    def carry_state(hd, qh, vh, b, kk):
        st = st_ref[hd]
        o = _dot_nt((qh * jnp.exp(b)).astype(BF16), st.astype(BF16))
        b_last = b[CHUNK - 1:CHUNK]
        kdec = kk * jnp.exp(b_last - b)
        st_ref[hd] = st * jnp.exp(b_last) + _dot_tn(vh.astype(BF16), kdec.astype(BF16))
        return o

    def finish(rows, ls, o):
        on = _rms(o, go_ref[:, ls]) * jax.nn.silu(hg_ref[rows, ls])
        o_ref[rows, ls] = on.astype(BF16)

    @pl.when(worst < HG_SAFE_LOG)
    def _():
        causal = row >= lax.broadcasted_iota(jnp.int32, (1, CHUNK), 1)

        def chunk_body(cj, carry):
            for u in range(HG_UNROLL):
                ci = cj * HG_UNROLL + u
                for hd in range(HG_HEADS):
                    rows, ls, qh, vh, b, kk = load(ci, hd)
                    o = carry_state(hd, qh, vh, b, kk)
                    b_m = b[mid:mid + 1]
                    qf = (qh * jnp.exp(b - b_m)).astype(BF16)
                    kf = (kk * jnp.exp(b_m - b)).astype(BF16)
                    a = jnp.where(causal, _dot_nt(qf, kf), 0.0)
                    finish(rows, ls, o + _dot(a.astype(BF16), vh.astype(BF16)))
            return carry

        lax.fori_loop(0, n_chunks // HG_UNROLL, chunk_body, 0)

    @pl.when(jnp.logical_not(worst < HG_SAFE_LOG))
    def _():
        n_sub = CHUNK // HG_SUB
        off_w = HG_SUB * (n_sub * (n_sub - 1) // 2)
        zpad = jnp.zeros((HG_HEADS, HG_SUB, HG_DK), F32)
        bpad_ref[:, :HG_SUB, :] = zpad
        kpad_ref[:, :HG_SUB, :] = zpad
        vpad_ref[:, :HG_SUB, :] = zpad
        row_sub = row % HG_SUB
        r2 = lax.broadcasted_iota(jnp.int32, (2 * HG_DK, 2 * HG_DK), 0) // HG_DK
        c2 = lax.broadcasted_iota(jnp.int32, (2 * HG_DK, 2 * HG_DK), 1) // HG_DK
        ones2 = jnp.where(r2 == c2, 1.0, 0.0).astype(BF16)
        col = lax.broadcasted_iota(jnp.int32, (1, off_w), 1)
        col_blk = jnp.zeros((1, off_w), jnp.int32)
        for i in range(1, n_sub):
            col_blk = col_blk + jnp.where(col >= HG_SUB * (i * (i - 1) // 2), 1, 0)
        off_mask = col_blk == (row // HG_SUB)

        def chunk_body(ci, carry):
            for hd in range(HG_HEADS):
                rows, ls, qh, vh, b, kk = load(ci, hd)
                o = carry_state(hd, qh, vh, b, kk)

                refs = [b[i * HG_SUB - 1:i * HG_SUB] for i in range(1, n_sub)]
                bref = jnp.concatenate(
                    [jnp.zeros((HG_SUB, HG_DK), F32)]
                    + [jnp.broadcast_to(r, (HG_SUB, HG_DK)) for r in refs], axis=0)
                qs = qh * jnp.exp(b - bref)
                kst = jnp.concatenate(
                    [kk[:i * HG_SUB] * jnp.exp(refs[i - 1] - b[:i * HG_SUB]) for i in range(1, n_sub)],
                    axis=0)
                vst = jnp.concatenate([vh[:i * HG_SUB] for i in range(1, n_sub)], axis=0)
                a_off = jnp.where(off_mask, _dot_nt(qs.astype(BF16), kst.astype(BF16)), 0.0)
                o = o + _dot(a_off.astype(BF16), vst.astype(BF16))

                bpad_ref[hd, HG_SUB:, :] = b
                kpad_ref[hd, HG_SUB:, :] = kk
                vpad_ref[hd, HG_SUB:, :] = vh
                for dp in range(HG_SUB // 2):
                    terms = []
                    for d in (2 * dp, 2 * dp + 1):
                        lo = HG_SUB - d
                        w = qh * jnp.exp(b - bpad_ref[hd, lo:lo + CHUNK, :]) * kpad_ref[hd, lo:lo + CHUNK, :]
                        terms.append(jnp.where(row_sub >= d, w, 0.0))
                    dsum = _dot(jnp.concatenate(terms, axis=1).astype(BF16), ones2)
                    for j, d in enumerate((2 * dp, 2 * dp + 1)):
                        lo = HG_SUB - d
                        o = o + dsum[:, j * HG_DK:(j + 1) * HG_DK] * vpad_ref[hd, lo:lo + CHUNK, :]
                finish(rows, ls, o)
            return carry

        lax.fori_loop(0, n_chunks, chunk_body, 0)


def _hgrn_call(zh, lb, g_out, b, s):
    width = HG_HEADS * HG_DK
    n_seq = s // HG_SEQ_TILE
    part = lambda j: pl.BlockSpec((HG_SEQ_TILE, width), lambda bi, si: (bi * n_seq + si, j))
    pad = pltpu.VMEM((HG_HEADS, HG_SUB + CHUNK, HG_DK), F32)
    return pl.pallas_call(
        _hgrn_kernel,
        grid=(b, n_seq),
        in_specs=[part(0), part(1), part(2), part(3), lb.spec, g_out.spec],
        out_specs=pl.BlockSpec((HG_SEQ_TILE, width), lambda bi, si: (bi * n_seq + si, 0)),
        out_shape=jax.ShapeDtypeStruct((b * s, width), BF16),
        scratch_shapes=[pltpu.VMEM((HG_HEADS, HG_DV, HG_DK), F32),
                        pltpu.VMEM((HG_HEADS, HG_SEQ_TILE, HG_DK), F32),
                        pltpu.VMEM((HG_HEADS, HG_SEQ_TILE, HG_DK), F32),
                        pad, pad, pad],
        compiler_params=_params("parallel", "arbitrary"),
        name="hgrn2",
    )(zh, zh, zh, zh, lb.arr, g_out.arr)


def _rope_tables(positions, dim, group):
    half = dim // 2
    inv_freq = ROPE_THETA ** (-jnp.arange(0, dim, 2, dtype=F32) / dim)
    ang = positions.astype(F32).reshape(-1, 1) * inv_freq
    cos, sin = jnp.cos(ang), jnp.sin(ang)
    t = ang.shape[0]
    rest = group - dim
    c = jnp.concatenate([cos, cos, jnp.ones((t, rest), F32)], axis=1)
    sa = jnp.concatenate([-sin, jnp.zeros((t, half + rest), F32)], axis=1)
    sb = jnp.concatenate([jnp.zeros((t, half), F32), sin, jnp.zeros((t, rest), F32)], axis=1)
    rep = LANES // group
    return tuple(jnp.tile(a, (1, rep)) for a in (c, sa, sb))


def kernel(x, positions, norm_g, ffn_w_gate, ffn_w_up, ffn_w_down, ev_w_in, ev_g_q, ev_w_uq, ev_g_kv, ev_w_ukv, ev_lb_logits, ev_g_out, ev_w_out, od_w_in, od_lambda, od_g_head, od_w_out):
    b, s, d = x.shape
    t = b * s
    xt = x.reshape(t, d)
    tabs_m = _rope_tables(positions, MLA_ROPE, LANES)
    tabs_d = _rope_tables(positions, DF_ROT, DF_DH)
    lb_w = jax.nn.softmax(ev_lb_logits.astype(F32), axis=0)
    lb_all = jnp.cumsum(lb_w, axis=0) - lb_w[0:1]

    n_even = ev_w_in.shape[0]
    o_pe = MLA_Q_RANK + MLA_KV_RANK + MLA_ROPE
    w_in_e = jnp.concatenate(
        [ev_w_in[..., :o_pe].astype(BF16), jnp.zeros((n_even, d, LANES - MLA_ROPE), BF16),
         ev_w_in[..., o_pe:].astype(BF16)], axis=-1)
    w_uq = ev_w_uq.astype(BF16).reshape(n_even, MLA_Q_RANK, MLA_HEADS, MLA_NOPE + MLA_ROPE)
    w_uq = jnp.pad(w_uq, ((0, 0), (0, 0), (0, 0), (0, MLA_QK_PAD - MLA_NOPE - MLA_ROPE)))
    w_uq = w_uq.reshape(n_even, MLA_Q_RANK, MLA_HEADS * MLA_QK_PAD)
    w_ukv = ev_w_ukv.astype(BF16).reshape(n_even, MLA_KV_RANK, MLA_HEADS, 2, MLA_NOPE)
    w_ukv = w_ukv.transpose(0, 1, 3, 2, 4).reshape(n_even, MLA_KV_RANK, 2 * MLA_HEADS * MLA_NOPE)
    w_out_e = ev_w_out.astype(BF16)
    w_in_o = od_w_in.astype(BF16)
    w_out_o = od_w_out.astype(BF16)
    g_q = ev_g_q.reshape(n_even, 1, -1)
    g_kv = ev_g_kv.reshape(n_even, 1, -1)
    g_out = ev_g_out.reshape(n_even, 1, -1)
    lb_all = lb_all.reshape(n_even, 1, -1)
    g_head_o = od_g_head.reshape(od_g_head.shape[0], 1, -1)

    for l in range(DEPTH):
        xt = _ffn_call(xt, _Pick(norm_g, l, 0), _Slabs(ffn_w_gate, l, 0), _Slabs(ffn_w_up, l, 0),
                       _Slabs(ffn_w_down, l, 0))
        j = l // 2
        g_mix = _Pick(norm_g, l, 1)
        if l % 2 == 0:
            q, k, v, zh = _even_pre_call(
                xt, g_mix, _Pick(w_in_e, j), _Pick(g_q, j), _Pick(w_uq, j), _Pick(g_kv, j),
                _Pick(w_ukv, j), tabs_m)
            o_a = _mla_attn_call(q.reshape(b, s, -1), k.reshape(b, s, -1), v)
            o_b = _hgrn_call(zh, _Pick(lb_all, j), _Pick(g_out, j), b, s)
            parts = [o_a, o_b]
            head_norm = {}
            w_out = _Pick(w_out_e, j)
        else:
            lambda_init = 0.8 - 0.6 * math.exp(-0.3 * l)
            q, k, v = _odd_pre_call(xt, g_mix, _Pick(w_in_o, j), tabs_d)
            o = _diff_attn_call(q.reshape(b, s, -1), k.reshape(b, s, -1), v, _Pick(od_lambda, j),
                                lambda_init)
            parts = [o]
            head_norm = dict(g_head=_Pick(g_head_o, j), head_scale=1.0 - lambda_init)
            w_out = _Pick(w_out_o, j)
        xt = _mix_ffn_call(xt, parts, w_out, g_mix, _Pick(norm_g, l, 2),
                           _Slabs(ffn_w_gate, l, 1), _Slabs(ffn_w_up, l, 1), _Slabs(ffn_w_down, l, 1),
                           **head_norm)
    return xt.reshape(b, s, d)
```

```python
import functools
import math

import jax
import jax.numpy as jnp
from jax import lax
from jax.experimental import pallas as pl
from jax.experimental.pallas import tpu as pltpu

D_MODEL = 1024
DEPTH = 4
CHUNK = 64
ROPE_THETA = 500000.0
EPS = 1e-6
NEG_INF = -1e30
LOG2_E = math.log2(math.e)
TINY = 1e-30
D_FF = 2816
MLA_HEADS = 4
MLA_NOPE = 128
MLA_ROPE = 64
MLA_V = 128
MLA_Q_RANK = 384
MLA_KV_RANK = 256
HG_HEADS = 4
HG_DK = 128
HG_DV = 128
DF_HEADS = 8
DF_DH = 64
DF_ROT = DF_DH // 4

LANES = 128
MXU_DIM = 256
VMEM_LIMIT = 56 * 1024 * 1024

TOKEN_TILE = 1024
MIX_TILE = 512
MLA_Q_TILE = 512
DF_Q_TILE = 256
KEY_BLOCK = 1024
ATTN_HEADS_PER_STEP = 4
HG_SEQ_TILE = 1024
HG_SUB = 16
HG_GATE_ROWS = 128
HG_UNROLL = 4
HG_SAFE_LOG = 75.0
W_CAST_STEPS = 8
MIX_ROW_GROUPS = 2
FFN_ROW_GROUPS = 4
PRE_ROW_GROUPS = 2
FF_CHUNKS =((0, 1024), (1024, 2048), (2048, D_FF))

MLA_QK_PAD = 2 * LANES
V_ROWS = LANES + 16
EVEN_Z = MLA_Q_RANK + MLA_KV_RANK + LANES + 4 * HG_HEADS * HG_DK

F32 = jnp.float32
BF16 = jnp.bfloat16


def _rms(x, g):
    ms = jnp.mean(x * x, axis=-1, keepdims=True)
    return x * lax.rsqrt(ms + EPS) * g


def _dot(a, b):
    return jnp.dot(a, b, preferred_element_type=F32)


def _dot_nt(a, b):
    return lax.dot_general(a, b, (((1,), (1,)), ((), ())), preferred_element_type=F32)


def _dot_tn(a, b):
    return lax.dot_general(a, b, (((0,), (0,)), ((), ())), preferred_element_type=F32)


def _rope_slab(x, c, sa, sb, half):
    return x * c + pltpu.roll(x, LANES - half, 1) * sa + pltpu.roll(x, half, 1) * sb


def _store_values(vt_ref, rs, v, heads):
    ones = jnp.ones((V_ROWS - LANES, v.shape[0]), BF16)
    for hd in range(heads):
        vt_ref[hd, :LANES, rs] = v[:, hd * LANES:(hd + 1) * LANES].T.astype(BF16)
        vt_ref[hd, LANES:, rs] = ones


def _vt_out(heads, t):
    per_blk = KEY_BLOCK // TOKEN_TILE
    spec = pl.BlockSpec((heads, None, V_ROWS, TOKEN_TILE), lambda i: (0, i // per_blk, 0, i % per_blk))
    return spec, jax.ShapeDtypeStruct((heads, t // KEY_BLOCK, V_ROWS, KEY_BLOCK), BF16)


class _Slabs:
    def __init__(self, arr, layer, half):
        self.arr = arr
        rows, cols = arr.shape[2:]
        self.shape = (rows, cols)
        self.spec = pl.BlockSpec((None, None, rows // W_CAST_STEPS, cols),
                                 lambda i: (layer, half, jnp.minimum(i, W_CAST_STEPS - 1), 0))


def _cast_slabs(i, pairs):
    for j in range(W_CAST_STEPS):
        @pl.when(i == j)
        def _():
            for src, dst in pairs:
                rows = src.shape[0]
                dst[j * rows:(j + 1) * rows, :] = src[...].astype(BF16)


def _token_tile(i):
    return jnp.maximum(i - W_CAST_STEPS, 0)


class _Pick:
    def __init__(self, arr, *idx):
        self.arr = arr
        nd = arr.ndim - len(idx)
        self.spec = pl.BlockSpec((None,) * len(idx) + arr.shape[len(idx):],
                                 lambda *_: idx + (0,) * nd, pipeline_mode=pl.Buffered(1))


def _params(*sem):
    return pltpu.CompilerParams(dimension_semantics=sem, vmem_limit_bytes=VMEM_LIMIT)


def _ffn_apply(xs, gpre, gpost, wg_ref, wu_ref, wd_ref):
    xn = [_rms(x, gpre).astype(BF16) for x in xs]
    acc = [None] * len(xs)
    for lo, hi in FF_CHUNKS:
        for i in range(len(xs)):
            g = _dot(xn[i], wg_ref[:, lo:hi])
            u = _dot(xn[i], wu_ref[:, lo:hi])
            a = (jax.nn.silu(g) * u).astype(BF16)
            h = _dot(a, wd_ref[lo:hi, :])
            acc[i] = h if acc[i] is None else acc[i] + h
    return [x + 0.5 * _rms(a, gpost) for x, a in zip(xs, acc)]


def _row_groups(rows, n):
    step = rows // n
    return [slice(i * step, (i + 1) * step) for i in range(n)]


def _ffn_kernel(x_ref, g_ref, wg32_ref, wu32_ref, wd32_ref, o_ref, wg_ref, wu_ref, wd_ref):
    i = pl.program_id(0)
    _cast_slabs(i, [(wg32_ref, wg_ref), (wu32_ref, wu_ref), (wd32_ref, wd_ref)])

    @pl.when(i >= W_CAST_STEPS)
    def _():
        g = g_ref[...]
        groups = _row_groups(TOKEN_TILE, FFN_ROW_GROUPS)
        outs = _ffn_apply([x_ref[rs, :] for rs in groups], g[0:1], g[1:2], wg_ref, wu_ref, wd_ref)
        for rs, o in zip(groups, outs):
            o_ref[rs, :] = o


def _ffn_call(x, g2, wg, wu, wd):
    t = x.shape[0]
    tile = pl.BlockSpec((TOKEN_TILE, D_MODEL), lambda i: (_token_tile(i), 0))
    picks = [g2, wg, wu, wd]
    return pl.pallas_call(
        _ffn_kernel,
        grid=(W_CAST_STEPS + t // TOKEN_TILE,),
        in_specs=[tile] + [p.spec for p in picks],
        out_specs=tile,
        out_shape=jax.ShapeDtypeStruct(x.shape, F32),
        scratch_shapes=[pltpu.VMEM(w.shape, BF16) for w in (wg, wu, wd)],
        compiler_params=_params("arbitrary"),
        name="ffn",
    )(x, *[p.arr for p in picks])


def _mix_ffn_kernel(n_parts, head_scale, *refs):
    x_ref = refs[0]
    part_refs = refs[1:1 + n_parts]
    rest = refs[1 + n_parts:]
    if head_scale is not None:
        gh_ref, rest = rest[0], rest[1:]
    wo_ref, gm_ref, gf_ref, wg32_ref, wu32_ref, wd32_ref, o_ref, wg_ref, wu_ref, wd_ref = rest
    i = pl.program_id(0)
    _cast_slabs(i, [(wg32_ref, wg_ref), (wu32_ref, wu_ref), (wd32_ref, wd_ref)])
    pl.when(i >= W_CAST_STEPS)(functools.partial(
        _mix_ffn_tile, head_scale, x_ref, part_refs, gh_ref if head_scale is not None else None,
        wo_ref, gm_ref, gf_ref, wg_ref, wu_ref, wd_ref, o_ref))


def _mix_ffn_tile(head_scale, x_ref, part_refs, gh_ref, wo_ref, gm_ref, gf_ref, wg_ref, wu_ref, wd_ref, o_ref):
    gf = gf_ref[...]
    groups = _row_groups(MIX_TILE, MIX_ROW_GROUPS)
    xs = []
    for rs in groups:
        cols = []
        for p_ref in part_refs:
            if len(p_ref.shape) == 2:
                cols.append(p_ref[rs, :])
                continue
            for hd in range(p_ref.shape[0]):
                o_t = p_ref[hd, :, rs].astype(F32)
                if head_scale is not None:
                    o_t = o_t * lax.rsqrt(jnp.mean(o_t * o_t, axis=0, keepdims=True) + EPS)
                o = o_t.T
                if head_scale is not None:
                    o = o * (gh_ref[:, hd * LANES:(hd + 1) * LANES] * head_scale)
                cols.append(o.astype(BF16))
        m = _dot(jnp.concatenate(cols, axis=1), wo_ref[...])
        xs.append(x_ref[rs, :] + _rms(m, gm_ref[1:2]))
    outs = _ffn_apply(xs, gf[0:1], gf[1:2], wg_ref, wu_ref, wd_ref)
    for rs, o in zip(groups, outs):
        o_ref[rs, :] = o


def _mix_ffn_call(x, parts, wo, g_mix, g_ffn, wg, wu, wd, g_head=None, head_scale=None):
    t = x.shape[0]
    tile = pl.BlockSpec((MIX_TILE, D_MODEL), lambda i: (_token_tile(i), 0))
    part_specs = [pl.BlockSpec((MIX_TILE, p.shape[-1]), lambda i: (_token_tile(i), 0)) if p.ndim == 2
                  else pl.BlockSpec(p.shape[:2] + (MIX_TILE,), lambda i: (0, 0, _token_tile(i)))
                  for p in parts]
    picks = ([] if g_head is None else [g_head]) + [wo, g_mix, g_ffn, wg, wu, wd]
    return pl.pallas_call(
        functools.partial(_mix_ffn_kernel, len(parts), head_scale),
        grid=(W_CAST_STEPS + t // MIX_TILE,),
        in_specs=[tile] + part_specs + [p.spec for p in picks],
        out_specs=tile,
        out_shape=jax.ShapeDtypeStruct(x.shape, F32),
        scratch_shapes=[pltpu.VMEM(w.shape, BF16) for w in (wg, wu, wd)],
        compiler_params=_params("arbitrary"),
        name="mix_ffn",
    )(x, *parts, *[p.arr for p in picks])


def _even_pre_kernel(x_ref, g_ref, win_ref, gq_ref, wuq_ref, gkv_ref, wukv_ref,
                     c_ref, sa_ref, sb_ref, q_ref, k_ref, v_ref, zh_ref):
    o_kv = MLA_Q_RANK
    o_pe = o_kv + MLA_KV_RANK
    o_h = o_pe + LANES
    half = MLA_ROPE // 2
    scale = LOG2_E * (MLA_NOPE + MLA_ROPE) ** -0.5
    for rs in _row_groups(TOKEN_TILE, PRE_ROW_GROUPS):
        h = _rms(x_ref[rs, :], g_ref[0:1]).astype(BF16)
        z = _dot(h, win_ref[...])
        zh_ref[rs, :] = z[:, o_h:]
        c, sa, sb = c_ref[rs, :], sa_ref[rs, :], sb_ref[rs, :]

        cq = _rms(z[:, :o_kv], gq_ref[...]).astype(BF16)
        q = _dot(cq, wuq_ref[...]) * scale
        for hd in range(MLA_HEADS):
            base = hd * MLA_QK_PAD
            q_ref[rs, base:base + LANES] = q[:, base:base + LANES].astype(BF16)
            q_ref[rs, base + LANES:base + 2 * LANES] = _rope_slab(
                q[:, base + LANES:base + 2 * LANES], c, sa, sb, half).astype(BF16)

        ckv = _rms(z[:, o_kv:o_pe], gkv_ref[...]).astype(BF16)
        kv = _dot(ckv, wukv_ref[...])
        kpe = _rope_slab(z[:, o_pe:o_h], c, sa, sb, half).astype(BF16)
        for hd in range(MLA_HEADS):
            base = hd * MLA_QK_PAD
            k_ref[rs, base:base + LANES] = kv[:, hd * LANES:(hd + 1) * LANES].astype(BF16)
            k_ref[rs, base + LANES:base + 2 * LANES] = kpe
        _store_values(v_ref, rs, kv[:, MLA_HEADS * MLA_NOPE:], MLA_HEADS)


def _even_pre_call(x, g, win, gq, wuq, gkv, wukv, tabs):
    t = x.shape[0]
    picks = [g, win, gq, wuq, gkv, wukv]
    row = lambda w: pl.BlockSpec((TOKEN_TILE, w), lambda i: (i, 0))
    qk_w = MLA_HEADS * MLA_QK_PAD
    vt_spec, vt_shape = _vt_out(MLA_HEADS, t)
    zh_w = 4 * HG_HEADS * HG_DK
    return pl.pallas_call(
        _even_pre_kernel,
        grid=(t // TOKEN_TILE,),
        in_specs=[row(D_MODEL)] + [p.spec for p in picks] + [row(LANES), row(LANES), row(LANES)],
        out_specs=[row(qk_w), row(qk_w), vt_spec, row(zh_w)],
        out_shape=[jax.ShapeDtypeStruct((t, qk_w), BF16), jax.ShapeDtypeStruct((t, qk_w), BF16),
                   vt_shape, jax.ShapeDtypeStruct((t, zh_w), F32)],
        compiler_params=_params("parallel"),
        name="even_pre",
    )(x, *[p.arr for p in picks], *tabs)


def _odd_pre_kernel(x_ref, g_ref, win_ref, c_ref, sa_ref, sb_ref, q_ref, k_ref, v_ref):
    half = DF_ROT // 2
    width = DF_HEADS * 2 * DF_DH
    scale = LOG2_E * DF_DH ** -0.5
    for rs in _row_groups(TOKEN_TILE, PRE_ROW_GROUPS):
        h = _rms(x_ref[rs, :], g_ref[0:1]).astype(BF16)
        c, sa, sb = c_ref[rs, :], sa_ref[rs, :], sb_ref[rs, :]
        q = _dot(h, win_ref[:, :width]) * scale
        k = _dot(h, win_ref[:, width:2 * width])
        for j in range(width // LANES):
            sl = slice(j * LANES, (j + 1) * LANES)
            q_ref[rs, sl] = _rope_slab(q[:, sl], c, sa, sb, half).astype(BF16)
            k_ref[rs, sl] = _rope_slab(k[:, sl], c, sa, sb, half).astype(BF16)
        _store_values(v_ref, rs, _dot(h, win_ref[:, 2 * width:]), DF_HEADS)


def _odd_pre_call(x, g, win, tabs):
    t = x.shape[0]
    width = DF_HEADS * 2 * DF_DH
    row = lambda w: pl.BlockSpec((TOKEN_TILE, w), lambda i: (i, 0))
    vt_spec, vt_shape = _vt_out(DF_HEADS, t)
    return pl.pallas_call(
        _odd_pre_kernel,
        grid=(t // TOKEN_TILE,),
        in_specs=[row(D_MODEL), g.spec, win.spec, row(LANES), row(LANES), row(LANES)],
        out_specs=[row(width), row(width), vt_spec],
        out_shape=[jax.ShapeDtypeStruct((t, width), BF16), jax.ShapeDtypeStruct((t, width), BF16), vt_shape],
        compiler_params=_params("parallel"),
        name="odd_pre",
    )(x, g.arr, win.arr, *tabs)


def _flash(streams, q_chunk, k_ref, vt_ref, m_ref, acc_ref, row0, tq):
    m_rows = streams[0][0].shape[0]
    for i in range(len(streams)):
        m_ref[i] = jnp.full((1, m_rows), NEG_INF, F32)
        acc_ref[i] = jnp.zeros((V_ROWS, m_rows), F32)

    def step(blk, width, masked):
        start = pl.multiple_of(blk * KEY_BLOCK, KEY_BLOCK)
        scores = [_dot_nt(k_ref[0, pl.ds(start, width), ksl], q) for q, ksl, _ in streams]
        for i, (s, (_, _, hd)) in enumerate(zip(scores, streams)):
            if masked:
                k_chunk = (start + (width - tq) + lax.broadcasted_iota(jnp.int32, (tq, 1), 0)) // CHUNK
                tail = jnp.where(k_chunk <= q_chunk, s[width - tq:], NEG_INF)
                s = tail if width == tq else jnp.concatenate([s[:width - tq], tail], axis=0)
            m_i = m_ref[i]
            m_new = jnp.maximum(m_i, jnp.max(s, axis=0, keepdims=True))
            m_ref[i] = m_new
            p = jnp.exp2((s - m_new).astype(BF16))
            acc_ref[i] = jnp.exp2(m_i - m_new) * acc_ref[i] + _dot(vt_ref[hd, blk, :, :width], p)

    n_full = row0 // KEY_BLOCK

    def unmasked(blk, carry):
        step(blk, KEY_BLOCK, False)
        return carry

    lax.fori_loop(0, n_full, unmasked, 0)
    for p in range(KEY_BLOCK // tq):
        pl.when((row0 % KEY_BLOCK) // tq == p)(functools.partial(step, n_full, tq * (p + 1), True))
    outs = []
    for i in range(len(streams)):
        acc = acc_ref[i]
        outs.append(acc[:LANES] * (1.0 / acc[LANES:LANES + 1]))
    return outs


def _row_chunks(row0, n_rows):
    rows = row0 + lax.broadcasted_iota(jnp.int32, (1, n_rows), 1)
    return rows // CHUNK


def _flash_scratch(hp, m_rows):
    return [pltpu.VMEM((hp, 1, m_rows), F32), pltpu.VMEM((hp, V_ROWS, m_rows), F32)]


def _ot_spec(hp, tq, s):
    return pl.BlockSpec((hp, LANES, tq), lambda bi, h, i: (h, 0, bi * (s // tq) + i))


def _vt_spec(hp, s):
    return pl.BlockSpec((hp, s // KEY_BLOCK, V_ROWS, KEY_BLOCK), lambda bi, h, i: (h, bi, 0, 0))


def _mla_attn_kernel(q_ref, k_ref, vt_ref, o_ref, m_ref, acc_ref):
    row0 = pl.program_id(2) * MLA_Q_TILE
    streams = [(q_ref[0, :, h * MLA_QK_PAD:(h + 1) * MLA_QK_PAD],
                slice(h * MLA_QK_PAD, (h + 1) * MLA_QK_PAD), h)
               for h in range(ATTN_HEADS_PER_STEP)]
    outs = _flash(streams, _row_chunks(row0, MLA_Q_TILE), k_ref, vt_ref, m_ref, acc_ref, row0, MLA_Q_TILE)
    for h, o_t in enumerate(outs):
        o_ref[h] = o_t.astype(BF16)


def _mla_attn_call(q, k, vt):
    b, s, _ = q.shape
    hp = ATTN_HEADS_PER_STEP
    return pl.pallas_call(
        _mla_attn_kernel,
        grid=(b, MLA_HEADS // hp, s // MLA_Q_TILE),
        in_specs=[pl.BlockSpec((1, MLA_Q_TILE, hp * MLA_QK_PAD), lambda bi, h, i: (bi, i, h)),
                  pl.BlockSpec((1, s, hp * MLA_QK_PAD), lambda bi, h, i: (bi, 0, h)),
                  _vt_spec(hp, s)],
        out_specs=_ot_spec(hp, MLA_Q_TILE, s),
        out_shape=jax.ShapeDtypeStruct((MLA_HEADS, MLA_V, b * s), BF16),
        scratch_shapes=_flash_scratch(hp, MLA_Q_TILE),
        compiler_params=_params("parallel", "parallel", "arbitrary"),
        name="mla_attn",
    )(q, k, vt)


def _diff_attn_kernel(lambda_init, q_ref, k_ref, vt_ref, lam_ref, o_ref, m_ref, acc_ref):
    row0 = pl.program_id(2) * DF_Q_TILE
    dv = 2 * DF_DH
    lane = lax.broadcasted_iota(jnp.int32, (1, dv), 1)
    streams = []
    for h in range(ATTN_HEADS_PER_STEP):
        hs = slice(h * dv, (h + 1) * dv)
        q = q_ref[0, :, hs]
        zero = jnp.zeros_like(q)
        q2 = jnp.concatenate([jnp.where(lane < DF_DH, q, zero), jnp.where(lane >= DF_DH, q, zero)], axis=0)
        streams.append((q2, hs, h))
    chunks = _row_chunks(row0, DF_Q_TILE)
    outs = _flash(streams, jnp.concatenate([chunks, chunks], axis=1), k_ref, vt_ref, m_ref, acc_ref, row0,
                  DF_Q_TILE)
    lp = lam_ref[...]
    lam = (jnp.exp(jnp.sum(lp[0:1] * lp[1:2], axis=-1, keepdims=True))
           - jnp.exp(jnp.sum(lp[2:3] * lp[3:4], axis=-1, keepdims=True)) + lambda_init)
    for h, o2_t in enumerate(outs):
        o_ref[h] = (o2_t[:, :DF_Q_TILE] - lam * o2_t[:, DF_Q_TILE:]).astype(BF16)


def _diff_attn_call(q, k, vt, lam_p, lambda_init):
    b, s, _ = q.shape
    hp = ATTN_HEADS_PER_STEP
    w = hp * 2 * DF_DH
    return pl.pallas_call(
        functools.partial(_diff_attn_kernel, lambda_init),
        grid=(b, DF_HEADS // hp, s // DF_Q_TILE),
        in_specs=[pl.BlockSpec((1, DF_Q_TILE, w), lambda bi, h, i: (bi, i, h)),
                  pl.BlockSpec((1, s, w), lambda bi, h, i: (bi, 0, h)),
                  _vt_spec(hp, s),
                  lam_p.spec],
        out_specs=_ot_spec(hp, DF_Q_TILE, s),
        out_shape=jax.ShapeDtypeStruct((DF_HEADS, 2 * DF_DH, b * s), BF16),
        scratch_shapes=_flash_scratch(hp, 2 * DF_Q_TILE),
        compiler_params=_params("parallel", "parallel", "arbitrary"),
        name="diff_attn",
    )(q, k, vt, lam_p.arr)


def _hgrn_kernel(hq_ref, hf_ref, hi_ref, hg_ref, lb_ref, go_ref, o_ref,
                 st_ref, b_ref, kk_ref, bpad_ref, kpad_ref, vpad_ref):
    n_chunks = HG_SEQ_TILE // CHUNK
    mid = CHUNK // 2 - 1

    @pl.when(pl.program_id(1) == 0)
    def _():
        st_ref[...] = jnp.zeros_like(st_ref)

    row_in_chunk = lax.broadcasted_iota(jnp.int32, (HG_GATE_ROWS, 1), 0) % CHUNK
    for hd in range(HG_HEADS):
        ls = slice(hd * HG_DK, (hd + 1) * HG_DK)
        lb = lb_ref[:, ls]
        for r0 in range(0, HG_SEQ_TILE, HG_GATE_ROWS):
            rs = slice(r0, r0 + HG_GATE_ROWS)
            sig = jax.nn.sigmoid(hf_ref[rs, ls])
            kk_ref[hd, rs, :] = (1.0 - lb) * (1.0 - sig)
            b = jnp.log(jnp.maximum(lb + (1.0 - lb) * sig, TINY))
            step = 1
            while step < CHUNK:
                b = b + jnp.where(row_in_chunk >= step, pltpu.roll(b, step, 0), 0.0)
                step *= 2
            b_ref[hd, rs, :] = b

    b_mid = b_ref[:, pl.ds(mid, n_chunks, stride=CHUNK), :]
    b_end = b_ref[:, pl.ds(CHUNK - 1, n_chunks, stride=CHUNK), :]
    worst = jnp.max(jnp.maximum(-b_mid, b_mid - b_end))

    row = lax.broadcasted_iota(jnp.int32, (CHUNK, 1), 0)

    def load(ci, hd):
        rows = pl.ds(pl.multiple_of(ci * CHUNK, CHUNK), CHUNK)
        ls = slice(hd * HG_DK, (hd + 1) * HG_DK)
        return rows, ls, hq_ref[rows, ls], hi_ref[rows, ls], b_ref[hd, rows, :], kk_ref[hd, rows, :]

    def carry_state(hd, qh, vh, b, kk):
        st = st_ref[hd]
        o = _dot_nt((qh * jnp.exp(b)).astype(BF16), st.astype(BF16))
        b_last = b[CHUNK - 1:CHUNK]
        kdec = kk * jnp.exp(b_last - b)
        st_ref[hd] = st * jnp.exp(b_last) + _dot_tn(vh.astype(BF16), kdec.astype(BF16))
        return o

    def finish(rows, ls, o):
        on = _rms(o, go_ref[:, ls]) * jax.nn.silu(hg_ref[rows, ls])
        o_ref[rows, ls] = on.astype(BF16)

    @pl.when(worst < HG_SAFE_LOG)
    def _():
        causal = row >= lax.broadcasted_iota(jnp.int32, (1, CHUNK), 1)

        def chunk_body(cj, carry):
            for u in range(HG_UNROLL):
                ci = cj * HG_UNROLL + u
                for hd in range(HG_HEADS):
                    rows, ls, qh, vh, b, kk = load(ci, hd)
                    o = carry_state(hd, qh, vh, b, kk)
                    b_m = b[mid:mid + 1]
                    qf = (qh * jnp.exp(b - b_m)).astype(BF16)
                    kf = (kk * jnp.exp(b_m - b)).astype(BF16)
                    a = jnp.where(causal, _dot_nt(qf, kf), 0.0)
                    finish(rows, ls, o + _dot(a.astype(BF16), vh.astype(BF16)))
            return carry

        lax.fori_loop(0, n_chunks // HG_UNROLL, chunk_body, 0)

    @pl.when(jnp.logical_not(worst < HG_SAFE_LOG))
    def _():
        n_sub = CHUNK // HG_SUB
        off_w = HG_SUB * (n_sub * (n_sub - 1) // 2)
        zpad = jnp.zeros((HG_HEADS, HG_SUB, HG_DK), F32)
        bpad_ref[:, :HG_SUB, :] = zpad
        kpad_ref[:, :HG_SUB, :] = zpad
        vpad_ref[:, :HG_SUB, :] = zpad
        row_sub = row % HG_SUB
        r2 = lax.broadcasted_iota(jnp.int32, (2 * HG_DK, 2 * HG_DK), 0) // HG_DK
        c2 = lax.broadcasted_iota(jnp.int32, (2 * HG_DK, 2 * HG_DK), 1) // HG_DK
        ones2 = jnp.where(r2 == c2, 1.0, 0.0).astype(BF16)
        col = lax.broadcasted_iota(jnp.int32, (1, off_w), 1)
        col_blk = jnp.zeros((1, off_w), jnp.int32)
        for i in range(1, n_sub):
            col_blk = col_blk + jnp.where(col >= HG_SUB * (i * (i - 1) // 2), 1, 0)
        off_mask = col_blk == (row // HG_SUB)

        def chunk_body(ci, carry):
            for hd in range(HG_HEADS):
                rows, ls, qh, vh, b, kk = load(ci, hd)
                o = carry_state(hd, qh, vh, b, kk)

                refs = [b[i * HG_SUB - 1:i * HG_SUB] for i in range(1, n_sub)]
                bref = jnp.concatenate(
                    [jnp.zeros((HG_SUB, HG_DK), F32)]
                    + [jnp.broadcast_to(r, (HG_SUB, HG_DK)) for r in refs], axis=0)
                qs = qh * jnp.exp(b - bref)
                kst = jnp.concatenate(
                    [kk[:i * HG_SUB] * jnp.exp(refs[i - 1] - b[:i * HG_SUB]) for i in range(1, n_sub)],
                    axis=0)
                vst = jnp.concatenate([vh[:i * HG_SUB] for i in range(1, n_sub)], axis=0)
                a_off = jnp.where(off_mask, _dot_nt(qs.astype(BF16), kst.astype(BF16)), 0.0)
                o = o + _dot(a_off.astype(BF16), vst.astype(BF16))

                bpad_ref[hd, HG_SUB:, :] = b
                kpad_ref[hd, HG_SUB:, :] = kk
                vpad_ref[hd, HG_SUB:, :] = vh
                for dp in range(HG_SUB // 2):
                    terms = []
                    for d in (2 * dp, 2 * dp + 1):
                        lo = HG_SUB - d
                        w = qh * jnp.exp(b - bpad_ref[hd, lo:lo + CHUNK, :]) * kpad_ref[hd, lo:lo + CHUNK, :]
                        terms.append(jnp.where(row_sub >= d, w, 0.0))
                    dsum = _dot(jnp.concatenate(terms, axis=1).astype(BF16), ones2)
                    for j, d in enumerate((2 * dp, 2 * dp + 1)):
                        lo = HG_SUB - d
                        o = o + dsum[:, j * HG_DK:(j + 1) * HG_DK] * vpad_ref[hd, lo:lo + CHUNK, :]
                finish(rows, ls, o)
            return carry

        lax.fori_loop(0, n_chunks, chunk_body, 0)


def _hgrn_call(zh, lb, g_out, b, s):
    width = HG_HEADS * HG_DK
    n_seq = s // HG_SEQ_TILE
    part = lambda j: pl.BlockSpec((HG_SEQ_TILE, width), lambda bi, si: (bi * n_seq + si, j))
    pad = pltpu.VMEM((HG_HEADS, HG_SUB + CHUNK, HG_DK), F32)
    return pl.pallas_call(
        _hgrn_kernel,
        grid=(b, n_seq),
        in_specs=[part(0), part(1), part(2), part(3), lb.spec, g_out.spec],
        out_specs=pl.BlockSpec((HG_SEQ_TILE, width), lambda bi, si: (bi * n_seq + si, 0)),
        out_shape=jax.ShapeDtypeStruct((b * s, width), BF16),
        scratch_shapes=[pltpu.VMEM((HG_HEADS, HG_DV, HG_DK), F32),
                        pltpu.VMEM((HG_HEADS, HG_SEQ_TILE, HG_DK), F32),
                        pltpu.VMEM((HG_HEADS, HG_SEQ_TILE, HG_DK), F32),
                        pad, pad, pad],
        compiler_params=_params("parallel", "arbitrary"),
        name="hgrn2",
    )(zh, zh, zh, zh, lb.arr, g_out.arr)


def _rope_tables(positions, dim, group):
    half = dim // 2
    inv_freq = ROPE_THETA ** (-jnp.arange(0, dim, 2, dtype=F32) / dim)
    ang = positions.astype(F32).reshape(-1, 1) * inv_freq
    cos, sin = jnp.cos(ang), jnp.sin(ang)
    t = ang.shape[0]
    rest = group - dim
    c = jnp.concatenate([cos, cos, jnp.ones((t, rest), F32)], axis=1)
    sa = jnp.concatenate([-sin, jnp.zeros((t, half + rest), F32)], axis=1)
    sb = jnp.concatenate([jnp.zeros((t, half), F32), sin, jnp.zeros((t, rest), F32)], axis=1)
    rep = LANES // group
    return tuple(jnp.tile(a, (1, rep)) for a in (c, sa, sb))


def kernel(x, positions, norm_g, ffn_w_gate, ffn_w_up, ffn_w_down, ev_w_in, ev_g_q, ev_w_uq, ev_g_kv, ev_w_ukv, ev_lb_logits, ev_g_out, ev_w_out, od_w_in, od_lambda, od_g_head, od_w_out):
    b, s, d = x.shape
    t = b * s
    xt = x.reshape(t, d)
    tabs_m = _rope_tables(positions, MLA_ROPE, LANES)
    tabs_d = _rope_tables(positions, DF_ROT, DF_DH)
    lb_w = jax.nn.softmax(ev_lb_logits.astype(F32), axis=0)
    lb_all = jnp.cumsum(lb_w, axis=0) - lb_w[0:1]

    n_even = ev_w_in.shape[0]
    o_pe = MLA_Q_RANK + MLA_KV_RANK + MLA_ROPE
    w_in_e = jnp.concatenate(
        [ev_w_in[..., :o_pe].astype(BF16), jnp.zeros((n_even, d, LANES - MLA_ROPE), BF16),
         ev_w_in[..., o_pe:].astype(BF16)], axis=-1)
    w_uq = ev_w_uq.astype(BF16).reshape(n_even, MLA_Q_RANK, MLA_HEADS, MLA_NOPE + MLA_ROPE)
    w_uq = jnp.pad(w_uq, ((0, 0), (0, 0), (0, 0), (0, MLA_QK_PAD - MLA_NOPE - MLA_ROPE)))
    w_uq = w_uq.reshape(n_even, MLA_Q_RANK, MLA_HEADS * MLA_QK_PAD)
    w_ukv = ev_w_ukv.astype(BF16).reshape(n_even, MLA_KV_RANK, MLA_HEADS, 2, MLA_NOPE)
    w_ukv = w_ukv.transpose(0, 1, 3, 2, 4).reshape(n_even, MLA_KV_RANK, 2 * MLA_HEADS * MLA_NOPE)
    w_out_e = ev_w_out.astype(BF16)
    w_in_o = od_w_in.astype(BF16)
    w_out_o = od_w_out.astype(BF16)
    g_q = ev_g_q.reshape(n_even, 1, -1)
    g_kv = ev_g_kv.reshape(n_even, 1, -1)
    g_out = ev_g_out.reshape(n_even, 1, -1)
    lb_all = lb_all.reshape(n_even, 1, -1)
    g_head_o = od_g_head.reshape(od_g_head.shape[0], 1, -1)

    for l in range(DEPTH):
        xt = _ffn_call(xt, _Pick(norm_g, l, 0), _Slabs(ffn_w_gate, l, 0), _Slabs(ffn_w_up, l, 0),
                       _Slabs(ffn_w_down, l, 0))
        j = l // 2
        g_mix = _Pick(norm_g, l, 1)
        if l % 2 == 0:
            q, k, v, zh = _even_pre_call(
                xt, g_mix, _Pick(w_in_e, j), _Pick(g_q, j), _Pick(w_uq, j), _Pick(g_kv, j),
                _Pick(w_ukv, j), tabs_m)
            o_a = _mla_attn_call(q.reshape(b, s, -1), k.reshape(b, s, -1), v)
            o_b = _hgrn_call(zh, _Pick(lb_all, j), _Pick(g_out, j), b, s)
            parts = [o_a, o_b]
            head_norm = {}
            w_out = _Pick(w_out_e, j)
        else:
            lambda_init = 0.8 - 0.6 * math.exp(-0.3 * l)
            q, k, v = _odd_pre_call(xt, g_mix, _Pick(w_in_o, j), tabs_d)
            o = _diff_attn_call(q.reshape(b, s, -1), k.reshape(b, s, -1), v, _Pick(od_lambda, j),
                                lambda_init)
            parts = [o]
            head_norm = dict(g_head=_Pick(g_head_o, j), head_scale=1.0 - lambda_init)
            w_out = _Pick(w_out_o, j)
        xt = _mix_ffn_call(xt, parts, w_out, g_mix, _Pick(norm_g, l, 2),
                           _Slabs(ffn_w_gate, l, 1), _Slabs(ffn_w_up, l, 1), _Slabs(ffn_w_down, l, 1),
                           **head_norm)
    return xt.reshape(b, s, d)
```

```python
import functools
import math

import jax
import jax.numpy as jnp
from jax import lax
from jax.experimental import pallas as pl
from jax.experimental.pallas import tpu as pltpu

D_MODEL = 1024
DEPTH = 4
CHUNK = 64
ROPE_THETA = 500000.0
EPS = 1e-6
NEG_INF = -1e30
LOG2_E = math.log2(math.e)
TINY = 1e-30
D_FF = 2816
MLA_HEADS = 4
MLA_NOPE = 128
MLA_ROPE = 64
MLA_V = 128
MLA_Q_RANK = 384
MLA_KV_RANK = 256
HG_HEADS = 4
HG_DK = 128
HG_DV = 128
DF_HEADS = 8
DF_DH = 64
DF_ROT = DF_DH // 4

LANES = 128
MXU_DIM = 256
VMEM_LIMIT = 56 * 1024 * 1024

TOKEN_TILE = 1024
MIX_TILE = 512
MLA_Q_TILE = 512
DF_Q_TILE = 256
KEY_BLOCK = 1024
ATTN_HEADS_PER_STEP = 4
HG_SEQ_TILE = 1024
HG_SUB = 16
HG_GATE_ROWS = 128
HG_UNROLL = 4
HG_SAFE_LOG = 75.0
W_CAST_STEPS = 8
MIX_ROW_GROUPS = 2
FFN_ROW_GROUPS = 4
PRE_ROW_GROUPS = 2
FF_CHUNKS =((0, 1024), (1024, 2048), (2048, D_FF))

MLA_QK_PAD = 2 * LANES
V_ROWS = LANES + 16
EVEN_Z = MLA_Q_RANK + MLA_KV_RANK + LANES + 4 * HG_HEADS * HG_DK

F32 = jnp.float32
BF16 = jnp.bfloat16


def _rms(x, g):
    ms = jnp.mean(x * x, axis=-1, keepdims=True)
    return x * lax.rsqrt(ms + EPS) * g


def _dot(a, b):
    return jnp.dot(a, b, preferred_element_type=F32)


def _dot_nt(a, b):
    return lax.dot_general(a, b, (((1,), (1,)), ((), ())), preferred_element_type=F32)


def _dot_tn(a, b):
    return lax.dot_general(a, b, (((0,), (0,)), ((), ())), preferred_element_type=F32)


def _rope_slab(x, c, sa, sb, half):
    return x * c + pltpu.roll(x, LANES - half, 1) * sa + pltpu.roll(x, half, 1) * sb


def _store_values(vt_ref, rs, v, heads):
    ones = jnp.ones((V_ROWS - LANES, v.shape[0]), BF16)
    for hd in range(heads):
        vt_ref[hd, :LANES, rs] = v[:, hd * LANES:(hd + 1) * LANES].T.astype(BF16)
        vt_ref[hd, LANES:, rs] = ones


def _vt_out(heads, t):
    per_blk = KEY_BLOCK // TOKEN_TILE
    spec = pl.BlockSpec((heads, None, V_ROWS, TOKEN_TILE), lambda i: (0, i // per_blk, 0, i % per_blk))
    return spec, jax.ShapeDtypeStruct((heads, t // KEY_BLOCK, V_ROWS, KEY_BLOCK), BF16)


class _Slabs:
    def __init__(self, arr, layer, half):
        self.arr = arr
        rows, cols = arr.shape[2:]
        self.shape = (rows, cols)
        self.spec = pl.BlockSpec((None, None, rows // W_CAST_STEPS, cols),
                                 lambda i: (layer, half, jnp.minimum(i, W_CAST_STEPS - 1), 0))


def _cast_slabs(i, pairs):
    for j in range(W_CAST_STEPS):
        @pl.when(i == j)
        def _():
            for src, dst in pairs:
                rows = src.shape[0]
                dst[j * rows:(j + 1) * rows, :] = src[...].astype(BF16)


def _token_tile(i):
    return jnp.maximum(i - W_CAST_STEPS, 0)


class _Pick:
    def __init__(self, arr, *idx):
        self.arr = arr
        nd = arr.ndim - len(idx)
        self.spec = pl.BlockSpec((None,) * len(idx) + arr.shape[len(idx):],
                                 lambda *_: idx + (0,) * nd, pipeline_mode=pl.Buffered(1))


def _params(*sem):
    return pltpu.CompilerParams(dimension_semantics=sem, vmem_limit_bytes=VMEM_LIMIT)


def _ffn_apply(xs, gpre, gpost, wg_ref, wu_ref, wd_ref):
    xn = [_rms(x, gpre).astype(BF16) for x in xs]
    acc = [None] * len(xs)
    for lo, hi in FF_CHUNKS:
        for i in range(len(xs)):
            g = _dot(xn[i], wg_ref[:, lo:hi])
            u = _dot(xn[i], wu_ref[:, lo:hi])
            a = (jax.nn.silu(g) * u).astype(BF16)
            h = _dot(a, wd_ref[lo:hi, :])
            acc[i] = h if acc[i] is None else acc[i] + h
    return [x + 0.5 * _rms(a, gpost) for x, a in zip(xs, acc)]


def _row_groups(rows, n):
    step = rows // n
    return [slice(i * step, (i + 1) * step) for i in range(n)]


def _ffn_kernel(x_ref, g_ref, wg32_ref, wu32_ref, wd32_ref, o_ref, wg_ref, wu_ref, wd_ref):
    i = pl.program_id(0)
    _cast_slabs(i, [(wg32_ref, wg_ref), (wu32_ref, wu_ref), (wd32_ref, wd_ref)])

    @pl.when(i >= W_CAST_STEPS)
    def _():
        g = g_ref[...]
        groups = _row_groups(TOKEN_TILE, FFN_ROW_GROUPS)
        outs = _ffn_apply([x_ref[rs, :] for rs in groups], g[0:1], g[1:2], wg_ref, wu_ref, wd_ref)
        for rs, o in zip(groups, outs):
            o_ref[rs, :] = o


def _ffn_call(x, g2, wg, wu, wd):
    t = x.shape[0]
    tile = pl.BlockSpec((TOKEN_TILE, D_MODEL), lambda i: (_token_tile(i), 0))
    picks = [g2, wg, wu, wd]
    return pl.pallas_call(
        _ffn_kernel,
        grid=(W_CAST_STEPS + t // TOKEN_TILE,),
        in_specs=[tile] + [p.spec for p in picks],
        out_specs=tile,
        out_shape=jax.ShapeDtypeStruct(x.shape, F32),
        scratch_shapes=[pltpu.VMEM(w.shape, BF16) for w in (wg, wu, wd)],
        compiler_params=_params("arbitrary"),
        name="ffn",
    )(x, *[p.arr for p in picks])


def _mix_ffn_kernel(n_parts, head_scale, *refs):
    x_ref = refs[0]
    part_refs = refs[1:1 + n_parts]
    rest = refs[1 + n_parts:]
    if head_scale is not None:
        gh_ref, rest = rest[0], rest[1:]
    wo_ref, gm_ref, gf_ref, wg32_ref, wu32_ref, wd32_ref, o_ref, wg_ref, wu_ref, wd_ref = rest
    i = pl.program_id(0)
    _cast_slabs(i, [(wg32_ref, wg_ref), (wu32_ref, wu_ref), (wd32_ref, wd_ref)])
    pl.when(i >= W_CAST_STEPS)(functools.partial(
        _mix_ffn_tile, head_scale, x_ref, part_refs, gh_ref if head_scale is not None else None,
        wo_ref, gm_ref, gf_ref, wg_ref, wu_ref, wd_ref, o_ref))


def _mix_ffn_tile(head_scale, x_ref, part_refs, gh_ref, wo_ref, gm_ref, gf_ref, wg_ref, wu_ref, wd_ref, o_ref):
    gf = gf_ref[...]
    groups = _row_groups(MIX_TILE, MIX_ROW_GROUPS)
    xs = []
    for rs in groups:
        cols = []
        for p_ref in part_refs:
            if len(p_ref.shape) == 2:
                cols.append(p_ref[rs, :])
                continue
            for hd in range(p_ref.shape[0]):
                o_t = p_ref[hd, :, rs].astype(F32)
                if head_scale is not None:
                    o_t = o_t * lax.rsqrt(jnp.mean(o_t * o_t, axis=0, keepdims=True) + EPS)
                o = o_t.T
                if head_scale is not None:
                    o = o * (gh_ref[:, hd * LANES:(hd + 1) * LANES] * head_scale)
                cols.append(o.astype(BF16))
        m = _dot(jnp.concatenate(cols, axis=1), wo_ref[...])
        xs.append(x_ref[rs, :] + _rms(m, gm_ref[1:2]))
    outs = _ffn_apply(xs, gf[0:1], gf[1:2], wg_ref, wu_ref, wd_ref)
    for rs, o in zip(groups, outs):
        o_ref[rs, :] = o


def _mix_ffn_call(x, parts, wo, g_mix, g_ffn, wg, wu, wd, g_head=None, head_scale=None):
    t = x.shape[0]
    tile = pl.BlockSpec((MIX_TILE, D_MODEL), lambda i: (_token_tile(i), 0))
    part_specs = [pl.BlockSpec((MIX_TILE, p.shape[-1]), lambda i: (_token_tile(i), 0)) if p.ndim == 2
                  else pl.BlockSpec(p.shape[:2] + (MIX_TILE,), lambda i: (0, 0, _token_tile(i)))
                  for p in parts]
    picks = ([] if g_head is None else [g_head]) + [wo, g_mix, g_ffn, wg, wu, wd]
    return pl.pallas_call(
        functools.partial(_mix_ffn_kernel, len(parts), head_scale),
        grid=(W_CAST_STEPS + t // MIX_TILE,),
        in_specs=[tile] + part_specs + [p.spec for p in picks],
        out_specs=tile,
        out_shape=jax.ShapeDtypeStruct(x.shape, F32),
        scratch_shapes=[pltpu.VMEM(w.shape, BF16) for w in (wg, wu, wd)],
        compiler_params=_params("arbitrary"),
        name="mix_ffn",
    )(x, *parts, *[p.arr for p in picks])


def _even_pre_kernel(x_ref, g_ref, win_ref, gq_ref, wuq_ref, gkv_ref, wukv_ref, lb_ref,
                     c_ref, sa_ref, sb_ref, q_ref, k_ref, v_ref, zh_ref, kk_ref, b_ref):
    o_kv = MLA_Q_RANK
    o_pe = o_kv + MLA_KV_RANK
    o_h = o_pe + LANES
    half = MLA_ROPE // 2
    scale = LOG2_E * (MLA_NOPE + MLA_ROPE) ** -0.5
    for rs in _row_groups(TOKEN_TILE, PRE_ROW_GROUPS):
        h = _rms(x_ref[rs, :], g_ref[0:1]).astype(BF16)
        z = _dot(h, win_ref[...])
        hw = HG_HEADS * HG_DK
        zh_ref[rs, :hw] = z[:, o_h:o_h + hw]
        zh_ref[rs, hw:] = z[:, o_h + 2 * hw:]
        _hgrn_gates(z[:, o_h + hw:o_h + 2 * hw], lb_ref, kk_ref, b_ref, rs)
        c, sa, sb = c_ref[rs, :], sa_ref[rs, :], sb_ref[rs, :]

        cq = _rms(z[:, :o_kv], gq_ref[...]).astype(BF16)
        q = _dot(cq, wuq_ref[...]) * scale
        for hd in range(MLA_HEADS):
            base = hd * MLA_QK_PAD
            q_ref[rs, base:base + LANES] = q[:, base:base + LANES].astype(BF16)
            q_ref[rs, base + LANES:base + 2 * LANES] = _rope_slab(
                q[:, base + LANES:base + 2 * LANES], c, sa, sb, half).astype(BF16)

        ckv = _rms(z[:, o_kv:o_pe], gkv_ref[...]).astype(BF16)
        kv = _dot(ckv, wukv_ref[...])
        kpe = _rope_slab(z[:, o_pe:o_h], c, sa, sb, half).astype(BF16)
        for hd in range(MLA_HEADS):
            base = hd * MLA_QK_PAD
            k_ref[rs, base:base + LANES] = kv[:, hd * LANES:(hd + 1) * LANES].astype(BF16)
            k_ref[rs, base + LANES:base + 2 * LANES] = kpe
        _store_values(v_ref, rs, kv[:, MLA_HEADS * MLA_NOPE:], MLA_HEADS)


def _even_pre_call(x, g, win, gq, wuq, gkv, wukv, lb, tabs):
    t = x.shape[0]
    picks = [g, win, gq, wuq, gkv, wukv, lb]
    row = lambda w: pl.BlockSpec((TOKEN_TILE, w), lambda i: (i, 0))
    qk_w = MLA_HEADS * MLA_QK_PAD
    vt_spec, vt_shape = _vt_out(MLA_HEADS, t)
    zh_w = 3 * HG_HEADS * HG_DK
    gate_spec = pl.BlockSpec((HG_HEADS, TOKEN_TILE, HG_DK), lambda i: (0, i, 0))
    gate_shape = jax.ShapeDtypeStruct((HG_HEADS, t, HG_DK), F32)
    return pl.pallas_call(
        _even_pre_kernel,
        grid=(t // TOKEN_TILE,),
        in_specs=[row(D_MODEL)] + [p.spec for p in picks] + [row(LANES), row(LANES), row(LANES)],
        out_specs=[row(qk_w), row(qk_w), vt_spec, row(zh_w), gate_spec, gate_spec],
        out_shape=[jax.ShapeDtypeStruct((t, qk_w), BF16), jax.ShapeDtypeStruct((t, qk_w), BF16),
                   vt_shape, jax.ShapeDtypeStruct((t, zh_w), F32), gate_shape, gate_shape],
        compiler_params=_params("parallel"),
        name="even_pre",
    )(x, *[p.arr for p in picks], *tabs)


def _odd_pre_kernel(x_ref, g_ref, win_ref, c_ref, sa_ref, sb_ref, q_ref, k_ref, v_ref):
    half = DF_ROT // 2
    width = DF_HEADS * 2 * DF_DH
    scale = LOG2_E * DF_DH ** -0.5
    for rs in _row_groups(TOKEN_TILE, PRE_ROW_GROUPS):
        h = _rms(x_ref[rs, :], g_ref[0:1]).astype(BF16)
        c, sa, sb = c_ref[rs, :], sa_ref[rs, :], sb_ref[rs, :]
        q = _dot(h, win_ref[:, :width]) * scale
        k = _dot(h, win_ref[:, width:2 * width])
        for j in range(width // LANES):
            sl = slice(j * LANES, (j + 1) * LANES)
            q_ref[rs, sl] = _rope_slab(q[:, sl], c, sa, sb, half).astype(BF16)
            k_ref[rs, sl] = _rope_slab(k[:, sl], c, sa, sb, half).astype(BF16)
        _store_values(v_ref, rs, _dot(h, win_ref[:, 2 * width:]), DF_HEADS)


def _odd_pre_call(x, g, win, tabs):
    t = x.shape[0]
    width = DF_HEADS * 2 * DF_DH
    row = lambda w: pl.BlockSpec((TOKEN_TILE, w), lambda i: (i, 0))
    vt_spec, vt_shape = _vt_out(DF_HEADS, t)
    return pl.pallas_call(
        _odd_pre_kernel,
        grid=(t // TOKEN_TILE,),
        in_specs=[row(D_MODEL), g.spec, win.spec, row(LANES), row(LANES), row(LANES)],
        out_specs=[row(width), row(width), vt_spec],
        out_shape=[jax.ShapeDtypeStruct((t, width), BF16), jax.ShapeDtypeStruct((t, width), BF16), vt_shape],
        compiler_params=_params("parallel"),
        name="odd_pre",
    )(x, g.arr, win.arr, *tabs)


def _flash(streams, q_chunk, k_ref, vt_ref, m_ref, acc_ref, row0, tq):
    m_rows = streams[0][0].shape[0]
    for i in range(len(streams)):
        m_ref[i] = jnp.full((1, m_rows), NEG_INF, F32)
        acc_ref[i] = jnp.zeros((V_ROWS, m_rows), F32)

    def step(blk, width, masked):
        start = pl.multiple_of(blk * KEY_BLOCK, KEY_BLOCK)
        scores = [_dot_nt(k_ref[0, pl.ds(start, width), ksl], q) for q, ksl, _ in streams]
        for i, (s, (_, _, hd)) in enumerate(zip(scores, streams)):
            if masked:
                k_chunk = (start + (width - tq) + lax.broadcasted_iota(jnp.int32, (tq, 1), 0)) // CHUNK
                tail = jnp.where(k_chunk <= q_chunk, s[width - tq:], NEG_INF)
                s = tail if width == tq else jnp.concatenate([s[:width - tq], tail], axis=0)
            m_i = m_ref[i]
            m_new = jnp.maximum(m_i, jnp.max(s, axis=0, keepdims=True))
            m_ref[i] = m_new
            p = jnp.exp2((s - m_new).astype(BF16))
            acc_ref[i] = jnp.exp2(m_i - m_new) * acc_ref[i] + _dot(vt_ref[hd, blk, :, :width], p)

    n_full = row0 // KEY_BLOCK

    def unmasked(blk, carry):
        step(blk, KEY_BLOCK, False)
        return carry

    lax.fori_loop(0, n_full, unmasked, 0)
    for p in range(KEY_BLOCK // tq):
        pl.when((row0 % KEY_BLOCK) // tq == p)(functools.partial(step, n_full, tq * (p + 1), True))
    outs = []
    for i in range(len(streams)):
        acc = acc_ref[i]
        outs.append(acc[:LANES] * (1.0 / acc[LANES:LANES + 1]))
    return outs


def _row_chunks(row0, n_rows):
    rows = row0 + lax.broadcasted_iota(jnp.int32, (1, n_rows), 1)
    return rows // CHUNK


def _flash_scratch(hp, m_rows):
    return [pltpu.VMEM((hp, 1, m_rows), F32), pltpu.VMEM((hp, V_ROWS, m_rows), F32)]


def _ot_spec(hp, tq, s):
    return pl.BlockSpec((hp, LANES, tq), lambda bi, h, i: (h, 0, bi * (s // tq) + i))


def _vt_spec(hp, s):
    return pl.BlockSpec((hp, s // KEY_BLOCK, V_ROWS, KEY_BLOCK), lambda bi, h, i: (h, bi, 0, 0))


def _mla_attn_kernel(q_ref, k_ref, vt_ref, o_ref, m_ref, acc_ref):
    row0 = pl.program_id(2) * MLA_Q_TILE
    streams = [(q_ref[0, :, h * MLA_QK_PAD:(h + 1) * MLA_QK_PAD],
                slice(h * MLA_QK_PAD, (h + 1) * MLA_QK_PAD), h)
               for h in range(ATTN_HEADS_PER_STEP)]
    outs = _flash(streams, _row_chunks(row0, MLA_Q_TILE), k_ref, vt_ref, m_ref, acc_ref, row0, MLA_Q_TILE)
    for h, o_t in enumerate(outs):
        o_ref[h] = o_t.astype(BF16)


def _mla_attn_call(q, k, vt):
    b, s, _ = q.shape
    hp = ATTN_HEADS_PER_STEP
    return pl.pallas_call(
        _mla_attn_kernel,
        grid=(b, MLA_HEADS // hp, s // MLA_Q_TILE),
        in_specs=[pl.BlockSpec((1, MLA_Q_TILE, hp * MLA_QK_PAD), lambda bi, h, i: (bi, i, h)),
                  pl.BlockSpec((1, s, hp * MLA_QK_PAD), lambda bi, h, i: (bi, 0, h)),
                  _vt_spec(hp, s)],
        out_specs=_ot_spec(hp, MLA_Q_TILE, s),
        out_shape=jax.ShapeDtypeStruct((MLA_HEADS, MLA_V, b * s), BF16),
        scratch_shapes=_flash_scratch(hp, MLA_Q_TILE),
        compiler_params=_params("parallel", "parallel", "arbitrary"),
        name="mla_attn",
    )(q, k, vt)


def _diff_attn_kernel(lambda_init, q_ref, k_ref, vt_ref, lam_ref, o_ref, m_ref, acc_ref):
    row0 = pl.program_id(2) * DF_Q_TILE
    dv = 2 * DF_DH
    lane = lax.broadcasted_iota(jnp.int32, (1, dv), 1)
    streams = []
    for h in range(ATTN_HEADS_PER_STEP):
        hs = slice(h * dv, (h + 1) * dv)
        q = q_ref[0, :, hs]
        zero = jnp.zeros_like(q)
        q2 = jnp.concatenate([jnp.where(lane < DF_DH, q, zero), jnp.where(lane >= DF_DH, q, zero)], axis=0)
        streams.append((q2, hs, h))
    chunks = _row_chunks(row0, DF_Q_TILE)
    outs = _flash(streams, jnp.concatenate([chunks, chunks], axis=1), k_ref, vt_ref, m_ref, acc_ref, row0,
                  DF_Q_TILE)
    lp = lam_ref[...]
    lam = (jnp.exp(jnp.sum(lp[0:1] * lp[1:2], axis=-1, keepdims=True))
           - jnp.exp(jnp.sum(lp[2:3] * lp[3:4], axis=-1, keepdims=True)) + lambda_init)
    for h, o2_t in enumerate(outs):
        o_ref[h] = (o2_t[:, :DF_Q_TILE] - lam * o2_t[:, DF_Q_TILE:]).astype(BF16)


def _diff_attn_call(q, k, vt, lam_p, lambda_init):
    b, s, _ = q.shape
    hp = ATTN_HEADS_PER_STEP
    w = hp * 2 * DF_DH
    return pl.pallas_call(
        functools.partial(_diff_attn_kernel, lambda_init),
        grid=(b, DF_HEADS // hp, s // DF_Q_TILE),
        in_specs=[pl.BlockSpec((1, DF_Q_TILE, w), lambda bi, h, i: (bi, i, h)),
                  pl.BlockSpec((1, s, w), lambda bi, h, i: (bi, 0, h)),
                  _vt_spec(hp, s),
                  lam_p.spec],
        out_specs=_ot_spec(hp, DF_Q_TILE, s),
        out_shape=jax.ShapeDtypeStruct((DF_HEADS, 2 * DF_DH, b * s), BF16),
        scratch_shapes=_flash_scratch(hp, 2 * DF_Q_TILE),
        compiler_params=_params("parallel", "parallel", "arbitrary"),
        name="diff_attn",
    )(q, k, vt, lam_p.arr)


def _hgrn_gates(zf, lb_ref, kk_ref, b_ref, rs):
    row_in_chunk = lax.broadcasted_iota(jnp.int32, (HG_GATE_ROWS, 1), 0) % CHUNK
    for hd in range(HG_HEADS):
        ls = slice(hd * HG_DK, (hd + 1) * HG_DK)
        lb = lb_ref[:, ls]
        for r0 in range(0, zf.shape[0], HG_GATE_ROWS):
            out_rows = slice(rs.start + r0, rs.start + r0 + HG_GATE_ROWS)
            sig = jax.nn.sigmoid(zf[r0:r0 + HG_GATE_ROWS, ls])
            kk_ref[hd, out_rows, :] = (1.0 - lb) * (1.0 - sig)
            b = jnp.log(jnp.maximum(lb + (1.0 - lb) * sig, TINY))
            step = 1
            while step < CHUNK:
                b = b + jnp.where(row_in_chunk >= step, pltpu.roll(b, step, 0), 0.0)
                step *= 2
            b_ref[hd, out_rows, :] = b


def _hgrn_kernel(hq_ref, hi_ref, hg_ref, kk_ref, b_ref, go_ref, o_ref,
                 st_ref, bpad_ref, kpad_ref, vpad_ref):
    n_chunks = HG_SEQ_TILE // CHUNK
    mid = CHUNK // 2 - 1

    @pl.when(pl.program_id(1) == 0)
    def _():
        st_ref[...] = jnp.zeros_like(st_ref)

    b_mid = b_ref[:, pl.ds(mid, n_chunks, stride=CHUNK), :]
    b_end = b_ref[:, pl.ds(CHUNK - 1, n_chunks, stride=CHUNK), :]
    worst = jnp.max(jnp.maximum(-b_mid, b_mid - b_end))

    row = lax.broadcasted_iota(jnp.int32, (CHUNK, 1), 0)

    def load(ci, hd):
        rows = pl.ds(pl.multiple_of(ci * CHUNK, CHUNK), CHUNK)
        ls = slice(hd * HG_DK, (hd + 1) * HG_DK)
        return rows, ls, hq_ref[rows, ls], hi_ref[rows, ls], b_ref[hd, rows, :], kk_ref[hd, rows, :]

    def carry_state(hd, qh, vh, b, kk):
        st = st_ref[hd]
        o = _dot_nt((qh * jnp.exp(b)).astype(BF16), st.astype(BF16))
        b_last = b[CHUNK - 1:CHUNK]
        kdec = kk * jnp.exp(b_last - b)
        st_ref[hd] = st * jnp.exp(b_last) + _dot_tn(vh.astype(BF16), kdec.astype(BF16))
        return o

    def finish(rows, ls, o):
        on = _rms(o, go_ref[:, ls]) * jax.nn.silu(hg_ref[rows, ls])
        o_ref[rows, ls] = on.astype(BF16)

    @pl.when(worst < HG_SAFE_LOG)
    def _():
        causal = row >= lax.broadcasted_iota(jnp.int32, (1, CHUNK), 1)

        def chunk_body(cj, carry):
            for u in range(HG_UNROLL):
                ci = cj * HG_UNROLL + u
                for hd in range(HG_HEADS):
                    rows, ls, qh, vh, b, kk = load(ci, hd)
                    o = carry_state(hd, qh, vh, b, kk)
                    b_m = b[mid:mid + 1]
                    qf = (qh * jnp.exp(b - b_m)).astype(BF16)
                    kf = (kk * jnp.exp(b_m - b)).astype(BF16)
                    a = jnp.where(causal, _dot_nt(qf, kf), 0.0)
                    finish(rows, ls, o + _dot(a.astype(BF16), vh.astype(BF16)))
            return carry

        lax.fori_loop(0, n_chunks // HG_UNROLL, chunk_body, 0)

    @pl.when(jnp.logical_not(worst < HG_SAFE_LOG))
    def _():
        n_sub = CHUNK // HG_SUB
        off_w = HG_SUB * (n_sub * (n_sub - 1) // 2)
        zpad = jnp.zeros((HG_HEADS, HG_SUB, HG_DK), F32)
        bpad_ref[:, :HG_SUB, :] = zpad
        kpad_ref[:, :HG_SUB, :] = zpad
        vpad_ref[:, :HG_SUB, :] = zpad
        row_sub = row % HG_SUB
        r2 = lax.broadcasted_iota(jnp.int32, (2 * HG_DK, 2 * HG_DK), 0) // HG_DK
        c2 = lax.broadcasted_iota(jnp.int32, (2 * HG_DK, 2 * HG_DK), 1) // HG_DK
        ones2 = jnp.where(r2 == c2, 1.0, 0.0).astype(BF16)
        col = lax.broadcasted_iota(jnp.int32, (1, off_w), 1)
        col_blk = jnp.zeros((1, off_w), jnp.int32)
        for i in range(1, n_sub):
            col_blk = col_blk + jnp.where(col >= HG_SUB * (i * (i - 1) // 2), 1, 0)
        off_mask = col_blk == (row // HG_SUB)

        def chunk_body(ci, carry):
            for hd in range(HG_HEADS):
                rows, ls, qh, vh, b, kk = load(ci, hd)
                o = carry_state(hd, qh, vh, b, kk)

                refs = [b[i * HG_SUB - 1:i * HG_SUB] for i in range(1, n_sub)]
                bref = jnp.concatenate(
                    [jnp.zeros((HG_SUB, HG_DK), F32)]
                    + [jnp.broadcast_to(r, (HG_SUB, HG_DK)) for r in refs], axis=0)
                qs = qh * jnp.exp(b - bref)
                kst = jnp.concatenate(
                    [kk[:i * HG_SUB] * jnp.exp(refs[i - 1] - b[:i * HG_SUB]) for i in range(1, n_sub)],
                    axis=0)
                vst = jnp.concatenate([vh[:i * HG_SUB] for i in range(1, n_sub)], axis=0)
                a_off = jnp.where(off_mask, _dot_nt(qs.astype(BF16), kst.astype(BF16)), 0.0)
                o = o + _dot(a_off.astype(BF16), vst.astype(BF16))

                bpad_ref[hd, HG_SUB:, :] = b
                kpad_ref[hd, HG_SUB:, :] = kk
                vpad_ref[hd, HG_SUB:, :] = vh
                for dp in range(HG_SUB // 2):
                    terms = []
                    for d in (2 * dp, 2 * dp + 1):
                        lo = HG_SUB - d
                        w = qh * jnp.exp(b - bpad_ref[hd, lo:lo + CHUNK, :]) * kpad_ref[hd, lo:lo + CHUNK, :]
                        terms.append(jnp.where(row_sub >= d, w, 0.0))
                    dsum = _dot(jnp.concatenate(terms, axis=1).astype(BF16), ones2)
                    for j, d in enumerate((2 * dp, 2 * dp + 1)):
                        lo = HG_SUB - d
                        o = o + dsum[:, j * HG_DK:(j + 1) * HG_DK] * vpad_ref[hd, lo:lo + CHUNK, :]
                finish(rows, ls, o)
            return carry

        lax.fori_loop(0, n_chunks, chunk_body, 0)


def _hgrn_call(zh, kk, bdec, g_out, b, s):
    width = HG_HEADS * HG_DK
    n_seq = s // HG_SEQ_TILE
    part = lambda j: pl.BlockSpec((HG_SEQ_TILE, width), lambda bi, si: (bi * n_seq + si, j))
    gate = pl.BlockSpec((HG_HEADS, HG_SEQ_TILE, HG_DK), lambda bi, si: (0, bi * n_seq + si, 0))
    pad = pltpu.VMEM((HG_HEADS, HG_SUB + CHUNK, HG_DK), F32)
    return pl.pallas_call(
        _hgrn_kernel,
        grid=(b, n_seq),
        in_specs=[part(0), part(1), part(2), gate, gate, g_out.spec],
        out_specs=pl.BlockSpec((HG_SEQ_TILE, width), lambda bi, si: (bi * n_seq + si, 0)),
        out_shape=jax.ShapeDtypeStruct((b * s, width), BF16),
        scratch_shapes=[pltpu.VMEM((HG_HEADS, HG_DV, HG_DK), F32), pad, pad, pad],
        compiler_params=_params("parallel", "arbitrary"),
        name="hgrn2",
    )(zh, zh, zh, kk, bdec, g_out.arr)


def _rope_tables(positions, dim, group):
    half = dim // 2
    inv_freq = ROPE_THETA ** (-jnp.arange(0, dim, 2, dtype=F32) / dim)
    ang = positions.astype(F32).reshape(-1, 1) * inv_freq
    cos, sin = jnp.cos(ang), jnp.sin(ang)
    t = ang.shape[0]
    rest = group - dim
    c = jnp.concatenate([cos, cos, jnp.ones((t, rest), F32)], axis=1)
    sa = jnp.concatenate([-sin, jnp.zeros((t, half + rest), F32)], axis=1)
    sb = jnp.concatenate([jnp.zeros((t, half), F32), sin, jnp.zeros((t, rest), F32)], axis=1)
    rep = LANES // group
    return tuple(jnp.tile(a, (1, rep)) for a in (c, sa, sb))


def kernel(x, positions, norm_g, ffn_w_gate, ffn_w_up, ffn_w_down, ev_w_in, ev_g_q, ev_w_uq, ev_g_kv, ev_w_ukv, ev_lb_logits, ev_g_out, ev_w_out, od_w_in, od_lambda, od_g_head, od_w_out):
    b, s, d = x.shape
    t = b * s
    xt = x.reshape(t, d)
    tabs_m = _rope_tables(positions, MLA_ROPE, LANES)
    tabs_d = _rope_tables(positions, DF_ROT, DF_DH)
    lb_w = jax.nn.softmax(ev_lb_logits.astype(F32), axis=0)
    lb_all = jnp.cumsum(lb_w, axis=0) - lb_w[0:1]

    n_even = ev_w_in.shape[0]
    o_pe = MLA_Q_RANK + MLA_KV_RANK + MLA_ROPE
    w_in_e = jnp.concatenate(
        [ev_w_in[..., :o_pe].astype(BF16), jnp.zeros((n_even, d, LANES - MLA_ROPE), BF16),
         ev_w_in[..., o_pe:].astype(BF16)], axis=-1)
    w_uq = ev_w_uq.astype(BF16).reshape(n_even, MLA_Q_RANK, MLA_HEADS, MLA_NOPE + MLA_ROPE)
    w_uq = jnp.pad(w_uq, ((0, 0), (0, 0), (0, 0), (0, MLA_QK_PAD - MLA_NOPE - MLA_ROPE)))
    w_uq = w_uq.reshape(n_even, MLA_Q_RANK, MLA_HEADS * MLA_QK_PAD)
    w_ukv = ev_w_ukv.astype(BF16).reshape(n_even, MLA_KV_RANK, MLA_HEADS, 2, MLA_NOPE)
    w_ukv = w_ukv.transpose(0, 1, 3, 2, 4).reshape(n_even, MLA_KV_RANK, 2 * MLA_HEADS * MLA_NOPE)
    w_out_e = ev_w_out.astype(BF16)
    w_in_o = od_w_in.astype(BF16)
    w_out_o = od_w_out.astype(BF16)
    g_q = ev_g_q.reshape(n_even, 1, -1)
    g_kv = ev_g_kv.reshape(n_even, 1, -1)
    g_out = ev_g_out.reshape(n_even, 1, -1)
    lb_all = lb_all.reshape(n_even, 1, -1)
    g_head_o = od_g_head.reshape(od_g_head.shape[0], 1, -1)

    for l in range(DEPTH):
        xt = _ffn_call(xt, _Pick(norm_g, l, 0), _Slabs(ffn_w_gate, l, 0), _Slabs(ffn_w_up, l, 0),
                       _Slabs(ffn_w_down, l, 0))
        j = l // 2
        g_mix = _Pick(norm_g, l, 1)
        if l % 2 == 0:
            q, k, v, zh, kk, bdec = _even_pre_call(
                xt, g_mix, _Pick(w_in_e, j), _Pick(g_q, j), _Pick(w_uq, j), _Pick(g_kv, j),
                _Pick(w_ukv, j), _Pick(lb_all, j), tabs_m)
            o_a = _mla_attn_call(q.reshape(b, s, -1), k.reshape(b, s, -1), v)
            o_b = _hgrn_call(zh, kk, bdec, _Pick(g_out, j), b, s)
            parts = [o_a, o_b]
            head_norm = {}
            w_out = _Pick(w_out_e, j)
        else:
            lambda_init = 0.8 - 0.6 * math.exp(-0.3 * l)
            q, k, v = _odd_pre_call(xt, g_mix, _Pick(w_in_o, j), tabs_d)
            o = _diff_attn_call(q.reshape(b, s, -1), k.reshape(b, s, -1), v, _Pick(od_lambda, j),
                                lambda_init)
            parts = [o]
            head_norm = dict(g_head=_Pick(g_head_o, j), head_scale=1.0 - lambda_init)
            w_out = _Pick(w_out_o, j)
        xt = _mix_ffn_call(xt, parts, w_out, g_mix, _Pick(norm_g, l, 2),
                           _Slabs(ffn_w_gate, l, 1), _Slabs(ffn_w_up, l, 1), _Slabs(ffn_w_down, l, 1),
                           **head_norm)
    return xt.reshape(b, s, d)
```

```python
import functools
import math

import jax
import jax.numpy as jnp
import numpy as np
from jax import lax
from jax.experimental import pallas as pl
from jax.experimental.pallas import tpu as pltpu

D_MODEL = 1024
DEPTH = 4
CHUNK = 64
ROPE_THETA = 500000.0
EPS = 1e-6
NEG_INF = -1e30
LOG2_E = math.log2(math.e)
TINY = 1e-30
D_FF = 2816
MLA_HEADS = 4
MLA_NOPE = 128
MLA_ROPE = 64
MLA_V = 128
MLA_Q_RANK = 384
MLA_KV_RANK = 256
HG_HEADS = 4
HG_DK = 128
HG_DV = 128
DF_HEADS = 8
DF_DH = 64
DF_ROT = DF_DH // 4

LANES = 128
VMEM_LIMIT = 56 * 1024 * 1024

TOKEN_TILE = 1024
MIX_TILE = 512
MLA_Q_TILE = 512
DF_Q_TILE = 256
KEY_BLOCK = 1024
ATTN_HEADS_PER_STEP = 4
HG_SEQ_TILE = 1024
HG_SUB = 16
HG_GATE_ROWS = 128
HG_UNROLL = 4
HG_SAFE_LOG = 75.0
W_CAST_STEPS = 8
MIX_ROW_GROUPS = 2
FFN_ROW_GROUPS = 4
PRE_ROW_GROUPS = 2
FF_CHUNKS = ((0, 1024), (1024, 2048), (2048, D_FF))

MLA_QK_PAD = 2 * LANES
V_ROWS = LANES + 16

F32 = jnp.float32
BF16 = jnp.bfloat16


def _rms(x, g):
    ms = jnp.mean(x * x, axis=-1, keepdims=True)
    return x * lax.rsqrt(ms + EPS) * g


def _dot(a, b):
    return jnp.dot(a, b, preferred_element_type=F32)


def _dot_nt(a, b):
    return lax.dot_general(a, b, (((1,), (1,)), ((), ())), preferred_element_type=F32)


def _dot_tn(a, b):
    return lax.dot_general(a, b, (((0,), (0,)), ((), ())), preferred_element_type=F32)


def _rope_slab(x, c, sa, sb, half):
    return x * c + pltpu.roll(x, LANES - half, 1) * sa + pltpu.roll(x, half, 1) * sb


def _store_values(vt_ref, rs, v, heads):
    ones = jnp.ones((V_ROWS - LANES, v.shape[0]), BF16)
    for hd in range(heads):
        vt_ref[hd, :LANES, rs] = v[:, hd * LANES:(hd + 1) * LANES].T.astype(BF16)
        vt_ref[hd, LANES:, rs] = ones


def _vt_out(heads, t):
    per_blk = KEY_BLOCK // TOKEN_TILE
    spec = pl.BlockSpec((heads, None, V_ROWS, TOKEN_TILE), lambda i: (0, i // per_blk, 0, i % per_blk))
    return spec, jax.ShapeDtypeStruct((heads, t // KEY_BLOCK, V_ROWS, KEY_BLOCK), BF16)


class _Slabs:
    def __init__(self, arr, layer, half):
        self.arr = arr
        rows, cols = arr.shape[2:]
        self.shape = (rows, cols)
        self.spec = pl.BlockSpec((None, None, rows // W_CAST_STEPS, cols),
                                 lambda i: (layer, half, jnp.minimum(i, W_CAST_STEPS - 1), 0))


def _cast_slabs(i, pairs):
    for j in range(W_CAST_STEPS):
        @pl.when(i == j)
        def _():
            for src, dst in pairs:
                rows = src.shape[0]
                dst[j * rows:(j + 1) * rows, :] = src[...].astype(BF16)


def _token_tile(i):
    return jnp.maximum(i - W_CAST_STEPS, 0)


class _Pick:
    def __init__(self, arr, *idx):
        self.arr = arr
        nd = arr.ndim - len(idx)
        self.spec = pl.BlockSpec((None,) * len(idx) + arr.shape[len(idx):],
                                 lambda *_: idx + (0,) * nd, pipeline_mode=pl.Buffered(1))


def _params(*sem):
    return pltpu.CompilerParams(dimension_semantics=sem, vmem_limit_bytes=VMEM_LIMIT)


def _ffn_apply(xs, gpre, gpost, wg_ref, wu_ref, wd_ref):
    xn = [_rms(x, gpre).astype(BF16) for x in xs]
    acc = [None] * len(xs)
    for lo, hi in FF_CHUNKS:
        for i in range(len(xs)):
            g = _dot(xn[i], wg_ref[:, lo:hi])
            u = _dot(xn[i], wu_ref[:, lo:hi])
            a = (jax.nn.silu(g) * u).astype(BF16)
            h = _dot(a, wd_ref[lo:hi, :])
            acc[i] = h if acc[i] is None else acc[i] + h
    return [x + 0.5 * _rms(a, gpost) for x, a in zip(xs, acc)]


def _row_groups(rows, n):
    step = rows // n
    return [slice(i * step, (i + 1) * step) for i in range(n)]


def _ffn_kernel(x_ref, g_ref, wg32_ref, wu32_ref, wd32_ref, o_ref, wg_ref, wu_ref, wd_ref):
    i = pl.program_id(0)
    _cast_slabs(i, [(wg32_ref, wg_ref), (wu32_ref, wu_ref), (wd32_ref, wd_ref)])

    @pl.when(i >= W_CAST_STEPS)
    def _():
        g = g_ref[...]
        groups = _row_groups(TOKEN_TILE, FFN_ROW_GROUPS)
        outs = _ffn_apply([x_ref[rs, :] for rs in groups], g[0:1], g[1:2], wg_ref, wu_ref, wd_ref)
        for rs, o in zip(groups, outs):
            o_ref[rs, :] = o


def _ffn_call(x, g2, wg, wu, wd):
    t = x.shape[0]
    tile = pl.BlockSpec((TOKEN_TILE, D_MODEL), lambda i: (_token_tile(i), 0))
    picks = [g2, wg, wu, wd]
    return pl.pallas_call(
        _ffn_kernel,
        grid=(W_CAST_STEPS + t // TOKEN_TILE,),
        in_specs=[tile] + [p.spec for p in picks],
        out_specs=tile,
        out_shape=jax.ShapeDtypeStruct(x.shape, F32),
        scratch_shapes=[pltpu.VMEM(w.shape, BF16) for w in (wg, wu, wd)],
        compiler_params=_params("arbitrary"),
        name="ffn",
    )(x, *[p.arr for p in picks])


def _mix_ffn_kernel(n_parts, head_scale, *refs):
    x_ref = refs[0]
    part_refs = refs[1:1 + n_parts]
    rest = refs[1 + n_parts:]
    if head_scale is not None:
        gh_ref, rest = rest[0], rest[1:]
    wo_ref, gm_ref, gf_ref, wg32_ref, wu32_ref, wd32_ref, o_ref, wg_ref, wu_ref, wd_ref = rest
    i = pl.program_id(0)
    _cast_slabs(i, [(wg32_ref, wg_ref), (wu32_ref, wu_ref), (wd32_ref, wd_ref)])
    pl.when(i >= W_CAST_STEPS)(functools.partial(
        _mix_ffn_tile, head_scale, x_ref, part_refs, gh_ref if head_scale is not None else None,
        wo_ref, gm_ref, gf_ref, wg_ref, wu_ref, wd_ref, o_ref))


def _mix_ffn_tile(head_scale, x_ref, part_refs, gh_ref, wo_ref, gm_ref, gf_ref, wg_ref, wu_ref, wd_ref, o_ref):
    gf = gf_ref[...]
    groups = _row_groups(MIX_TILE, MIX_ROW_GROUPS)
    xs = []
    for rs in groups:
        cols = []
        for p_ref in part_refs:
            if len(p_ref.shape) == 2:
                cols.append(p_ref[rs, :])
                continue
            for hd in range(p_ref.shape[0]):
                o_t = p_ref[hd, :, rs].astype(F32)
                if head_scale is not None:
                    o_t = o_t * lax.rsqrt(jnp.mean(o_t * o_t, axis=0, keepdims=True) + EPS)
                o = o_t.T
                if head_scale is not None:
                    o = o * (gh_ref[:, hd * LANES:(hd + 1) * LANES] * head_scale)
                cols.append(o.astype(BF16))
        m = _dot(jnp.concatenate(cols, axis=1), wo_ref[...])
        xs.append(x_ref[rs, :] + _rms(m, gm_ref[1:2]))
    outs = _ffn_apply(xs, gf[0:1], gf[1:2], wg_ref, wu_ref, wd_ref)
    for rs, o in zip(groups, outs):
        o_ref[rs, :] = o


def _mix_ffn_call(x, parts, wo, g_mix, g_ffn, wg, wu, wd, g_head=None, head_scale=None):
    t = x.shape[0]
    tile = pl.BlockSpec((MIX_TILE, D_MODEL), lambda i: (_token_tile(i), 0))
    part_specs = [pl.BlockSpec((MIX_TILE, p.shape[-1]), lambda i: (_token_tile(i), 0)) if p.ndim == 2
                  else pl.BlockSpec(p.shape[:2] + (MIX_TILE,), lambda i: (0, 0, _token_tile(i)))
                  for p in parts]
    picks = ([] if g_head is None else [g_head]) + [wo, g_mix, g_ffn, wg, wu, wd]
    return pl.pallas_call(
        functools.partial(_mix_ffn_kernel, len(parts), head_scale),
        grid=(W_CAST_STEPS + t // MIX_TILE,),
        in_specs=[tile] + part_specs + [p.spec for p in picks],
        out_specs=tile,
        out_shape=jax.ShapeDtypeStruct(x.shape, F32),
        scratch_shapes=[pltpu.VMEM(w.shape, BF16) for w in (wg, wu, wd)],
        compiler_params=_params("arbitrary"),
        name="mix_ffn",
    )(x, *parts, *[p.arr for p in picks])


def _even_pre_kernel(x_ref, g_ref, win_ref, gq_ref, wuq_ref, gkv_ref, wukv_ref, lb_ref,
                     c_ref, sa_ref, sb_ref, q_ref, k_ref, v_ref, zh_ref, kk_ref, b_ref):
    o_kv = MLA_Q_RANK
    o_pe = o_kv + MLA_KV_RANK
    o_h = o_pe + LANES
    half = MLA_ROPE // 2
    scale = LOG2_E * (MLA_NOPE + MLA_ROPE) ** -0.5
    for rs in _row_groups(TOKEN_TILE, PRE_ROW_GROUPS):
        h = _rms(x_ref[rs, :], g_ref[0:1]).astype(BF16)
        z = _dot(h, win_ref[...])
        hw = HG_HEADS * HG_DK
        zh_ref[rs, :hw] = z[:, o_h:o_h + hw]
        zh_ref[rs, hw:] = z[:, o_h + 2 * hw:]
        _hgrn_gates(z[:, o_h + hw:o_h + 2 * hw], lb_ref, kk_ref, b_ref, rs)
        c, sa, sb = c_ref[rs, :], sa_ref[rs, :], sb_ref[rs, :]

        cq = _rms(z[:, :o_kv], gq_ref[...]).astype(BF16)
        q = _dot(cq, wuq_ref[...]) * scale
        for hd in range(MLA_HEADS):
            base = hd * MLA_QK_PAD
            q_ref[rs, base:base + LANES] = q[:, base:base + LANES].astype(BF16)
            q_ref[rs, base + LANES:base + 2 * LANES] = _rope_slab(
                q[:, base + LANES:base + 2 * LANES], c, sa, sb, half).astype(BF16)

        ckv = _rms(z[:, o_kv:o_pe], gkv_ref[...]).astype(BF16)
        kv = _dot(ckv, wukv_ref[...])
        kpe = _rope_slab(z[:, o_pe:o_h], c, sa, sb, half).astype(BF16)
        for hd in range(MLA_HEADS):
            base = hd * MLA_QK_PAD
            k_ref[rs, base:base + LANES] = kv[:, hd * LANES:(hd + 1) * LANES].astype(BF16)
            k_ref[rs, base + LANES:base + 2 * LANES] = kpe
        _store_values(v_ref, rs, kv[:, MLA_HEADS * MLA_NOPE:], MLA_HEADS)


def _even_pre_call(x, g, win, gq, wuq, gkv, wukv, lb, tabs):
    t = x.shape[0]
    picks = [g, win, gq, wuq, gkv, wukv, lb]
    row = lambda w: pl.BlockSpec((TOKEN_TILE, w), lambda i: (i, 0))
    qk_w = MLA_HEADS * MLA_QK_PAD
    vt_spec, vt_shape = _vt_out(MLA_HEADS, t)
    zh_w = 3 * HG_HEADS * HG_DK
    gate_spec = pl.BlockSpec((HG_HEADS, TOKEN_TILE, HG_DK), lambda i: (0, i, 0))
    gate_shape = jax.ShapeDtypeStruct((HG_HEADS, t, HG_DK), F32)
    return pl.pallas_call(
        _even_pre_kernel,
        grid=(t // TOKEN_TILE,),
        in_specs=[row(D_MODEL)] + [p.spec for p in picks] + [row(LANES), row(LANES), row(LANES)],
        out_specs=[row(qk_w), row(qk_w), vt_spec, row(zh_w), gate_spec, gate_spec],
        out_shape=[jax.ShapeDtypeStruct((t, qk_w), BF16), jax.ShapeDtypeStruct((t, qk_w), BF16),
                   vt_shape, jax.ShapeDtypeStruct((t, zh_w), F32), gate_shape, gate_shape],
        compiler_params=_params("parallel"),
        name="even_pre",
    )(x, *[p.arr for p in picks], *tabs)


def _odd_pre_kernel(x_ref, g_ref, win_ref, c_ref, sa_ref, sb_ref, q_ref, k_ref, v_ref):
    half = DF_ROT // 2
    width = DF_HEADS * 2 * DF_DH
    scale = LOG2_E * DF_DH ** -0.5
    for rs in _row_groups(TOKEN_TILE, PRE_ROW_GROUPS):
        h = _rms(x_ref[rs, :], g_ref[0:1]).astype(BF16)
        c, sa, sb = c_ref[rs, :], sa_ref[rs, :], sb_ref[rs, :]
        q = _dot(h, win_ref[:, :width]) * scale
        k = _dot(h, win_ref[:, width:2 * width])
        for j in range(width // LANES):
            sl = slice(j * LANES, (j + 1) * LANES)
            q_ref[rs, sl] = _rope_slab(q[:, sl], c, sa, sb, half).astype(BF16)
            k_ref[rs, sl] = _rope_slab(k[:, sl], c, sa, sb, half).astype(BF16)
        _store_values(v_ref, rs, _dot(h, win_ref[:, 2 * width:]), DF_HEADS)


def _odd_pre_call(x, g, win, tabs):
    t = x.shape[0]
    width = DF_HEADS * 2 * DF_DH
    row = lambda w: pl.BlockSpec((TOKEN_TILE, w), lambda i: (i, 0))
    vt_spec, vt_shape = _vt_out(DF_HEADS, t)
    return pl.pallas_call(
        _odd_pre_kernel,
        grid=(t // TOKEN_TILE,),
        in_specs=[row(D_MODEL), g.spec, win.spec, row(LANES), row(LANES), row(LANES)],
        out_specs=[row(width), row(width), vt_spec],
        out_shape=[jax.ShapeDtypeStruct((t, width), BF16), jax.ShapeDtypeStruct((t, width), BF16), vt_shape],
        compiler_params=_params("parallel"),
        name="odd_pre",
    )(x, g.arr, win.arr, *tabs)


def _flash(streams, q_chunk, k_ref, vt_ref, m_ref, acc_ref, row0, tq):
    m_rows = streams[0][0].shape[0]
    for i in range(len(streams)):
        m_ref[i] = jnp.full((1, m_rows), NEG_INF, F32)
        acc_ref[i] = jnp.zeros((V_ROWS, m_rows), F32)

    def step(blk, width, masked):
        start = pl.multiple_of(blk * KEY_BLOCK, KEY_BLOCK)
        scores = [_dot_nt(k_ref[0, pl.ds(start, width), ksl], q) for q, ksl, _ in streams]
        for i, (s, (_, _, hd)) in enumerate(zip(scores, streams)):
            if masked:
                k_chunk = (start + (width - tq) + lax.broadcasted_iota(jnp.int32, (tq, 1), 0)) // CHUNK
                tail = jnp.where(k_chunk <= q_chunk, s[width - tq:], NEG_INF)
                s = tail if width == tq else jnp.concatenate([s[:width - tq], tail], axis=0)
            m_i = m_ref[i]
            m_new = jnp.maximum(m_i, jnp.max(s, axis=0, keepdims=True))
            m_ref[i] = m_new
            p = jnp.exp2((s - m_new).astype(BF16))
            acc_ref[i] = jnp.exp2(m_i - m_new) * acc_ref[i] + _dot(vt_ref[hd, blk, :, :width], p)

    n_full = row0 // KEY_BLOCK

    def unmasked(blk, carry):
        step(blk, KEY_BLOCK, False)
        return carry

    lax.fori_loop(0, n_full, unmasked, 0)
    for p in range(KEY_BLOCK // tq):
        pl.when((row0 % KEY_BLOCK) // tq == p)(functools.partial(step, n_full, tq * (p + 1), True))
    outs = []
    for i in range(len(streams)):
        acc = acc_ref[i]
        outs.append(acc[:LANES] * (1.0 / acc[LANES:LANES + 1]))
    return outs


def _row_chunks(row0, n_rows):
    rows = row0 + lax.broadcasted_iota(jnp.int32, (1, n_rows), 1)
    return rows // CHUNK


def _flash_scratch(hp, m_rows):
    return [pltpu.VMEM((hp, 1, m_rows), F32), pltpu.VMEM((hp, V_ROWS, m_rows), F32)]


def _ot_spec(hp, tq, s):
    return pl.BlockSpec((hp, LANES, tq), lambda bi, h, i: (h, 0, bi * (s // tq) + i))


def _vt_spec(hp, s):
    return pl.BlockSpec((hp, s // KEY_BLOCK, V_ROWS, KEY_BLOCK), lambda bi, h, i: (h, bi, 0, 0))


def _mla_attn_kernel(q_ref, k_ref, vt_ref, o_ref, m_ref, acc_ref):
    row0 = pl.program_id(2) * MLA_Q_TILE
    streams = [(q_ref[0, :, h * MLA_QK_PAD:(h + 1) * MLA_QK_PAD],
                slice(h * MLA_QK_PAD, (h + 1) * MLA_QK_PAD), h)
               for h in range(ATTN_HEADS_PER_STEP)]
    outs = _flash(streams, _row_chunks(row0, MLA_Q_TILE), k_ref, vt_ref, m_ref, acc_ref, row0, MLA_Q_TILE)
    for h, o_t in enumerate(outs):
        o_ref[h] = o_t.astype(BF16)


def _mla_attn_call(q, k, vt):
    b, s, _ = q.shape
    hp = ATTN_HEADS_PER_STEP
    return pl.pallas_call(
        _mla_attn_kernel,
        grid=(b, MLA_HEADS // hp, s // MLA_Q_TILE),
        in_specs=[pl.BlockSpec((1, MLA_Q_TILE, hp * MLA_QK_PAD), lambda bi, h, i: (bi, i, h)),
                  pl.BlockSpec((1, s, hp * MLA_QK_PAD), lambda bi, h, i: (bi, 0, h)),
                  _vt_spec(hp, s)],
        out_specs=_ot_spec(hp, MLA_Q_TILE, s),
        out_shape=jax.ShapeDtypeStruct((MLA_HEADS, MLA_V, b * s), BF16),
        scratch_shapes=_flash_scratch(hp, MLA_Q_TILE),
        compiler_params=_params("parallel", "parallel", "arbitrary"),
        name="mla_attn",
    )(q, k, vt)


def _diff_attn_kernel(lambda_init, q_ref, k_ref, vt_ref, lam_ref, o_ref, m_ref, acc_ref):
    row0 = pl.program_id(2) * DF_Q_TILE
    dv = 2 * DF_DH
    lane = lax.broadcasted_iota(jnp.int32, (1, dv), 1)
    streams = []
    for h in range(ATTN_HEADS_PER_STEP):
        hs = slice(h * dv, (h + 1) * dv)
        q = q_ref[0, :, hs]
        zero = jnp.zeros_like(q)
        q2 = jnp.concatenate([jnp.where(lane < DF_DH, q, zero), jnp.where(lane >= DF_DH, q, zero)], axis=0)
        streams.append((q2, hs, h))
    chunks = _row_chunks(row0, DF_Q_TILE)
    outs = _flash(streams, jnp.concatenate([chunks, chunks], axis=1), k_ref, vt_ref, m_ref, acc_ref, row0,
                  DF_Q_TILE)
    lp = lam_ref[...]
    lam = (jnp.exp(jnp.sum(lp[0:1] * lp[1:2], axis=-1, keepdims=True))
           - jnp.exp(jnp.sum(lp[2:3] * lp[3:4], axis=-1, keepdims=True)) + lambda_init)
    for h, o2_t in enumerate(outs):
        o_ref[h] = (o2_t[:, :DF_Q_TILE] - lam * o2_t[:, DF_Q_TILE:]).astype(BF16)


def _diff_attn_call(q, k, vt, lam_p, lambda_init):
    b, s, _ = q.shape
    hp = ATTN_HEADS_PER_STEP
    w = hp * 2 * DF_DH
    return pl.pallas_call(
        functools.partial(_diff_attn_kernel, lambda_init),
        grid=(b, DF_HEADS // hp, s // DF_Q_TILE),
        in_specs=[pl.BlockSpec((1, DF_Q_TILE, w), lambda bi, h, i: (bi, i, h)),
                  pl.BlockSpec((1, s, w), lambda bi, h, i: (bi, 0, h)),
                  _vt_spec(hp, s),
                  lam_p.spec],
        out_specs=_ot_spec(hp, DF_Q_TILE, s),
        out_shape=jax.ShapeDtypeStruct((DF_HEADS, 2 * DF_DH, b * s), BF16),
        scratch_shapes=_flash_scratch(hp, 2 * DF_Q_TILE),
        compiler_params=_params("parallel", "parallel", "arbitrary"),
        name="diff_attn",
    )(q, k, vt, lam_p.arr)


def _hgrn_gates(zf, lb_ref, kk_ref, b_ref, rs):
    row_in_chunk = lax.broadcasted_iota(jnp.int32, (HG_GATE_ROWS, 1), 0) % CHUNK
    for hd in range(HG_HEADS):
        ls = slice(hd * HG_DK, (hd + 1) * HG_DK)
        lb = lb_ref[:, ls]
        for r0 in range(0, zf.shape[0], HG_GATE_ROWS):
            out_rows = slice(rs.start + r0, rs.start + r0 + HG_GATE_ROWS)
            sig = jax.nn.sigmoid(zf[r0:r0 + HG_GATE_ROWS, ls])
            kk_ref[hd, out_rows, :] = (1.0 - lb) * (1.0 - sig)
            b = jnp.log(jnp.maximum(lb + (1.0 - lb) * sig, TINY))
            step = 1
            while step < CHUNK:
                b = b + jnp.where(row_in_chunk >= step, pltpu.roll(b, step, 0), 0.0)
                step *= 2
            b_ref[hd, out_rows, :] = b


def _hgrn_kernel(hq_ref, hi_ref, hg_ref, kk_ref, b_ref, go_ref, o_ref,
                 st_ref, bpad_ref, kpad_ref, vpad_ref):
    n_chunks = HG_SEQ_TILE // CHUNK
    mid = CHUNK // 2 - 1

    @pl.when(pl.program_id(1) == 0)
    def _():
        st_ref[...] = jnp.zeros_like(st_ref)

    b_mid = b_ref[:, pl.ds(mid, n_chunks, stride=CHUNK), :]
    b_end = b_ref[:, pl.ds(CHUNK - 1, n_chunks, stride=CHUNK), :]
    worst = jnp.max(jnp.maximum(-b_mid, b_mid - b_end))

    row = lax.broadcasted_iota(jnp.int32, (CHUNK, 1), 0)

    def load(ci, hd):
        rows = pl.ds(pl.multiple_of(ci * CHUNK, CHUNK), CHUNK)
        ls = slice(hd * HG_DK, (hd + 1) * HG_DK)
        return rows, ls, hq_ref[rows, ls], hi_ref[rows, ls], b_ref[hd, rows, :], kk_ref[hd, rows, :]

    def carry_state(hd, qh, vh, b, kk):
        st = st_ref[hd]
        o = _dot_nt((qh * jnp.exp(b)).astype(BF16), st.astype(BF16))
        b_last = b[CHUNK - 1:CHUNK]
        kdec = kk * jnp.exp(b_last - b)
        st_ref[hd] = st * jnp.exp(b_last) + _dot_tn(vh.astype(BF16), kdec.astype(BF16))
        return o

    def finish(rows, ls, o):
        on = _rms(o, go_ref[:, ls]) * jax.nn.silu(hg_ref[rows, ls])
        o_ref[rows, ls] = on.astype(BF16)

    @pl.when(worst < HG_SAFE_LOG)
    def _():
        causal = row >= lax.broadcasted_iota(jnp.int32, (1, CHUNK), 1)

        def chunk_body(cj, carry):
            for u in range(HG_UNROLL):
                ci = cj * HG_UNROLL + u
                for hd in range(HG_HEADS):
                    rows, ls, qh, vh, b, kk = load(ci, hd)
                    o = carry_state(hd, qh, vh, b, kk)
                    b_m = b[mid:mid + 1]
                    qf = (qh * jnp.exp(b - b_m)).astype(BF16)
                    kf = (kk * jnp.exp(b_m - b)).astype(BF16)
                    a = jnp.where(causal, _dot_nt(qf, kf), 0.0)
                    finish(rows, ls, o + _dot(a.astype(BF16), vh.astype(BF16)))
            return carry

        lax.fori_loop(0, n_chunks // HG_UNROLL, chunk_body, 0)

    @pl.when(jnp.logical_not(worst < HG_SAFE_LOG))
    def _():
        n_sub = CHUNK // HG_SUB
        off_w = HG_SUB * (n_sub * (n_sub - 1) // 2)
        zpad = jnp.zeros((HG_HEADS, HG_SUB, HG_DK), F32)
        bpad_ref[:, :HG_SUB, :] = zpad
        kpad_ref[:, :HG_SUB, :] = zpad
        vpad_ref[:, :HG_SUB, :] = zpad
        row_sub = row % HG_SUB
        r2 = lax.broadcasted_iota(jnp.int32, (2 * HG_DK, 2 * HG_DK), 0) // HG_DK
        c2 = lax.broadcasted_iota(jnp.int32, (2 * HG_DK, 2 * HG_DK), 1) // HG_DK
        ones2 = jnp.where(r2 == c2, 1.0, 0.0).astype(BF16)
        col = lax.broadcasted_iota(jnp.int32, (1, off_w), 1)
        col_blk = jnp.zeros((1, off_w), jnp.int32)
        for i in range(1, n_sub):
            col_blk = col_blk + jnp.where(col >= HG_SUB * (i * (i - 1) // 2), 1, 0)
        off_mask = col_blk == (row // HG_SUB)

        def chunk_body(ci, carry):
            for hd in range(HG_HEADS):
                rows, ls, qh, vh, b, kk = load(ci, hd)
                o = carry_state(hd, qh, vh, b, kk)

                refs = [b[i * HG_SUB - 1:i * HG_SUB] for i in range(1, n_sub)]
                bref = jnp.concatenate(
                    [jnp.zeros((HG_SUB, HG_DK), F32)]
                    + [jnp.broadcast_to(r, (HG_SUB, HG_DK)) for r in refs], axis=0)
                qs = qh * jnp.exp(b - bref)
                kst = jnp.concatenate(
                    [kk[:i * HG_SUB] * jnp.exp(refs[i - 1] - b[:i * HG_SUB]) for i in range(1, n_sub)],
                    axis=0)
                vst = jnp.concatenate([vh[:i * HG_SUB] for i in range(1, n_sub)], axis=0)
                a_off = jnp.where(off_mask, _dot_nt(qs.astype(BF16), kst.astype(BF16)), 0.0)
                o = o + _dot(a_off.astype(BF16), vst.astype(BF16))

                bpad_ref[hd, HG_SUB:, :] = b
                kpad_ref[hd, HG_SUB:, :] = kk
                vpad_ref[hd, HG_SUB:, :] = vh
                for dp in range(HG_SUB // 2):
                    terms = []
                    for d in (2 * dp, 2 * dp + 1):
                        lo = HG_SUB - d
                        w = qh * jnp.exp(b - bpad_ref[hd, lo:lo + CHUNK, :]) * kpad_ref[hd, lo:lo + CHUNK, :]
                        terms.append(jnp.where(row_sub >= d, w, 0.0))
                    dsum = _dot(jnp.concatenate(terms, axis=1).astype(BF16), ones2)
                    for j, d in enumerate((2 * dp, 2 * dp + 1)):
                        lo = HG_SUB - d
                        o = o + dsum[:, j * HG_DK:(j + 1) * HG_DK] * vpad_ref[hd, lo:lo + CHUNK, :]
                finish(rows, ls, o)
            return carry

        lax.fori_loop(0, n_chunks, chunk_body, 0)


def _hgrn_call(zh, kk, bdec, g_out, b, s):
    width = HG_HEADS * HG_DK
    n_seq = s // HG_SEQ_TILE
    part = lambda j: pl.BlockSpec((HG_SEQ_TILE, width), lambda bi, si: (bi * n_seq + si, j))
    gate = pl.BlockSpec((HG_HEADS, HG_SEQ_TILE, HG_DK), lambda bi, si: (0, bi * n_seq + si, 0))
    pad = pltpu.VMEM((HG_HEADS, HG_SUB + CHUNK, HG_DK), F32)
    return pl.pallas_call(
        _hgrn_kernel,
        grid=(b, n_seq),
        in_specs=[part(0), part(1), part(2), gate, gate, g_out.spec],
        out_specs=pl.BlockSpec((HG_SEQ_TILE, width), lambda bi, si: (bi * n_seq + si, 0)),
        out_shape=jax.ShapeDtypeStruct((b * s, width), BF16),
        scratch_shapes=[pltpu.VMEM((HG_HEADS, HG_DV, HG_DK), F32), pad, pad, pad],
        compiler_params=_params("parallel", "arbitrary"),
        name="hgrn2",
    )(zh, zh, zh, kk, bdec, g_out.arr)


def _rope_tables(positions, dim, group):
    half = dim // 2
    inv_freq = ROPE_THETA ** (-jnp.arange(0, dim, 2, dtype=F32) / dim)
    lane = np.arange(LANES) % group
    freq = jnp.where(lane < dim, inv_freq[lane % half], 0.0)
    ang = positions.astype(F32).reshape(-1, 1) * freq
    sin = jnp.sin(ang)
    return jnp.cos(ang), jnp.where(lane < half, -sin, 0.0), jnp.where(lane >= half, sin, 0.0)


def kernel(x, positions, norm_g, ffn_w_gate, ffn_w_up, ffn_w_down, ev_w_in, ev_g_q, ev_w_uq, ev_g_kv, ev_w_ukv, ev_lb_logits, ev_g_out, ev_w_out, od_w_in, od_lambda, od_g_head, od_w_out):
    b, s, d = x.shape
    t = b * s
    xt = x.reshape(t, d)
    tabs_m = _rope_tables(positions, MLA_ROPE, LANES)
    tabs_d = _rope_tables(positions, DF_ROT, DF_DH)
    lb_w = jax.nn.softmax(ev_lb_logits.astype(F32), axis=0)
    lb_all = jnp.cumsum(lb_w, axis=0) - lb_w[0:1]

    n_even = ev_w_in.shape[0]
    o_pe = MLA_Q_RANK + MLA_KV_RANK + MLA_ROPE
    w_in_e = jnp.concatenate(
        [ev_w_in[..., :o_pe].astype(BF16), jnp.zeros((n_even, d, LANES - MLA_ROPE), BF16),
         ev_w_in[..., o_pe:].astype(BF16)], axis=-1)
    w_uq = ev_w_uq.astype(BF16).reshape(n_even, MLA_Q_RANK, MLA_HEADS, MLA_NOPE + MLA_ROPE)
    w_uq = jnp.pad(w_uq, ((0, 0), (0, 0), (0, 0), (0, MLA_QK_PAD - MLA_NOPE - MLA_ROPE)))
    w_uq = w_uq.reshape(n_even, MLA_Q_RANK, MLA_HEADS * MLA_QK_PAD)
    w_ukv = ev_w_ukv.astype(BF16).reshape(n_even, MLA_KV_RANK, MLA_HEADS, 2, MLA_NOPE)
    w_ukv = w_ukv.transpose(0, 1, 3, 2, 4).reshape(n_even, MLA_KV_RANK, 2 * MLA_HEADS * MLA_NOPE)
    w_out_e = ev_w_out.astype(BF16)
    w_in_o = od_w_in.astype(BF16)
    w_out_o = od_w_out.astype(BF16)
    g_q = ev_g_q.reshape(n_even, 1, -1)
    g_kv = ev_g_kv.reshape(n_even, 1, -1)
    g_out = ev_g_out.reshape(n_even, 1, -1)
    lb_all = lb_all.reshape(n_even, 1, -1)
    g_head_o = od_g_head.reshape(od_g_head.shape[0], 1, -1)

    for l in range(DEPTH):
        xt = _ffn_call(xt, _Pick(norm_g, l, 0), _Slabs(ffn_w_gate, l, 0), _Slabs(ffn_w_up, l, 0),
                       _Slabs(ffn_w_down, l, 0))
        j = l // 2
        g_mix = _Pick(norm_g, l, 1)
        if l % 2 == 0:
            q, k, v, zh, kk, bdec = _even_pre_call(
                xt, g_mix, _Pick(w_in_e, j), _Pick(g_q, j), _Pick(w_uq, j), _Pick(g_kv, j),
                _Pick(w_ukv, j), _Pick(lb_all, j), tabs_m)
            o_a = _mla_attn_call(q.reshape(b, s, -1), k.reshape(b, s, -1), v)
            o_b = _hgrn_call(zh, kk, bdec, _Pick(g_out, j), b, s)
            parts = [o_a, o_b]
            head_norm = {}
            w_out = _Pick(w_out_e, j)
        else:
            lambda_init = 0.8 - 0.6 * math.exp(-0.3 * l)
            q, k, v = _odd_pre_call(xt, g_mix, _Pick(w_in_o, j), tabs_d)
            o = _diff_attn_call(q.reshape(b, s, -1), k.reshape(b, s, -1), v, _Pick(od_lambda, j),
                                lambda_init)
            parts = [o]
            head_norm = dict(g_head=_Pick(g_head_o, j), head_scale=1.0 - lambda_init)
            w_out = _Pick(w_out_o, j)
        xt = _mix_ffn_call(xt, parts, w_out, g_mix, _Pick(norm_g, l, 2),
                           _Slabs(ffn_w_gate, l, 1), _Slabs(ffn_w_up, l, 1), _Slabs(ffn_w_down, l, 1),
                           **head_norm)
    return xt.reshape(b, s, d)
```

```python
import functools
import math

import jax
import jax.numpy as jnp
import numpy as np
from jax import lax
from jax.experimental import pallas as pl
from jax.experimental.pallas import tpu as pltpu

D_MODEL = 1024
DEPTH = 4
CHUNK = 64
ROPE_THETA = 500000.0
EPS = 1e-6
NEG_INF = -1e30
LOG2_E = math.log2(math.e)
TINY = 1e-30
D_FF = 2816
MLA_HEADS = 4
MLA_NOPE = 128
MLA_ROPE = 64
MLA_V = 128
MLA_Q_RANK = 384
MLA_KV_RANK = 256
HG_HEADS = 4
HG_DK = 128
HG_DV = 128
DF_HEADS = 8
DF_DH = 64
DF_ROT = DF_DH // 4

LANES = 128
VMEM_LIMIT = 56 * 1024 * 1024

TOKEN_TILE = 1024
MIX_TILE = 512
MLA_Q_TILE = 512
DF_Q_TILE = 256
KEY_BLOCK = 1024
ATTN_HEADS_PER_STEP = 4
HG_SEQ_TILE = 1024
HG_SUB = 16
HG_GATE_ROWS = 128
HG_UNROLL = 4
HG_SAFE_LOG = 75.0
W_CAST_STEPS = 8
MIX_ROW_GROUPS = 2
FFN_ROW_GROUPS = 4
PRE_ROW_GROUPS = 2
FF_CHUNKS = ((0, 1024), (1024, 2048), (2048, D_FF))

MLA_QK_PAD = 2 * LANES
V_ROWS = LANES + 16

F32 = jnp.float32
BF16 = jnp.bfloat16


def _rms(x, g):
    ms = jnp.mean(x * x, axis=-1, keepdims=True)
    return x * lax.rsqrt(ms + EPS) * g


def _dot(a, b):
    return jnp.dot(a, b, preferred_element_type=F32)


def _dot_nt(a, b):
    return lax.dot_general(a, b, (((1,), (1,)), ((), ())), preferred_element_type=F32)


def _dot_tn(a, b):
    return lax.dot_general(a, b, (((0,), (0,)), ((), ())), preferred_element_type=F32)


def _rope_slab(x, c, sa, sb, half):
    return x * c + pltpu.roll(x, LANES - half, 1) * sa + pltpu.roll(x, half, 1) * sb


def _store_values(vt_ref, rs, v, heads):
    ones = jnp.ones((V_ROWS - LANES, v.shape[0]), BF16)
    for hd in range(heads):
        vt_ref[hd, :LANES, rs] = v[:, hd * LANES:(hd + 1) * LANES].T.astype(BF16)
        vt_ref[hd, LANES:, rs] = ones


def _vt_out(heads, t):
    per_blk = KEY_BLOCK // TOKEN_TILE
    spec = pl.BlockSpec((heads, None, V_ROWS, TOKEN_TILE), lambda i: (0, i // per_blk, 0, i % per_blk))
    return spec, jax.ShapeDtypeStruct((heads, t // KEY_BLOCK, V_ROWS, KEY_BLOCK), BF16)


class _Slabs:
    def __init__(self, arr, layer, half):
        self.arr = arr
        rows, cols = arr.shape[2:]
        self.shape = (rows, cols)
        self.spec = pl.BlockSpec((None, None, rows // W_CAST_STEPS, cols),
                                 lambda i: (layer, half, jnp.minimum(i, W_CAST_STEPS - 1), 0))


def _cast_slabs(i, pairs):
    for j in range(W_CAST_STEPS):
        @pl.when(i == j)
        def _():
            for src, dst in pairs:
                rows = src.shape[0]
                dst[j * rows:(j + 1) * rows, :] = src[...].astype(BF16)


def _token_tile(i):
    return jnp.maximum(i - W_CAST_STEPS, 0)


class _Pick:
    def __init__(self, arr, *idx):
        self.arr = arr
        nd = arr.ndim - len(idx)
        self.spec = pl.BlockSpec((None,) * len(idx) + arr.shape[len(idx):],
                                 lambda *_: idx + (0,) * nd, pipeline_mode=pl.Buffered(1))


def _params(*sem):
    return pltpu.CompilerParams(dimension_semantics=sem, vmem_limit_bytes=VMEM_LIMIT)


def _ffn_apply(xs, gpre, gpost, wg_ref, wu_ref, wd_ref):
    xn = [_rms(x, gpre).astype(BF16) for x in xs]
    acc = [None] * len(xs)
    for lo, hi in FF_CHUNKS:
        for i in range(len(xs)):
            g = _dot(xn[i], wg_ref[:, lo:hi])
            u = _dot(xn[i], wu_ref[:, lo:hi])
            a = (jax.nn.silu(g) * u).astype(BF16)
            h = _dot(a, wd_ref[lo:hi, :])
            acc[i] = h if acc[i] is None else acc[i] + h
    return [x + 0.5 * _rms(a, gpost) for x, a in zip(xs, acc)]


def _row_groups(rows, n):
    step = rows // n
    return [slice(i * step, (i + 1) * step) for i in range(n)]


def _ffn_kernel(x_ref, g_ref, wg32_ref, wu32_ref, wd32_ref, o_ref, wg_ref, wu_ref, wd_ref):
    i = pl.program_id(0)
    _cast_slabs(i, [(wg32_ref, wg_ref), (wu32_ref, wu_ref), (wd32_ref, wd_ref)])

    @pl.when(i >= W_CAST_STEPS)
    def _():
        g = g_ref[...]
        groups = _row_groups(TOKEN_TILE, FFN_ROW_GROUPS)
        outs = _ffn_apply([x_ref[rs, :] for rs in groups], g[0:1], g[1:2], wg_ref, wu_ref, wd_ref)
        for rs, o in zip(groups, outs):
            o_ref[rs, :] = o


def _ffn_call(x, g2, wg, wu, wd):
    t = x.shape[0]
    tile = pl.BlockSpec((TOKEN_TILE, D_MODEL), lambda i: (_token_tile(i), 0))
    picks = [g2, wg, wu, wd]
    return pl.pallas_call(
        _ffn_kernel,
        grid=(W_CAST_STEPS + t // TOKEN_TILE,),
        in_specs=[tile] + [p.spec for p in picks],
        out_specs=tile,
        out_shape=jax.ShapeDtypeStruct(x.shape, F32),
        scratch_shapes=[pltpu.VMEM(w.shape, BF16) for w in (wg, wu, wd)],
        compiler_params=_params("arbitrary"),
        name="ffn",
    )(x, *[p.arr for p in picks])


def _mix_ffn_kernel(n_parts, head_scale, *refs):
    x_ref = refs[0]
    part_refs = refs[1:1 + n_parts]
    rest = refs[1 + n_parts:]
    if head_scale is not None:
        gh_ref, rest = rest[0], rest[1:]
    wo_ref, gm_ref, gf_ref, wg32_ref, wu32_ref, wd32_ref, o_ref, wg_ref, wu_ref, wd_ref = rest
    i = pl.program_id(0)
    _cast_slabs(i, [(wg32_ref, wg_ref), (wu32_ref, wu_ref), (wd32_ref, wd_ref)])
    pl.when(i >= W_CAST_STEPS)(functools.partial(
        _mix_ffn_tile, head_scale, x_ref, part_refs, gh_ref if head_scale is not None else None,
        wo_ref, gm_ref, gf_ref, wg_ref, wu_ref, wd_ref, o_ref))


def _mix_ffn_tile(head_scale, x_ref, part_refs, gh_ref, wo_ref, gm_ref, gf_ref, wg_ref, wu_ref, wd_ref, o_ref):
    gf = gf_ref[...]
    groups = _row_groups(MIX_TILE, MIX_ROW_GROUPS)
    xs = []
    for rs in groups:
        cols = []
        for p_ref in part_refs:
            if len(p_ref.shape) == 2:
                cols.append(p_ref[rs, :])
                continue
            for hd in range(p_ref.shape[0]):
                o_t = p_ref[hd, :, rs].astype(F32)
                if head_scale is not None:
                    o_t = o_t * lax.rsqrt(jnp.mean(o_t * o_t, axis=0, keepdims=True) + EPS)
                o = o_t.T
                if head_scale is not None:
                    o = o * (gh_ref[:, hd * LANES:(hd + 1) * LANES] * head_scale)
                cols.append(o.astype(BF16))
        m = _dot(jnp.concatenate(cols, axis=1), wo_ref[...])
        xs.append(x_ref[rs, :] + _rms(m, gm_ref[1:2]))
    outs = _ffn_apply(xs, gf[0:1], gf[1:2], wg_ref, wu_ref, wd_ref)
    for rs, o in zip(groups, outs):
        o_ref[rs, :] = o


def _mix_ffn_call(x, parts, wo, g_mix, g_ffn, wg, wu, wd, g_head=None, head_scale=None):
    t = x.shape[0]
    tile = pl.BlockSpec((MIX_TILE, D_MODEL), lambda i: (_token_tile(i), 0))
    part_specs = [pl.BlockSpec((MIX_TILE, p.shape[-1]), lambda i: (_token_tile(i), 0)) if p.ndim == 2
                  else pl.BlockSpec(p.shape[:2] + (MIX_TILE,), lambda i: (0, 0, _token_tile(i)))
                  for p in parts]
    picks = ([] if g_head is None else [g_head]) + [wo, g_mix, g_ffn, wg, wu, wd]
    return pl.pallas_call(
        functools.partial(_mix_ffn_kernel, len(parts), head_scale),
        grid=(W_CAST_STEPS + t // MIX_TILE,),
        in_specs=[tile] + part_specs + [p.spec for p in picks],
        out_specs=tile,
        out_shape=jax.ShapeDtypeStruct(x.shape, F32),
        scratch_shapes=[pltpu.VMEM(w.shape, BF16) for w in (wg, wu, wd)],
        compiler_params=_params("arbitrary"),
        name="mix_ffn",
    )(x, *parts, *[p.arr for p in picks])


def _even_pre_kernel(x_ref, g_ref, win_ref, gq_ref, wuq_ref, gkv_ref, wukv_ref, lb_ref,
                     c_ref, sa_ref, sb_ref, q_ref, k_ref, v_ref, zh_ref, kk_ref, b_ref):
    o_kv = MLA_Q_RANK
    o_pe = o_kv + MLA_KV_RANK
    o_h = o_pe + LANES
    half = MLA_ROPE // 2
    scale = LOG2_E * (MLA_NOPE + MLA_ROPE) ** -0.5
    for rs in _row_groups(TOKEN_TILE, PRE_ROW_GROUPS):
        h = _rms(x_ref[rs, :], g_ref[0:1]).astype(BF16)
        z = _dot(h, win_ref[...])
        hw = HG_HEADS * HG_DK
        zh_ref[rs, :hw] = z[:, o_h:o_h + hw]
        zh_ref[rs, hw:] = z[:, o_h + 2 * hw:]
        _hgrn_gates(z[:, o_h + hw:o_h + 2 * hw], lb_ref, kk_ref, b_ref, rs)
        c, sa, sb = c_ref[rs, :], sa_ref[rs, :], sb_ref[rs, :]

        cq = _rms(z[:, :o_kv], gq_ref[...]).astype(BF16)
        q = _dot(cq, wuq_ref[...]) * scale
        for hd in range(MLA_HEADS):
            base = hd * MLA_QK_PAD
            q_ref[rs, base:base + LANES] = q[:, base:base + LANES].astype(BF16)
            q_ref[rs, base + LANES:base + 2 * LANES] = _rope_slab(
                q[:, base + LANES:base + 2 * LANES], c, sa, sb, half).astype(BF16)

        ckv = _rms(z[:, o_kv:o_pe], gkv_ref[...]).astype(BF16)
        kv = _dot(ckv, wukv_ref[...])
        kpe = _rope_slab(z[:, o_pe:o_h], c, sa, sb, half).astype(BF16)
        for hd in range(MLA_HEADS):
            base = hd * MLA_QK_PAD
            k_ref[rs, base:base + LANES] = kv[:, hd * LANES:(hd + 1) * LANES].astype(BF16)
            k_ref[rs, base + LANES:base + 2 * LANES] = kpe
        _store_values(v_ref, rs, kv[:, MLA_HEADS * MLA_NOPE:], MLA_HEADS)


def _even_pre_call(x, g, win, gq, wuq, gkv, wukv, lb, tabs):
    t = x.shape[0]
    picks = [g, win, gq, wuq, gkv, wukv, lb]
    row = lambda w: pl.BlockSpec((TOKEN_TILE, w), lambda i: (i, 0))
    qk_w = MLA_HEADS * MLA_QK_PAD
    vt_spec, vt_shape = _vt_out(MLA_HEADS, t)
    zh_w = 3 * HG_HEADS * HG_DK
    gate_spec = pl.BlockSpec((HG_HEADS, TOKEN_TILE, HG_DK), lambda i: (0, i, 0))
    gate_shape = jax.ShapeDtypeStruct((HG_HEADS, t, HG_DK), F32)
    return pl.pallas_call(
        _even_pre_kernel,
        grid=(t // TOKEN_TILE,),
        in_specs=[row(D_MODEL)] + [p.spec for p in picks] + [row(LANES), row(LANES), row(LANES)],
        out_specs=[row(qk_w), row(qk_w), vt_spec, row(zh_w), gate_spec, gate_spec],
        out_shape=[jax.ShapeDtypeStruct((t, qk_w), BF16), jax.ShapeDtypeStruct((t, qk_w), BF16),
                   vt_shape, jax.ShapeDtypeStruct((t, zh_w), F32), gate_shape, gate_shape],
        compiler_params=_params("parallel"),
        name="even_pre",
    )(x, *[p.arr for p in picks], *tabs)


def _odd_pre_kernel(x_ref, g_ref, win_ref, c_ref, sa_ref, sb_ref, q_ref, k_ref, v_ref):
    half = DF_ROT // 2
    width = DF_HEADS * 2 * DF_DH
    scale = LOG2_E * DF_DH ** -0.5
    for rs in _row_groups(TOKEN_TILE, PRE_ROW_GROUPS):
        h = _rms(x_ref[rs, :], g_ref[0:1]).astype(BF16)
        c, sa, sb = c_ref[rs, :], sa_ref[rs, :], sb_ref[rs, :]
        q = _dot(h, win_ref[:, :width]) * scale
        k = _dot(h, win_ref[:, width:2 * width])
        for j in range(width // LANES):
            sl = slice(j * LANES, (j + 1) * LANES)
            q_ref[rs, sl] = _rope_slab(q[:, sl], c, sa, sb, half).astype(BF16)
            k_ref[rs, sl] = _rope_slab(k[:, sl], c, sa, sb, half).astype(BF16)
        _store_values(v_ref, rs, _dot(h, win_ref[:, 2 * width:]), DF_HEADS)


def _odd_pre_call(x, g, win, tabs):
    t = x.shape[0]
    width = DF_HEADS * 2 * DF_DH
    row = lambda w: pl.BlockSpec((TOKEN_TILE, w), lambda i: (i, 0))
    vt_spec, vt_shape = _vt_out(DF_HEADS, t)
    return pl.pallas_call(
        _odd_pre_kernel,
        grid=(t // TOKEN_TILE,),
        in_specs=[row(D_MODEL), g.spec, win.spec, row(LANES), row(LANES), row(LANES)],
        out_specs=[row(width), row(width), vt_spec],
        out_shape=[jax.ShapeDtypeStruct((t, width), BF16), jax.ShapeDtypeStruct((t, width), BF16), vt_shape],
        compiler_params=_params("parallel"),
        name="odd_pre",
    )(x, g.arr, win.arr, *tabs)


def _flash(streams, q_chunk, k_ref, vt_ref, m_ref, acc_ref, row0, tq):
    m_rows = streams[0][0].shape[0]
    for i in range(len(streams)):
        m_ref[i] = jnp.full((1, m_rows), NEG_INF, F32)
        acc_ref[i] = jnp.zeros((V_ROWS, m_rows), F32)

    def step(blk, width, masked):
        start = pl.multiple_of(blk * KEY_BLOCK, KEY_BLOCK)
        scores = [_dot_nt(k_ref[0, pl.ds(start, width), ksl], q) for q, ksl, _ in streams]
        for i, (s, (_, _, hd)) in enumerate(zip(scores, streams)):
            if masked:
                k_chunk = (start + (width - tq) + lax.broadcasted_iota(jnp.int32, (tq, 1), 0)) // CHUNK
                tail = jnp.where(k_chunk <= q_chunk, s[width - tq:], NEG_INF)
                s = tail if width == tq else jnp.concatenate([s[:width - tq], tail], axis=0)
            m_i = m_ref[i]
            m_new = jnp.maximum(m_i, jnp.max(s, axis=0, keepdims=True))
            m_ref[i] = m_new
            p = jnp.exp2((s - m_new).astype(BF16))
            acc_ref[i] = jnp.exp2(m_i - m_new) * acc_ref[i] + _dot(vt_ref[hd, blk, :, :width], p)

    n_full = row0 // KEY_BLOCK

    def unmasked(blk, carry):
        step(blk, KEY_BLOCK, False)
        return carry

    lax.fori_loop(0, n_full, unmasked, 0)
    for p in range(KEY_BLOCK // tq):
        pl.when((row0 % KEY_BLOCK) // tq == p)(functools.partial(step, n_full, tq * (p + 1), True))
    outs = []
    for i in range(len(streams)):
        acc = acc_ref[i]
        outs.append(acc[:LANES] * (1.0 / acc[LANES:LANES + 1]))
    return outs


def _row_chunks(row0, n_rows):
    rows = row0 + lax.broadcasted_iota(jnp.int32, (1, n_rows), 1)
    return rows // CHUNK


def _flash_scratch(hp, m_rows):
    return [pltpu.VMEM((hp, 1, m_rows), F32), pltpu.VMEM((hp, V_ROWS, m_rows), F32)]


def _ot_spec(hp, tq, s):
    return pl.BlockSpec((hp, LANES, tq), lambda bi, h, i: (h, 0, bi * (s // tq) + i))


def _vt_spec(hp, s):
    return pl.BlockSpec((hp, s // KEY_BLOCK, V_ROWS, KEY_BLOCK), lambda bi, h, i: (h, bi, 0, 0))


def _mla_attn_kernel(q_ref, k_ref, vt_ref, o_ref, m_ref, acc_ref):
    row0 = pl.program_id(2) * MLA_Q_TILE
    streams = [(q_ref[0, :, h * MLA_QK_PAD:(h + 1) * MLA_QK_PAD],
                slice(h * MLA_QK_PAD, (h + 1) * MLA_QK_PAD), h)
               for h in range(ATTN_HEADS_PER_STEP)]
    outs = _flash(streams, _row_chunks(row0, MLA_Q_TILE), k_ref, vt_ref, m_ref, acc_ref, row0, MLA_Q_TILE)
    for h, o_t in enumerate(outs):
        o_ref[h] = o_t.astype(BF16)


def _mla_attn_call(q, k, vt):
    b, s, _ = q.shape
    hp = ATTN_HEADS_PER_STEP
    return pl.pallas_call(
        _mla_attn_kernel,
        grid=(b, MLA_HEADS // hp, s // MLA_Q_TILE),
        in_specs=[pl.BlockSpec((1, MLA_Q_TILE, hp * MLA_QK_PAD), lambda bi, h, i: (bi, i, h)),
                  pl.BlockSpec((1, s, hp * MLA_QK_PAD), lambda bi, h, i: (bi, 0, h)),
                  _vt_spec(hp, s)],
        out_specs=_ot_spec(hp, MLA_Q_TILE, s),
        out_shape=jax.ShapeDtypeStruct((MLA_HEADS, MLA_V, b * s), BF16),
        scratch_shapes=_flash_scratch(hp, MLA_Q_TILE),
        compiler_params=_params("parallel", "parallel", "arbitrary"),
        name="mla_attn",
    )(q, k, vt)


def _diff_attn_kernel(lambda_init, q_ref, k_ref, vt_ref, lam_ref, o_ref, m_ref, acc_ref):
    row0 = pl.program_id(2) * DF_Q_TILE
    dv = 2 * DF_DH
    lane = lax.broadcasted_iota(jnp.int32, (1, dv), 1)
    streams = []
    for h in range(ATTN_HEADS_PER_STEP):
        hs = slice(h * dv, (h + 1) * dv)
        q = q_ref[0, :, hs]
        zero = jnp.zeros_like(q)
        q2 = jnp.concatenate([jnp.where(lane < DF_DH, q, zero), jnp.where(lane >= DF_DH, q, zero)], axis=0)
        streams.append((q2, hs, h))
    chunks = _row_chunks(row0, DF_Q_TILE)
    outs = _flash(streams, jnp.concatenate([chunks, chunks], axis=1), k_ref, vt_ref, m_ref, acc_ref, row0,
                  DF_Q_TILE)
    lp = lam_ref[...]
    lam = (jnp.exp(jnp.sum(lp[0:1] * lp[1:2], axis=-1, keepdims=True))
           - jnp.exp(jnp.sum(lp[2:3] * lp[3:4], axis=-1, keepdims=True)) + lambda_init)
    for h, o2_t in enumerate(outs):
        o_ref[h] = (o2_t[:, :DF_Q_TILE] - lam * o2_t[:, DF_Q_TILE:]).astype(BF16)


def _diff_attn_call(q, k, vt, lam_p, lambda_init):
    b, s, _ = q.shape
    hp = ATTN_HEADS_PER_STEP
    w = hp * 2 * DF_DH
    return pl.pallas_call(
        functools.partial(_diff_attn_kernel, lambda_init),
        grid=(b, DF_HEADS // hp, s // DF_Q_TILE),
        in_specs=[pl.BlockSpec((1, DF_Q_TILE, w), lambda bi, h, i: (bi, i, h)),
                  pl.BlockSpec((1, s, w), lambda bi, h, i: (bi, 0, h)),
                  _vt_spec(hp, s),
                  lam_p.spec],
        out_specs=_ot_spec(hp, DF_Q_TILE, s),
        out_shape=jax.ShapeDtypeStruct((DF_HEADS, 2 * DF_DH, b * s), BF16),
        scratch_shapes=_flash_scratch(hp, 2 * DF_Q_TILE),
        compiler_params=_params("parallel", "parallel", "arbitrary"),
        name="diff_attn",
    )(q, k, vt, lam_p.arr)


def _hgrn_gates(zf, lb_ref, kk_ref, b_ref, rs):
    row_in_chunk = lax.broadcasted_iota(jnp.int32, (HG_GATE_ROWS, 1), 0) % CHUNK
    for hd in range(HG_HEADS):
        ls = slice(hd * HG_DK, (hd + 1) * HG_DK)
        lb = lb_ref[:, ls]
        for r0 in range(0, zf.shape[0], HG_GATE_ROWS):
            out_rows = slice(rs.start + r0, rs.start + r0 + HG_GATE_ROWS)
            sig = jax.nn.sigmoid(zf[r0:r0 + HG_GATE_ROWS, ls])
            kk_ref[hd, out_rows, :] = (1.0 - lb) * (1.0 - sig)
            b = jnp.log(jnp.maximum(lb + (1.0 - lb) * sig, TINY))
            step = 1
            while step < CHUNK:
                b = b + jnp.where(row_in_chunk >= step, pltpu.roll(b, step, 0), 0.0)
                step *= 2
            b_ref[hd, out_rows, :] = b


def _hgrn_kernel(hq_ref, hi_ref, hg_ref, kk_ref, b_ref, go_ref, o_ref,
                 st_ref, bpad_ref, kpad_ref, vpad_ref):
    n_chunks = HG_SEQ_TILE // CHUNK
    mid = CHUNK // 2 - 1

    @pl.when(pl.program_id(1) == 0)
    def _():
        st_ref[...] = jnp.zeros_like(st_ref)

    b_mid = b_ref[:, pl.ds(mid, n_chunks, stride=CHUNK), :]
    b_end = b_ref[:, pl.ds(CHUNK - 1, n_chunks, stride=CHUNK), :]
    worst = jnp.max(jnp.maximum(-b_mid, b_mid - b_end))

    row = lax.broadcasted_iota(jnp.int32, (CHUNK, 1), 0)

    def load(ci, hd):
        rows = pl.ds(pl.multiple_of(ci * CHUNK, CHUNK), CHUNK)
        ls = slice(hd * HG_DK, (hd + 1) * HG_DK)
        return rows, ls, hq_ref[rows, ls], hi_ref[rows, ls], b_ref[hd, rows, :], kk_ref[hd, rows, :]

    def carry_state(hd, qh, vh, b, kk):
        st = st_ref[hd]
        o = _dot_nt((qh * jnp.exp(b)).astype(BF16), st.astype(BF16))
        b_last = b[CHUNK - 1:CHUNK]
        kdec = kk * jnp.exp(b_last - b)
        st_ref[hd] = st * jnp.exp(b_last) + _dot_tn(vh.astype(BF16), kdec.astype(BF16))
        return o

    def finish(rows, ls, o):
        on = _rms(o, go_ref[:, ls]) * jax.nn.silu(hg_ref[rows, ls])
        o_ref[rows, ls] = on.astype(BF16)

    @pl.when(worst < HG_SAFE_LOG)
    def _():
        causal = row >= lax.broadcasted_iota(jnp.int32, (1, CHUNK), 1)

        def chunk_body(cj, carry):
            for u in range(HG_UNROLL):
                ci = cj * HG_UNROLL + u
                for hd in range(HG_HEADS):
                    rows, ls, qh, vh, b, kk = load(ci, hd)
                    o = carry_state(hd, qh, vh, b, kk)
                    b_m = b[mid:mid + 1]
                    qf = (qh * jnp.exp(b - b_m)).astype(BF16)
                    kf = (kk * jnp.exp(b_m - b)).astype(BF16)
                    a = jnp.where(causal, _dot_nt(qf, kf), 0.0)
                    finish(rows, ls, o + _dot(a.astype(BF16), vh.astype(BF16)))
            return carry

        lax.fori_loop(0, n_chunks // HG_UNROLL, chunk_body, 0)

    @pl.when(jnp.logical_not(worst < HG_SAFE_LOG))
    def _():
        n_sub = CHUNK // HG_SUB
        off_w = HG_SUB * (n_sub * (n_sub - 1) // 2)
        zpad = jnp.zeros((HG_HEADS, HG_SUB, HG_DK), F32)
        bpad_ref[:, :HG_SUB, :] = zpad
        kpad_ref[:, :HG_SUB, :] = zpad
        vpad_ref[:, :HG_SUB, :] = zpad
        row_sub = row % HG_SUB
        r2 = lax.broadcasted_iota(jnp.int32, (2 * HG_DK, 2 * HG_DK), 0) // HG_DK
        c2 = lax.broadcasted_iota(jnp.int32, (2 * HG_DK, 2 * HG_DK), 1) // HG_DK
        ones2 = jnp.where(r2 == c2, 1.0, 0.0).astype(BF16)
        col = lax.broadcasted_iota(jnp.int32, (1, off_w), 1)
        col_blk = jnp.zeros((1, off_w), jnp.int32)
        for i in range(1, n_sub):
            col_blk = col_blk + jnp.where(col >= HG_SUB * (i * (i - 1) // 2), 1, 0)
        off_mask = col_blk == (row // HG_SUB)

        def chunk_body(ci, carry):
            for hd in range(HG_HEADS):
                rows, ls, qh, vh, b, kk = load(ci, hd)
                o = carry_state(hd, qh, vh, b, kk)

                refs = [b[i * HG_SUB - 1:i * HG_SUB] for i in range(1, n_sub)]
                bref = jnp.concatenate(
                    [jnp.zeros((HG_SUB, HG_DK), F32)]
                    + [jnp.broadcast_to(r, (HG_SUB, HG_DK)) for r in refs], axis=0)
                qs = qh * jnp.exp(b - bref)
                kst = jnp.concatenate(
                    [kk[:i * HG_SUB] * jnp.exp(refs[i - 1] - b[:i * HG_SUB]) for i in range(1, n_sub)],
                    axis=0)
                vst = jnp.concatenate([vh[:i * HG_SUB] for i in range(1, n_sub)], axis=0)
                a_off = jnp.where(off_mask, _dot_nt(qs.astype(BF16), kst.astype(BF16)), 0.0)
                o = o + _dot(a_off.astype(BF16), vst.astype(BF16))

                bpad_ref[hd, HG_SUB:, :] = b
                kpad_ref[hd, HG_SUB:, :] = kk
                vpad_ref[hd, HG_SUB:, :] = vh
                for dp in range(HG_SUB // 2):
                    terms = []
                    for d in (2 * dp, 2 * dp + 1):
                        lo = HG_SUB - d
                        w = qh * jnp.exp(b - bpad_ref[hd, lo:lo + CHUNK, :]) * kpad_ref[hd, lo:lo + CHUNK, :]
                        terms.append(jnp.where(row_sub >= d, w, 0.0))
                    dsum = _dot(jnp.concatenate(terms, axis=1).astype(BF16), ones2)
                    for j, d in enumerate((2 * dp, 2 * dp + 1)):
                        lo = HG_SUB - d
                        o = o + dsum[:, j * HG_DK:(j + 1) * HG_DK] * vpad_ref[hd, lo:lo + CHUNK, :]
                finish(rows, ls, o)
            return carry

        lax.fori_loop(0, n_chunks, chunk_body, 0)


def _hgrn_call(zh, kk, bdec, g_out, b, s):
    width = HG_HEADS * HG_DK
    n_seq = s // HG_SEQ_TILE
    part = lambda j: pl.BlockSpec((HG_SEQ_TILE, width), lambda bi, si: (bi * n_seq + si, j))
    gate = pl.BlockSpec((HG_HEADS, HG_SEQ_TILE, HG_DK), lambda bi, si: (0, bi * n_seq + si, 0))
    pad = pltpu.VMEM((HG_HEADS, HG_SUB + CHUNK, HG_DK), F32)
    return pl.pallas_call(
        _hgrn_kernel,
        grid=(b, n_seq),
        in_specs=[part(0), part(1), part(2), gate, gate, g_out.spec],
        out_specs=pl.BlockSpec((HG_SEQ_TILE, width), lambda bi, si: (bi * n_seq + si, 0)),
        out_shape=jax.ShapeDtypeStruct((b * s, width), BF16),
        scratch_shapes=[pltpu.VMEM((HG_HEADS, HG_DV, HG_DK), F32), pad, pad, pad],
        compiler_params=_params("parallel", "arbitrary"),
        name="hgrn2",
    )(zh, zh, zh, kk, bdec, g_out.arr)


def _rope_tables(positions, dim, group):
    half = dim // 2
    inv_freq = ROPE_THETA ** (-jnp.arange(0, dim, 2, dtype=F32) / dim)
    ang = positions.astype(F32).reshape(-1, 1) * inv_freq
    lane = np.arange(LANES) % group
    same_freq = (lane[None, :] % half == np.arange(half)[:, None]) & (lane[None, :] < dim)

    def place(a, lanes):
        return jnp.dot(a, jnp.asarray(same_freq & lanes, F32), precision=lax.Precision.HIGHEST)

    sin = jnp.sin(ang)
    c = place(jnp.cos(ang), lane >= 0) + jnp.asarray(lane >= dim, F32)
    return c, place(-sin, lane < half), place(sin, lane >= half)


def kernel(x, positions, norm_g, ffn_w_gate, ffn_w_up, ffn_w_down, ev_w_in, ev_g_q, ev_w_uq, ev_g_kv, ev_w_ukv, ev_lb_logits, ev_g_out, ev_w_out, od_w_in, od_lambda, od_g_head, od_w_out):
    b, s, d = x.shape
    t = b * s
    xt = x.reshape(t, d)
    tabs_m = _rope_tables(positions, MLA_ROPE, LANES)
    tabs_d = _rope_tables(positions, DF_ROT, DF_DH)
    lb_w = jax.nn.softmax(ev_lb_logits.astype(F32), axis=0)
    lb_all = jnp.cumsum(lb_w, axis=0) - lb_w[0:1]

    n_even = ev_w_in.shape[0]
    o_pe = MLA_Q_RANK + MLA_KV_RANK + MLA_ROPE
    w_in_e = jnp.concatenate(
        [ev_w_in[..., :o_pe].astype(BF16), jnp.zeros((n_even, d, LANES - MLA_ROPE), BF16),
         ev_w_in[..., o_pe:].astype(BF16)], axis=-1)
    w_uq = ev_w_uq.astype(BF16).reshape(n_even, MLA_Q_RANK, MLA_HEADS, MLA_NOPE + MLA_ROPE)
    w_uq = jnp.pad(w_uq, ((0, 0), (0, 0), (0, 0), (0, MLA_QK_PAD - MLA_NOPE - MLA_ROPE)))
    w_uq = w_uq.reshape(n_even, MLA_Q_RANK, MLA_HEADS * MLA_QK_PAD)
    w_ukv = ev_w_ukv.astype(BF16).reshape(n_even, MLA_KV_RANK, MLA_HEADS, 2, MLA_NOPE)
    w_ukv = w_ukv.transpose(0, 1, 3, 2, 4).reshape(n_even, MLA_KV_RANK, 2 * MLA_HEADS * MLA_NOPE)
    w_out_e = ev_w_out.astype(BF16)
    w_in_o = od_w_in.astype(BF16)
    w_out_o = od_w_out.astype(BF16)
    g_q = ev_g_q.reshape(n_even, 1, -1)
    g_kv = ev_g_kv.reshape(n_even, 1, -1)
    g_out = ev_g_out.reshape(n_even, 1, -1)
    lb_all = lb_all.reshape(n_even, 1, -1)
    g_head_o = od_g_head.reshape(od_g_head.shape[0], 1, -1)

    for l in range(DEPTH):
        xt = _ffn_call(xt, _Pick(norm_g, l, 0), _Slabs(ffn_w_gate, l, 0), _Slabs(ffn_w_up, l, 0),
                       _Slabs(ffn_w_down, l, 0))
        j = l // 2
        g_mix = _Pick(norm_g, l, 1)
        if l % 2 == 0:
            q, k, v, zh, kk, bdec = _even_pre_call(
                xt, g_mix, _Pick(w_in_e, j), _Pick(g_q, j), _Pick(w_uq, j), _Pick(g_kv, j),
                _Pick(w_ukv, j), _Pick(lb_all, j), tabs_m)
            o_a = _mla_attn_call(q.reshape(b, s, -1), k.reshape(b, s, -1), v)
            o_b = _hgrn_call(zh, kk, bdec, _Pick(g_out, j), b, s)
            parts = [o_a, o_b]
            head_norm = {}
            w_out = _Pick(w_out_e, j)
        else:
            lambda_init = 0.8 - 0.6 * math.exp(-0.3 * l)
            q, k, v = _odd_pre_call(xt, g_mix, _Pick(w_in_o, j), tabs_d)
            o = _diff_attn_call(q.reshape(b, s, -1), k.reshape(b, s, -1), v, _Pick(od_lambda, j),
                                lambda_init)
            parts = [o]
            head_norm = dict(g_head=_Pick(g_head_o, j), head_scale=1.0 - lambda_init)
            w_out = _Pick(w_out_o, j)
        xt = _mix_ffn_call(xt, parts, w_out, g_mix, _Pick(norm_g, l, 2),
                           _Slabs(ffn_w_gate, l, 1), _Slabs(ffn_w_up, l, 1), _Slabs(ffn_w_down, l, 1),
                           **head_norm)
    return xt.reshape(b, s, d)
```

```python
import functools
import math

import jax
import jax.numpy as jnp
import numpy as np
from jax import lax
from jax.experimental import pallas as pl
from jax.experimental.pallas import tpu as pltpu

D_MODEL = 1024
DEPTH = 4
CHUNK = 64
ROPE_THETA = 500000.0
EPS = 1e-6
NEG_INF = -1e30
LOG2_E = math.log2(math.e)
TINY = 1e-30
D_FF = 2816
MLA_HEADS = 4
MLA_NOPE = 128
MLA_ROPE = 64
MLA_V = 128
MLA_Q_RANK = 384
MLA_KV_RANK = 256
HG_HEADS = 4
HG_DK = 128
HG_DV = 128
DF_HEADS = 8
DF_DH = 64
DF_ROT = DF_DH // 4

LANES = 128
VMEM_LIMIT = 56 * 1024 * 1024

TOKEN_TILE = 1024
MIX_TILE = 512
MLA_Q_TILE = 512
DF_Q_TILE = 256
KEY_BLOCK = 1024
ATTN_HEADS_PER_STEP = 4
HG_SEQ_TILE = 1024
HG_SUB = 16
HG_GATE_ROWS = 128
HG_UNROLL = 4
HG_SAFE_LOG = 75.0
W_CAST_STEPS = 8
MIX_ROW_GROUPS = 2
FFN_ROW_GROUPS = 4
PRE_ROW_GROUPS = 2
FF_CHUNKS = ((0, 1024), (1024, 2048), (2048, D_FF))

MLA_QK_PAD = 2 * LANES
V_ROWS = LANES + 16

F32 = jnp.float32
BF16 = jnp.bfloat16


def _rms(x, g):
    ms = jnp.mean(x * x, axis=-1, keepdims=True)
    return x * lax.rsqrt(ms + EPS) * g


def _dot(a, b):
    return jnp.dot(a, b, preferred_element_type=F32)


def _dot_nt(a, b):
    return lax.dot_general(a, b, (((1,), (1,)), ((), ())), preferred_element_type=F32)


def _dot_tn(a, b):
    return lax.dot_general(a, b, (((0,), (0,)), ((), ())), preferred_element_type=F32)


def _rope_slab(x, c, sa, sb, half):
    return x * c + pltpu.roll(x, LANES - half, 1) * sa + pltpu.roll(x, half, 1) * sb


def _store_values(vt_ref, rs, v, heads):
    ones = jnp.ones((V_ROWS - LANES, v.shape[0]), BF16)
    for hd in range(heads):
        vt_ref[hd, :LANES, rs] = v[:, hd * LANES:(hd + 1) * LANES].T.astype(BF16)
        vt_ref[hd, LANES:, rs] = ones


def _vt_out(heads, t):
    per_blk = KEY_BLOCK // TOKEN_TILE
    spec = pl.BlockSpec((heads, None, V_ROWS, TOKEN_TILE), lambda i: (0, i // per_blk, 0, i % per_blk))
    return spec, jax.ShapeDtypeStruct((heads, t // KEY_BLOCK, V_ROWS, KEY_BLOCK), BF16)


class _Slabs:
    def __init__(self, arr, layer, half):
        self.arr = arr
        rows, cols = arr.shape[2:]
        self.shape = (rows, cols)
        self.spec = pl.BlockSpec((None, None, rows // W_CAST_STEPS, cols),
                                 lambda i: (layer, half, jnp.minimum(i, W_CAST_STEPS - 1), 0))


def _cast_slabs(i, pairs):
    for j in range(W_CAST_STEPS):
        @pl.when(i == j)
        def _():
            for src, dst in pairs:
                rows = src.shape[0]
                dst[j * rows:(j + 1) * rows, :] = src[...].astype(BF16)


def _token_tile(i):
    return jnp.maximum(i - W_CAST_STEPS, 0)


class _Pick:
    def __init__(self, arr, *idx):
        self.arr = arr
        nd = arr.ndim - len(idx)
        self.spec = pl.BlockSpec((None,) * len(idx) + arr.shape[len(idx):],
                                 lambda *_: idx + (0,) * nd, pipeline_mode=pl.Buffered(1))


def _params(*sem):
    return pltpu.CompilerParams(dimension_semantics=sem, vmem_limit_bytes=VMEM_LIMIT)


def _ffn_apply(xs, gpre, gpost, wg_ref, wu_ref, wd_ref):
    xn = [_rms(x, gpre).astype(BF16) for x in xs]
    acc = [None] * len(xs)
    for lo, hi in FF_CHUNKS:
        for i in range(len(xs)):
            g = _dot(xn[i], wg_ref[:, lo:hi])
            u = _dot(xn[i], wu_ref[:, lo:hi])
            a = (jax.nn.silu(g) * u).astype(BF16)
            h = _dot(a, wd_ref[lo:hi, :])
            acc[i] = h if acc[i] is None else acc[i] + h
    return [x + 0.5 * _rms(a, gpost) for x, a in zip(xs, acc)]


def _row_groups(rows, n):
    step = rows // n
    return [slice(i * step, (i + 1) * step) for i in range(n)]


def _ffn_kernel(x_ref, g_ref, wg32_ref, wu32_ref, wd32_ref, o_ref, wg_ref, wu_ref, wd_ref):
    i = pl.program_id(0)
    _cast_slabs(i, [(wg32_ref, wg_ref), (wu32_ref, wu_ref), (wd32_ref, wd_ref)])

    @pl.when(i >= W_CAST_STEPS)
    def _():
        g = g_ref[...]
        groups = _row_groups(TOKEN_TILE, FFN_ROW_GROUPS)
        outs = _ffn_apply([x_ref[rs, :] for rs in groups], g[0:1], g[1:2], wg_ref, wu_ref, wd_ref)
        for rs, o in zip(groups, outs):
            o_ref[rs, :] = o


def _ffn_call(x, g2, wg, wu, wd):
    t = x.shape[0]
    tile = pl.BlockSpec((TOKEN_TILE, D_MODEL), lambda i: (_token_tile(i), 0))
    picks = [g2, wg, wu, wd]
    return pl.pallas_call(
        _ffn_kernel,
        grid=(W_CAST_STEPS + t // TOKEN_TILE,),
        in_specs=[tile] + [p.spec for p in picks],
        out_specs=tile,
        out_shape=jax.ShapeDtypeStruct(x.shape, F32),
        scratch_shapes=[pltpu.VMEM(w.shape, BF16) for w in (wg, wu, wd)],
        compiler_params=_params("arbitrary"),
        name="ffn",
    )(x, *[p.arr for p in picks])


def _mix_ffn_kernel(n_parts, head_scale, *refs):
    x_ref = refs[0]
    part_refs = refs[1:1 + n_parts]
    rest = refs[1 + n_parts:]
    if head_scale is not None:
        gh_ref, rest = rest[0], rest[1:]
    wo_ref, gm_ref, gf_ref, wg32_ref, wu32_ref, wd32_ref, o_ref, wg_ref, wu_ref, wd_ref = rest
    i = pl.program_id(0)
    _cast_slabs(i, [(wg32_ref, wg_ref), (wu32_ref, wu_ref), (wd32_ref, wd_ref)])
    pl.when(i >= W_CAST_STEPS)(functools.partial(
        _mix_ffn_tile, head_scale, x_ref, part_refs, gh_ref if head_scale is not None else None,
        wo_ref, gm_ref, gf_ref, wg_ref, wu_ref, wd_ref, o_ref))


def _mix_ffn_tile(head_scale, x_ref, part_refs, gh_ref, wo_ref, gm_ref, gf_ref, wg_ref, wu_ref, wd_ref, o_ref):
    gf = gf_ref[...]
    groups = _row_groups(MIX_TILE, MIX_ROW_GROUPS)
    xs = []
    for rs in groups:
        cols = []
        for p_ref in part_refs:
            if len(p_ref.shape) == 2:
                cols.append(p_ref[rs, :])
                continue
            for hd in range(p_ref.shape[0]):
                o_t = p_ref[hd, :, rs].astype(F32)
                if head_scale is not None:
                    o_t = o_t * lax.rsqrt(jnp.mean(o_t * o_t, axis=0, keepdims=True) + EPS)
                o = o_t.T
                if head_scale is not None:
                    o = o * (gh_ref[:, hd * LANES:(hd + 1) * LANES] * head_scale)
                cols.append(o.astype(BF16))
        m = _dot(jnp.concatenate(cols, axis=1), wo_ref[...])
        xs.append(x_ref[rs, :] + _rms(m, gm_ref[1:2]))
    outs = _ffn_apply(xs, gf[0:1], gf[1:2], wg_ref, wu_ref, wd_ref)
    for rs, o in zip(groups, outs):
        o_ref[rs, :] = o


def _mix_ffn_call(x, parts, wo, g_mix, g_ffn, wg, wu, wd, g_head=None, head_scale=None):
    t = x.shape[0]
    tile = pl.BlockSpec((MIX_TILE, D_MODEL), lambda i: (_token_tile(i), 0))
    part_specs = [pl.BlockSpec((MIX_TILE, p.shape[-1]), lambda i: (_token_tile(i), 0)) if p.ndim == 2
                  else pl.BlockSpec(p.shape[:2] + (MIX_TILE,), lambda i: (0, 0, _token_tile(i)))
                  for p in parts]
    picks = ([] if g_head is None else [g_head]) + [wo, g_mix, g_ffn, wg, wu, wd]
    return pl.pallas_call(
        functools.partial(_mix_ffn_kernel, len(parts), head_scale),
        grid=(W_CAST_STEPS + t // MIX_TILE,),
        in_specs=[tile] + part_specs + [p.spec for p in picks],
        out_specs=tile,
        out_shape=jax.ShapeDtypeStruct(x.shape, F32),
        scratch_shapes=[pltpu.VMEM(w.shape, BF16) for w in (wg, wu, wd)],
        compiler_params=_params("arbitrary"),
        name="mix_ffn",
    )(x, *parts, *[p.arr for p in picks])


def _even_pre_kernel(x_ref, g_ref, win_ref, gq_ref, wuq_ref, gkv_ref, wukv_ref, lb_ref,
                     c_ref, sa_ref, sb_ref, q_ref, k_ref, v_ref, zh_ref, kk_ref, b_ref):
    o_kv = MLA_Q_RANK
    o_pe = o_kv + MLA_KV_RANK
    o_h = o_pe + LANES
    half = MLA_ROPE // 2
    scale = LOG2_E * (MLA_NOPE + MLA_ROPE) ** -0.5
    for rs in _row_groups(TOKEN_TILE, PRE_ROW_GROUPS):
        h = _rms(x_ref[rs, :], g_ref[0:1]).astype(BF16)
        z = _dot(h, win_ref[...])
        hw = HG_HEADS * HG_DK
        zh_ref[rs, :hw] = z[:, o_h:o_h + hw]
        zh_ref[rs, hw:] = z[:, o_h + 2 * hw:]
        _hgrn_gates(z[:, o_h + hw:o_h + 2 * hw], lb_ref, kk_ref, b_ref, rs)
        c, sa, sb = c_ref[rs, :], sa_ref[rs, :], sb_ref[rs, :]

        cq = _rms(z[:, :o_kv], gq_ref[...]).astype(BF16)
        q = _dot(cq, wuq_ref[...]) * scale
        for hd in range(MLA_HEADS):
            base = hd * MLA_QK_PAD
            q_ref[rs, base:base + LANES] = q[:, base:base + LANES].astype(BF16)
            q_ref[rs, base + LANES:base + 2 * LANES] = _rope_slab(
                q[:, base + LANES:base + 2 * LANES], c, sa, sb, half).astype(BF16)

        ckv = _rms(z[:, o_kv:o_pe], gkv_ref[...]).astype(BF16)
        kv = _dot(ckv, wukv_ref[...])
        kpe = _rope_slab(z[:, o_pe:o_h], c, sa, sb, half).astype(BF16)
        for hd in range(MLA_HEADS):
            base = hd * MLA_QK_PAD
            k_ref[rs, base:base + LANES] = kv[:, hd * LANES:(hd + 1) * LANES].astype(BF16)
            k_ref[rs, base + LANES:base + 2 * LANES] = kpe
        _store_values(v_ref, rs, kv[:, MLA_HEADS * MLA_NOPE:], MLA_HEADS)


def _even_pre_call(x, g, win, gq, wuq, gkv, wukv, lb, tabs):
    t = x.shape[0]
    picks = [g, win, gq, wuq, gkv, wukv, lb]
    row = lambda w: pl.BlockSpec((TOKEN_TILE, w), lambda i: (i, 0))
    qk_w = MLA_HEADS * MLA_QK_PAD
    vt_spec, vt_shape = _vt_out(MLA_HEADS, t)
    zh_w = 3 * HG_HEADS * HG_DK
    gate_spec = pl.BlockSpec((HG_HEADS, TOKEN_TILE, HG_DK), lambda i: (0, i, 0))
    gate_shape = jax.ShapeDtypeStruct((HG_HEADS, t, HG_DK), F32)
    return pl.pallas_call(
        _even_pre_kernel,
        grid=(t // TOKEN_TILE,),
        in_specs=[row(D_MODEL)] + [p.spec for p in picks] + [row(LANES), row(LANES), row(LANES)],
        out_specs=[row(qk_w), row(qk_w), vt_spec, row(zh_w), gate_spec, gate_spec],
        out_shape=[jax.ShapeDtypeStruct((t, qk_w), BF16), jax.ShapeDtypeStruct((t, qk_w), BF16),
                   vt_shape, jax.ShapeDtypeStruct((t, zh_w), F32), gate_shape, gate_shape],
        compiler_params=_params("parallel"),
        name="even_pre",
    )(x, *[p.arr for p in picks], *tabs)


def _odd_pre_kernel(x_ref, g_ref, win_ref, c_ref, sa_ref, sb_ref, q_ref, k_ref, v_ref):
    half = DF_ROT // 2
    width = DF_HEADS * 2 * DF_DH
    scale = LOG2_E * DF_DH ** -0.5
    for rs in _row_groups(TOKEN_TILE, PRE_ROW_GROUPS):
        h = _rms(x_ref[rs, :], g_ref[0:1]).astype(BF16)
        c, sa, sb = c_ref[rs, :], sa_ref[rs, :], sb_ref[rs, :]
        q = _dot(h, win_ref[:, :width]) * scale
        k = _dot(h, win_ref[:, width:2 * width])
        for j in range(width // LANES):
            sl = slice(j * LANES, (j + 1) * LANES)
            q_ref[rs, sl] = _rope_slab(q[:, sl], c, sa, sb, half).astype(BF16)
            k_ref[rs, sl] = _rope_slab(k[:, sl], c, sa, sb, half).astype(BF16)
        _store_values(v_ref, rs, _dot(h, win_ref[:, 2 * width:]), DF_HEADS)


def _odd_pre_call(x, g, win, tabs):
    t = x.shape[0]
    width = DF_HEADS * 2 * DF_DH
    row = lambda w: pl.BlockSpec((TOKEN_TILE, w), lambda i: (i, 0))
    vt_spec, vt_shape = _vt_out(DF_HEADS, t)
    return pl.pallas_call(
        _odd_pre_kernel,
        grid=(t // TOKEN_TILE,),
        in_specs=[row(D_MODEL), g.spec, win.spec, row(LANES), row(LANES), row(LANES)],
        out_specs=[row(width), row(width), vt_spec],
        out_shape=[jax.ShapeDtypeStruct((t, width), BF16), jax.ShapeDtypeStruct((t, width), BF16), vt_shape],
        compiler_params=_params("parallel"),
        name="odd_pre",
    )(x, g.arr, win.arr, *tabs)


def _flash(streams, q_chunk, k_ref, vt_ref, m_ref, acc_ref, row0, tq):
    m_rows = streams[0][0].shape[0]
    for i in range(len(streams)):
        m_ref[i] = jnp.full((1, m_rows), NEG_INF, F32)
        acc_ref[i] = jnp.zeros((V_ROWS, m_rows), F32)

    def step(blk, width, masked):
        start = pl.multiple_of(blk * KEY_BLOCK, KEY_BLOCK)
        scores = [_dot_nt(k_ref[0, pl.ds(start, width), ksl], q) for q, ksl, _ in streams]
        for i, (s, (_, _, hd)) in enumerate(zip(scores, streams)):
            if masked:
                k_chunk = (start + (width - tq) + lax.broadcasted_iota(jnp.int32, (tq, 1), 0)) // CHUNK
                tail = jnp.where(k_chunk <= q_chunk, s[width - tq:], NEG_INF)
                s = tail if width == tq else jnp.concatenate([s[:width - tq], tail], axis=0)
            m_i = m_ref[i]
            m_new = jnp.maximum(m_i, jnp.max(s, axis=0, keepdims=True))
            m_ref[i] = m_new
            p = jnp.exp2((s - m_new).astype(BF16))
            acc_ref[i] = jnp.exp2(m_i - m_new) * acc_ref[i] + _dot(vt_ref[hd, blk, :, :width], p)

    n_full = row0 // KEY_BLOCK

    def unmasked(blk, carry):
        step(blk, KEY_BLOCK, False)
        return carry

    lax.fori_loop(0, n_full, unmasked, 0)
    for p in range(KEY_BLOCK // tq):
        pl.when((row0 % KEY_BLOCK) // tq == p)(functools.partial(step, n_full, tq * (p + 1), True))
    outs = []
    for i in range(len(streams)):
        acc = acc_ref[i]
        outs.append(acc[:LANES] * (1.0 / acc[LANES:LANES + 1]))
    return outs


def _row_chunks(row0, n_rows):
    rows = row0 + lax.broadcasted_iota(jnp.int32, (1, n_rows), 1)
    return rows // CHUNK


def _flash_scratch(hp, m_rows):
    return [pltpu.VMEM((hp, 1, m_rows), F32), pltpu.VMEM((hp, V_ROWS, m_rows), F32)]


def _ot_spec(hp, tq, s):
    return pl.BlockSpec((hp, LANES, tq), lambda bi, h, i: (h, 0, bi * (s // tq) + i))


def _vt_spec(hp, s):
    return pl.BlockSpec((hp, s // KEY_BLOCK, V_ROWS, KEY_BLOCK), lambda bi, h, i: (h, bi, 0, 0))


def _mla_attn_kernel(q_ref, k_ref, vt_ref, o_ref, m_ref, acc_ref):
    row0 = pl.program_id(2) * MLA_Q_TILE
    streams = [(q_ref[0, :, h * MLA_QK_PAD:(h + 1) * MLA_QK_PAD],
                slice(h * MLA_QK_PAD, (h + 1) * MLA_QK_PAD), h)
               for h in range(ATTN_HEADS_PER_STEP)]
    outs = _flash(streams, _row_chunks(row0, MLA_Q_TILE), k_ref, vt_ref, m_ref, acc_ref, row0, MLA_Q_TILE)
    for h, o_t in enumerate(outs):
        o_ref[h] = o_t.astype(BF16)


def _mla_attn_call(q, k, vt):
    b, s, _ = q.shape
    hp = ATTN_HEADS_PER_STEP
    return pl.pallas_call(
        _mla_attn_kernel,
        grid=(b, MLA_HEADS // hp, s // MLA_Q_TILE),
        in_specs=[pl.BlockSpec((1, MLA_Q_TILE, hp * MLA_QK_PAD), lambda bi, h, i: (bi, i, h)),
                  pl.BlockSpec((1, s, hp * MLA_QK_PAD), lambda bi, h, i: (bi, 0, h)),
                  _vt_spec(hp, s)],
        out_specs=_ot_spec(hp, MLA_Q_TILE, s),
        out_shape=jax.ShapeDtypeStruct((MLA_HEADS, MLA_V, b * s), BF16),
        scratch_shapes=_flash_scratch(hp, MLA_Q_TILE),
        compiler_params=_params("parallel", "parallel", "arbitrary"),
        name="mla_attn",
    )(q, k, vt)


def _diff_attn_kernel(lambda_init, q_ref, k_ref, vt_ref, lam_ref, o_ref, m_ref, acc_ref):
    row0 = pl.program_id(2) * DF_Q_TILE
    dv = 2 * DF_DH
    lane = lax.broadcasted_iota(jnp.int32, (1, dv), 1)
    streams = []
    for h in range(ATTN_HEADS_PER_STEP):
        hs = slice(h * dv, (h + 1) * dv)
        q = q_ref[0, :, hs]
        zero = jnp.zeros_like(q)
        streams.append((jnp.where(lane < DF_DH, q, zero), hs, h))
        streams.append((jnp.where(lane >= DF_DH, q, zero), hs, h))
    outs = _flash(streams, _row_chunks(row0, DF_Q_TILE), k_ref, vt_ref, m_ref, acc_ref, row0, DF_Q_TILE)
    lp = lam_ref[...]
    lam = (jnp.exp(jnp.sum(lp[0:1] * lp[1:2], axis=-1, keepdims=True))
           - jnp.exp(jnp.sum(lp[2:3] * lp[3:4], axis=-1, keepdims=True)) + lambda_init)
    for h in range(ATTN_HEADS_PER_STEP):
        o_ref[h] = (outs[2 * h] - lam * outs[2 * h + 1]).astype(BF16)


def _diff_attn_call(q, k, vt, lam_p, lambda_init):
    b, s, _ = q.shape
    hp = ATTN_HEADS_PER_STEP
    w = hp * 2 * DF_DH
    return pl.pallas_call(
        functools.partial(_diff_attn_kernel, lambda_init),
        grid=(b, DF_HEADS // hp, s // DF_Q_TILE),
        in_specs=[pl.BlockSpec((1, DF_Q_TILE, w), lambda bi, h, i: (bi, i, h)),
                  pl.BlockSpec((1, s, w), lambda bi, h, i: (bi, 0, h)),
                  _vt_spec(hp, s),
                  lam_p.spec],
        out_specs=_ot_spec(hp, DF_Q_TILE, s),
        out_shape=jax.ShapeDtypeStruct((DF_HEADS, 2 * DF_DH, b * s), BF16),
        scratch_shapes=_flash_scratch(2 * hp, DF_Q_TILE),
        compiler_params=_params("parallel", "parallel", "arbitrary"),
        name="diff_attn",
    )(q, k, vt, lam_p.arr)


def _hgrn_gates(zf, lb_ref, kk_ref, b_ref, rs):
    row_in_chunk = lax.broadcasted_iota(jnp.int32, (HG_GATE_ROWS, 1), 0) % CHUNK
    for hd in range(HG_HEADS):
        ls = slice(hd * HG_DK, (hd + 1) * HG_DK)
        lb = lb_ref[:, ls]
        for r0 in range(0, zf.shape[0], HG_GATE_ROWS):
            out_rows = slice(rs.start + r0, rs.start + r0 + HG_GATE_ROWS)
            sig = jax.nn.sigmoid(zf[r0:r0 + HG_GATE_ROWS, ls])
            kk_ref[hd, out_rows, :] = (1.0 - lb) * (1.0 - sig)
            b = jnp.log(jnp.maximum(lb + (1.0 - lb) * sig, TINY))
            step = 1
            while step < CHUNK:
                b = b + jnp.where(row_in_chunk >= step, pltpu.roll(b, step, 0), 0.0)
                step *= 2
            b_ref[hd, out_rows, :] = b


def _hgrn_kernel(hq_ref, hi_ref, hg_ref, kk_ref, b_ref, go_ref, o_ref,
                 st_ref, bpad_ref, kpad_ref, vpad_ref):
    n_chunks = HG_SEQ_TILE // CHUNK
    mid = CHUNK // 2 - 1

    @pl.when(pl.program_id(1) == 0)
    def _():
        st_ref[...] = jnp.zeros_like(st_ref)

    b_mid = b_ref[:, pl.ds(mid, n_chunks, stride=CHUNK), :]
    b_end = b_ref[:, pl.ds(CHUNK - 1, n_chunks, stride=CHUNK), :]
    worst = jnp.max(jnp.maximum(-b_mid, b_mid - b_end))

    row = lax.broadcasted_iota(jnp.int32, (CHUNK, 1), 0)

    def load(ci, hd):
        rows = pl.ds(pl.multiple_of(ci * CHUNK, CHUNK), CHUNK)
        ls = slice(hd * HG_DK, (hd + 1) * HG_DK)
        return rows, ls, hq_ref[rows, ls], hi_ref[rows, ls], b_ref[hd, rows, :], kk_ref[hd, rows, :]

    def carry_state(hd, qh, vh, b, kk):
        st = st_ref[hd]
        o = _dot_nt((qh * jnp.exp(b)).astype(BF16), st.astype(BF16))
        b_last = b[CHUNK - 1:CHUNK]
        kdec = kk * jnp.exp(b_last - b)
        st_ref[hd] = st * jnp.exp(b_last) + _dot_tn(vh.astype(BF16), kdec.astype(BF16))
        return o

    def finish(rows, ls, o):
        on = _rms(o, go_ref[:, ls]) * jax.nn.silu(hg_ref[rows, ls])
        o_ref[rows, ls] = on.astype(BF16)

    @pl.when(worst < HG_SAFE_LOG)
    def _():
        causal = row >= lax.broadcasted_iota(jnp.int32, (1, CHUNK), 1)

        def chunk_body(cj, carry):
            for u in range(HG_UNROLL):
                ci = cj * HG_UNROLL + u
                for hd in range(HG_HEADS):
                    rows, ls, qh, vh, b, kk = load(ci, hd)
                    o = carry_state(hd, qh, vh, b, kk)
                    b_m = b[mid:mid + 1]
                    qf = (qh * jnp.exp(b - b_m)).astype(BF16)
                    kf = (kk * jnp.exp(b_m - b)).astype(BF16)
                    a = jnp.where(causal, _dot_nt(qf, kf), 0.0)
                    finish(rows, ls, o + _dot(a.astype(BF16), vh.astype(BF16)))
            return carry

        lax.fori_loop(0, n_chunks // HG_UNROLL, chunk_body, 0)

    @pl.when(jnp.logical_not(worst < HG_SAFE_LOG))
    def _():
        n_sub = CHUNK // HG_SUB
        off_w = HG_SUB * (n_sub * (n_sub - 1) // 2)
        zpad = jnp.zeros((HG_HEADS, HG_SUB, HG_DK), F32)
        bpad_ref[:, :HG_SUB, :] = zpad
        kpad_ref[:, :HG_SUB, :] = zpad
        vpad_ref[:, :HG_SUB, :] = zpad
        row_sub = row % HG_SUB
        r2 = lax.broadcasted_iota(jnp.int32, (2 * HG_DK, 2 * HG_DK), 0) // HG_DK
        c2 = lax.broadcasted_iota(jnp.int32, (2 * HG_DK, 2 * HG_DK), 1) // HG_DK
        ones2 = jnp.where(r2 == c2, 1.0, 0.0).astype(BF16)
        col = lax.broadcasted_iota(jnp.int32, (1, off_w), 1)
        col_blk = jnp.zeros((1, off_w), jnp.int32)
        for i in range(1, n_sub):
            col_blk = col_blk + jnp.where(col >= HG_SUB * (i * (i - 1) // 2), 1, 0)
        off_mask = col_blk == (row // HG_SUB)

        def chunk_body(ci, carry):
            for hd in range(HG_HEADS):
                rows, ls, qh, vh, b, kk = load(ci, hd)
                o = carry_state(hd, qh, vh, b, kk)

                refs = [b[i * HG_SUB - 1:i * HG_SUB] for i in range(1, n_sub)]
                bref = jnp.concatenate(
                    [jnp.zeros((HG_SUB, HG_DK), F32)]
                    + [jnp.broadcast_to(r, (HG_SUB, HG_DK)) for r in refs], axis=0)
                qs = qh * jnp.exp(b - bref)
                kst = jnp.concatenate(
                    [kk[:i * HG_SUB] * jnp.exp(refs[i - 1] - b[:i * HG_SUB]) for i in range(1, n_sub)],
                    axis=0)
                vst = jnp.concatenate([vh[:i * HG_SUB] for i in range(1, n_sub)], axis=0)
                a_off = jnp.where(off_mask, _dot_nt(qs.astype(BF16), kst.astype(BF16)), 0.0)
                o = o + _dot(a_off.astype(BF16), vst.astype(BF16))

                bpad_ref[hd, HG_SUB:, :] = b
                kpad_ref[hd, HG_SUB:, :] = kk
                vpad_ref[hd, HG_SUB:, :] = vh
                for dp in range(HG_SUB // 2):
                    terms = []
                    for d in (2 * dp, 2 * dp + 1):
                        lo = HG_SUB - d
                        w = qh * jnp.exp(b - bpad_ref[hd, lo:lo + CHUNK, :]) * kpad_ref[hd, lo:lo + CHUNK, :]
                        terms.append(jnp.where(row_sub >= d, w, 0.0))
                    dsum = _dot(jnp.concatenate(terms, axis=1).astype(BF16), ones2)
                    for j, d in enumerate((2 * dp, 2 * dp + 1)):
                        lo = HG_SUB - d
                        o = o + dsum[:, j * HG_DK:(j + 1) * HG_DK] * vpad_ref[hd, lo:lo + CHUNK, :]
                finish(rows, ls, o)
            return carry

        lax.fori_loop(0, n_chunks, chunk_body, 0)


def _hgrn_call(zh, kk, bdec, g_out, b, s):
    width = HG_HEADS * HG_DK
    n_seq = s // HG_SEQ_TILE
    part = lambda j: pl.BlockSpec((HG_SEQ_TILE, width), lambda bi, si: (bi * n_seq + si, j))
    gate = pl.BlockSpec((HG_HEADS, HG_SEQ_TILE, HG_DK), lambda bi, si: (0, bi * n_seq + si, 0))
    pad = pltpu.VMEM((HG_HEADS, HG_SUB + CHUNK, HG_DK), F32)
    return pl.pallas_call(
        _hgrn_kernel,
        grid=(b, n_seq),
        in_specs=[part(0), part(1), part(2), gate, gate, g_out.spec],
        out_specs=pl.BlockSpec((HG_SEQ_TILE, width), lambda bi, si: (bi * n_seq + si, 0)),
        out_shape=jax.ShapeDtypeStruct((b * s, width), BF16),
        scratch_shapes=[pltpu.VMEM((HG_HEADS, HG_DV, HG_DK), F32), pad, pad, pad],
        compiler_params=_params("parallel", "arbitrary"),
        name="hgrn2",
    )(zh, zh, zh, kk, bdec, g_out.arr)


def _rope_tables(positions, dim, group):
    half = dim // 2
    inv_freq = ROPE_THETA ** (-jnp.arange(0, dim, 2, dtype=F32) / dim)
    lane = np.arange(LANES) % group
    freq = jnp.where(lane < dim, inv_freq[lane % half], 0.0)
    ang = positions.astype(F32).reshape(-1, 1) * freq
    sin = jnp.sin(ang)
    return jnp.cos(ang), jnp.where(lane < half, -sin, 0.0), jnp.where(lane >= half, sin, 0.0)


def kernel(x, positions, norm_g, ffn_w_gate, ffn_w_up, ffn_w_down, ev_w_in, ev_g_q, ev_w_uq, ev_g_kv, ev_w_ukv, ev_lb_logits, ev_g_out, ev_w_out, od_w_in, od_lambda, od_g_head, od_w_out):
    b, s, d = x.shape
    t = b * s
    xt = x.reshape(t, d)
    tabs_m = _rope_tables(positions, MLA_ROPE, LANES)
    tabs_d = _rope_tables(positions, DF_ROT, DF_DH)
    lb_w = jax.nn.softmax(ev_lb_logits.astype(F32), axis=0)
    lb_all = jnp.cumsum(lb_w, axis=0) - lb_w[0:1]

    n_even = ev_w_in.shape[0]
    o_pe = MLA_Q_RANK + MLA_KV_RANK + MLA_ROPE
    w_in_e = jnp.concatenate(
        [ev_w_in[..., :o_pe].astype(BF16), jnp.zeros((n_even, d, LANES - MLA_ROPE), BF16),
         ev_w_in[..., o_pe:].astype(BF16)], axis=-1)
    w_uq = ev_w_uq.astype(BF16).reshape(n_even, MLA_Q_RANK, MLA_HEADS, MLA_NOPE + MLA_ROPE)
    w_uq = jnp.pad(w_uq, ((0, 0), (0, 0), (0, 0), (0, MLA_QK_PAD - MLA_NOPE - MLA_ROPE)))
    w_uq = w_uq.reshape(n_even, MLA_Q_RANK, MLA_HEADS * MLA_QK_PAD)
    w_ukv = ev_w_ukv.astype(BF16).reshape(n_even, MLA_KV_RANK, MLA_HEADS, 2, MLA_NOPE)
    w_ukv = w_ukv.transpose(0, 1, 3, 2, 4).reshape(n_even, MLA_KV_RANK, 2 * MLA_HEADS * MLA_NOPE)
    w_out_e = ev_w_out.astype(BF16)
    w_in_o = od_w_in.astype(BF16)
    w_out_o = od_w_out.astype(BF16)
    g_q = ev_g_q.reshape(n_even, 1, -1)
    g_kv = ev_g_kv.reshape(n_even, 1, -1)
    g_out = ev_g_out.reshape(n_even, 1, -1)
    lb_all = lb_all.reshape(n_even, 1, -1)
    g_head_o = od_g_head.reshape(od_g_head.shape[0], 1, -1)

    for l in range(DEPTH):
        xt = _ffn_call(xt, _Pick(norm_g, l, 0), _Slabs(ffn_w_gate, l, 0), _Slabs(ffn_w_up, l, 0),
                       _Slabs(ffn_w_down, l, 0))
        j = l // 2
        g_mix = _Pick(norm_g, l, 1)
        if l % 2 == 0:
            q, k, v, zh, kk, bdec = _even_pre_call(
                xt, g_mix, _Pick(w_in_e, j), _Pick(g_q, j), _Pick(w_uq, j), _Pick(g_kv, j),
                _Pick(w_ukv, j), _Pick(lb_all, j), tabs_m)
            o_a = _mla_attn_call(q.reshape(b, s, -1), k.reshape(b, s, -1), v)
            o_b = _hgrn_call(zh, kk, bdec, _Pick(g_out, j), b, s)
            parts = [o_a, o_b]
            head_norm = {}
            w_out = _Pick(w_out_e, j)
        else:
            lambda_init = 0.8 - 0.6 * math.exp(-0.3 * l)
            q, k, v = _odd_pre_call(xt, g_mix, _Pick(w_in_o, j), tabs_d)
            o = _diff_attn_call(q.reshape(b, s, -1), k.reshape(b, s, -1), v, _Pick(od_lambda, j),
                                lambda_init)
            parts = [o]
            head_norm = dict(g_head=_Pick(g_head_o, j), head_scale=1.0 - lambda_init)
            w_out = _Pick(w_out_o, j)
        xt = _mix_ffn_call(xt, parts, w_out, g_mix, _Pick(norm_g, l, 2),
                           _Slabs(ffn_w_gate, l, 1), _Slabs(ffn_w_up, l, 1), _Slabs(ffn_w_down, l, 1),
                           **head_norm)
    return xt.reshape(b, s, d)
```
